```python
import math
import jax, jax.numpy as jnp
from jax import lax
import numpy as np

D_MODEL = 1024
BATCH = 16
SEQ = 4096
DEPTH = 1
DEC_BATCH = 16
DEC_SEQ = 32
PAST_LEN = 4096

CHUNK = 64
N_HEADS_A = 8
HEAD_DIM_A = 128
D_ATTN = N_HEADS_A * HEAD_DIM_A
Q_BLOCK = 128
FORGET_BIAS_INIT = 3.0
D_INNER = D_MODEL
HEAD_DIM_S = 64
N_HEADS_S = D_INNER // HEAD_DIM_S
N_GROUPS_S = 2
D_STATE = 128
CONV_W = 4
CONV_DIM = D_INNER + 2 * N_GROUPS_S * D_STATE
SSD_CHUNK = CHUNK
MIX_W = D_ATTN + D_INNER
N_EXPERTS = 32
TOP_K = 4
D_FF = D_MODEL
SWIGLU_LIMIT = 7.0
SWIGLU_ALPHA = 1.702
MOE_BLOCK = 128
EPS = 1e-5
Q_END = D_ATTN
K_END = 2 * D_ATTN
V_END = 3 * D_ATTN
F_END = V_END + N_HEADS_A
Z_END = F_END + D_INNER
XBC_END = Z_END + CONV_DIM
IN_COLS = XBC_END + N_HEADS_S

kernel_name = 'hymba_fox_ssd_moe_stream_step'


def rmsnorm(x, g):
    xf = x.astype(jnp.float32)
    xf = xf * lax.rsqrt(jnp.mean(xf * xf, axis=-1, keepdims=True) + EPS)
    return (xf * g.astype(jnp.float32)).astype(x.dtype)


def fox_attention(q, k, v, cum, n_past):
    b, L, H, Dh = q.shape
    qb = min(Q_BLOCK, L)
    nq = L // qb
    cum_q = cum[:, n_past:]
    cum_k = jnp.swapaxes(cum, 1, 2)
    k_pos = jnp.arange(k.shape[1])
    scale = 1.0 / math.sqrt(Dh)

    def one_block(args):
        q_blk, cq_blk, i = args
        q_pos = n_past + i * qb + jnp.arange(qb)
        s = jnp.einsum('bqhd,bkhd->bhqk', q_blk, k).astype(jnp.float32) * scale
        s = s + jnp.swapaxes(cq_blk, 1, 2).astype(jnp.float32)[..., None] - cum_k[:, :, None, :]
        s = jnp.where(k_pos[None, :] <= q_pos[:, None], s, -jnp.inf)
        p = jax.nn.softmax(s, axis=-1).astype(v.dtype)
        return jnp.einsum('bhqk,bkhd->bqhd', p, v)

    q_blocks = jnp.swapaxes(q.reshape(b, nq, qb, H, Dh), 0, 1)
    c_blocks = jnp.swapaxes(cum_q.reshape(b, nq, qb, H), 0, 1)
    out = lax.map(one_block, (q_blocks, c_blocks, jnp.arange(nq)))
    return jnp.swapaxes(out, 0, 1).reshape(b, L, H * Dh)


def ssd_scan(x, dt, A, Bm, Cm, h0, chunk):
    b, L, H, P = x.shape
    G, N = Bm.shape[2], Bm.shape[3]
    R = H // G
    nc = L // chunk
    xc = x.reshape(b, nc, chunk, G, R, P)
    dtc = dt.reshape(b, nc, chunk, G, R)
    Bc = Bm.reshape(b, nc, chunk, G, N)
    Cc = Cm.reshape(b, nc, chunk, G, N)
    a_cs = jnp.cumsum(dtc * A.reshape(G, R), axis=2)
    xdt = xc * dtc[..., None]
    causal = jnp.tril(jnp.ones((chunk, chunk), dtype=bool))[:, :, None, None]
    seg = a_cs[:, :, :, None] - a_cs[:, :, None, :]
    decay = jnp.where(causal, jnp.exp(jnp.where(causal, seg, 0.0)), 0.0)
    cb = jnp.einsum('bclgn,bcsgn->bclsg', Cc, Bc)
    y_diag = jnp.einsum('bclsgr,bcsgrp->bclgrp', cb[..., None] * decay, xdt)
    decay_to_end = jnp.exp(a_cs[:, :, -1:] - a_cs)
    st = jnp.einsum('bclgn,bclgrp->bcgrpn', Bc, xdt * decay_to_end[..., None])
    chunk_decay = jnp.exp(a_cs[:, :, -1])

    def step(h, inp):
        dec, s_c = inp
        return dec[..., None, None] * h + s_c, h

    h_last, h_prev = lax.scan(step, h0.reshape(b, G, R, P, N),
                              (jnp.swapaxes(chunk_decay, 0, 1), jnp.swapaxes(st, 0, 1)))
    h_prev = jnp.swapaxes(h_prev, 0, 1)
    y_off = jnp.einsum('bclgn,bcgrpn->bclgrp', Cc, h_prev) * jnp.exp(a_cs)[..., None]
    return (y_diag + y_off).reshape(b, L, H, P), h_last.reshape(b, H, P, N)


def ssd_mixer(z, xbc, dt_raw, conv_past, ssm_past, conv_w, conv_b, dt_bias, a_log, d_skip, norm_w):
    b, L, _ = xbc.shape
    xbc_ext = jnp.concatenate([conv_past.astype(xbc.dtype), xbc], axis=1)
    conv_new = xbc_ext[:, xbc_ext.shape[1] - (CONV_W - 1):]
    u = lax.conv_general_dilated(xbc_ext, conv_w.astype(xbc.dtype)[:, None, :], (1,), 'VALID',
                                 dimension_numbers=('NWC', 'WIO', 'NWC'), feature_group_count=CONV_DIM)
    u = jax.nn.silu(u + conv_b)
    gn = N_GROUPS_S * D_STATE
    xs = u[..., :D_INNER].reshape(b, L, N_HEADS_S, HEAD_DIM_S).astype(jnp.float32)
    Bm = u[..., D_INNER:D_INNER + gn].reshape(b, L, N_GROUPS_S, D_STATE).astype(jnp.float32)
    Cm = u[..., D_INNER + gn:].reshape(b, L, N_GROUPS_S, D_STATE).astype(jnp.float32)
    dt = jax.nn.softplus((dt_raw + dt_bias).astype(jnp.float32))
    A = -jnp.exp(a_log.astype(jnp.float32))
    y, h_last = ssd_scan(xs, dt, A, Bm, Cm, ssm_past.astype(jnp.float32), min(SSD_CHUNK, L))
    y = y + d_skip.astype(jnp.float32)[:, None] * xs
    g = y.reshape(b, L, D_INNER) * jax.nn.silu(z.astype(jnp.float32))
    g = g.reshape(b, L, N_GROUPS_S, D_INNER // N_GROUPS_S)
    g = g * lax.rsqrt(jnp.mean(g * g, axis=-1, keepdims=True) + EPS)
    out = (g.reshape(b, L, D_INNER) * norm_w.astype(jnp.float32)).astype(z.dtype)
    return out, conv_new, h_last


def moe_ffn(h, router_w, router_b, w_gate_up, b_gate_up, w_down, b_down):
    T, D = h.shape
    logits = (h @ router_w).astype(jnp.float32) + router_b.astype(jnp.float32)
    top_logit, top_e = lax.top_k(logits, TOP_K)
    gates = jax.nn.softmax(top_logit, axis=-1)
    flat_e = top_e.reshape(-1)
    flat_tok = jnp.repeat(jnp.arange(T, dtype=jnp.int32), TOP_K)
    flat_gate = gates.reshape(-1)
    order = jnp.argsort(flat_e)
    e_sorted = flat_e[order]
    counts = jnp.bincount(flat_e, length=N_EXPERTS).astype(jnp.int32)
    padded = (counts + MOE_BLOCK - 1) // MOE_BLOCK * MOE_BLOCK
    ends_p = jnp.cumsum(padded)
    starts_p = ends_p - padded
    starts = jnp.cumsum(counts) - counts
    rank = jnp.arange(T * TOP_K, dtype=jnp.int32) - starts[e_sorted]
    dest = starts_p[e_sorted] + rank
    n_slots = (T * TOP_K + N_EXPERTS * (MOE_BLOCK - 1) + MOE_BLOCK - 1) // MOE_BLOCK * MOE_BLOCK
    n_blocks = n_slots // MOE_BLOCK
    slot_tok = jnp.full((n_slots,), T, jnp.int32).at[dest].set(flat_tok[order])
    slot_gate = jnp.zeros((n_slots,), jnp.float32).at[dest].set(flat_gate[order])
    block_e = jnp.minimum(jnp.searchsorted(ends_p, jnp.arange(n_blocks, dtype=jnp.int32) * MOE_BLOCK,
                                           side='right'), N_EXPERTS - 1)
    h_pad = jnp.concatenate([h, jnp.zeros((1, D), h.dtype)], axis=0)
    xs = h_pad[slot_tok].reshape(n_blocks, MOE_BLOCK, D)

    def expert_block(args):
        xb, e = args
        gu = xb @ w_gate_up[e] + b_gate_up[e]
        gate = jnp.minimum(gu[:, :D_FF], SWIGLU_LIMIT)
        up = jnp.clip(gu[:, D_FF:], -SWIGLU_LIMIT, SWIGLU_LIMIT)
        act = (up + 1.0) * gate * jax.nn.sigmoid(SWIGLU_ALPHA * gate)
        return act @ w_down[e] + b_down[e]

    out = lax.map(expert_block, (xs, block_e)).reshape(n_slots, D)
    y = jax.ops.segment_sum(out.astype(jnp.float32) * slot_gate[:, None], slot_tok, num_segments=T + 1)[:T]
    return y.astype(h.dtype)


def layer_step(x, c, k_past, v_past, logf_past, conv_past, ssm_past, p):
    b, L, _ = x.shape
    mod = jax.nn.silu(c) @ p['w_ada'] + p['b_ada']
    sh_a, sc_a, gt_a, sh_m, sc_m, gt_m = jnp.split(mod, 6, axis=-1)
    h = rmsnorm(x, p['g_mix']) * (1.0 + sc_a[:, None]) + sh_a[:, None]
    proj = h @ p['w_in']
    q = proj[..., :Q_END].reshape(b, L, N_HEADS_A, HEAD_DIM_A)
    k = proj[..., Q_END:K_END].reshape(b, L, N_HEADS_A, HEAD_DIM_A)
    v = proj[..., K_END:V_END].reshape(b, L, N_HEADS_A, HEAD_DIM_A)
    logf = jax.nn.log_sigmoid((proj[..., V_END:F_END] + p['b_f']).astype(jnp.float32))
    z = proj[..., F_END:Z_END]
    xbc = proj[..., Z_END:XBC_END]
    dt_raw = proj[..., XBC_END:]
    n_past = k_past.shape[1]
    k_all = jnp.concatenate([k_past.astype(k.dtype), k], axis=1)
    v_all = jnp.concatenate([v_past.astype(v.dtype), v], axis=1)
    cum = jnp.cumsum(jnp.concatenate([logf_past.astype(jnp.float32), logf], axis=1), axis=1)
    o_a = fox_attention(q, k_all, v_all, cum, n_past)
    o_s, conv_new, ssm_new = ssd_mixer(z, xbc, dt_raw, conv_past, ssm_past, p['conv_w'], p['conv_b'],
                                       p['dt_bias'], p['a_log'], p['d_skip'], p['ssd_norm_w'])
    x = x + gt_a[:, None] * (jnp.concatenate([o_a, o_s], axis=-1) @ p['w_out'])
    h2 = rmsnorm(x, p['g_ffn']) * (1.0 + sc_m[:, None]) + sh_m[:, None]
    ffn = moe_ffn(h2.reshape(b * L, D_MODEL), p['router_w'], p['router_b'], p['w_gate_up'],
                  p['b_gate_up'], p['w_down'], p['b_down']).reshape(b, L, D_MODEL)
    x = x + gt_m[:, None] * ffn
    return x, k, v, logf.astype(x.dtype), conv_new, ssm_new.astype(x.dtype)


def setup_inputs(seed: int = 0) -> dict:
    key = jax.random.key(seed)
    ks = jax.random.split(key, 29)
    nrm = lambda i, shape, scale: jax.random.normal(ks[i], shape, jnp.float32) * scale
    dt0 = jnp.exp(jax.random.uniform(ks[16], (DEPTH, N_HEADS_S), jnp.float32,
                                     math.log(1e-3), math.log(1e-1)))
    return {
        'x_prompt': nrm(0, (BATCH, SEQ, D_MODEL), 1.0),
        'x_sample': nrm(1, (DEC_BATCH, DEC_SEQ, D_MODEL), 1.0),
        'cache_k': nrm(2, (DEPTH, DEC_BATCH, PAST_LEN, N_HEADS_A, HEAD_DIM_A), 1.0),
        'cache_v': nrm(3, (DEPTH, DEC_BATCH, PAST_LEN, N_HEADS_A, HEAD_DIM_A), 1.0),
        'cache_logf': jax.nn.log_sigmoid(FORGET_BIAS_INIT + nrm(4, (DEPTH, DEC_BATCH, PAST_LEN, N_HEADS_A), 1.0)),
        'state_conv': nrm(5, (DEPTH, DEC_BATCH, CONV_W - 1, CONV_DIM), 1.0),
        'state_ssm': nrm(6, (DEPTH, DEC_BATCH, N_HEADS_S, HEAD_DIM_S, D_STATE), 0.1),
        'c_prompt': nrm(7, (BATCH, D_MODEL), 1.0),
        'c_sample': nrm(8, (DEC_BATCH, D_MODEL), 1.0),
        'w_ada': nrm(9, (DEPTH, D_MODEL, 6 * D_MODEL), 0.5 * D_MODEL ** -0.5),
        'b_ada': nrm(10, (DEPTH, 6 * D_MODEL), 0.02),
        'g_mix': 1.0 + nrm(11, (DEPTH, D_MODEL), 0.05),
        'w_in': nrm(12, (DEPTH, D_MODEL, IN_COLS), D_MODEL ** -0.5),
        'b_f': FORGET_BIAS_INIT + nrm(13, (DEPTH, N_HEADS_A), 0.1),
        'conv_w': nrm(14, (DEPTH, CONV_W, CONV_DIM), CONV_W ** -0.5),
        'conv_b': nrm(15, (DEPTH, CONV_DIM), 0.02),
        'dt_bias': dt0 + jnp.log(-jnp.expm1(-dt0)),
        'a_log': jnp.log(jax.random.uniform(ks[17], (DEPTH, N_HEADS_S), jnp.float32, 1.0, 16.0)),
        'd_skip': 1.0 + nrm(18, (DEPTH, N_HEADS_S), 0.1),
        'ssd_norm_w': 1.0 + nrm(19, (DEPTH, D_INNER), 0.05),
        'w_out': nrm(20, (DEPTH, MIX_W, D_MODEL), MIX_W ** -0.5),
        'g_ffn': 1.0 + nrm(21, (DEPTH, D_MODEL), 0.05),
        'router_w': nrm(22, (DEPTH, D_MODEL, N_EXPERTS), D_MODEL ** -0.5),
        'router_b': nrm(23, (DEPTH, N_EXPERTS), 0.01),
        'w_gate_up': nrm(24, (DEPTH, N_EXPERTS, D_MODEL, 2 * D_FF), D_MODEL ** -0.5),
        'b_gate_up': nrm(25, (DEPTH, N_EXPERTS, 2 * D_FF), 0.01),
        'w_down': nrm(26, (DEPTH, N_EXPERTS, D_FF, D_MODEL), D_FF ** -0.5),
        'b_down': nrm(27, (DEPTH, N_EXPERTS, D_MODEL), 0.01),
        'g_final': 1.0 + nrm(28, (D_MODEL,), 0.05),
    }


def reference(x_prompt, x_sample, cache_k, cache_v, cache_logf, state_conv, state_ssm, c_prompt, c_sample,
              w_ada, b_ada, g_mix, w_in, b_f, conv_w, conv_b, dt_bias, a_log, d_skip, ssd_norm_w, w_out,
              g_ffn, router_w, router_b, w_gate_up, b_gate_up, w_down, b_down, g_final):
    bp = x_prompt.shape[0]
    xp, xs = x_prompt, x_sample
    kp_l, vp_l, lfp_l, cvp_l, ssp_l = [], [], [], [], []
    ks_l, vs_l, lfs_l, cvs_l, sss_l = [], [], [], [], []
    for l in range(DEPTH):
        p = {'w_ada': w_ada[l], 'b_ada': b_ada[l], 'g_mix': g_mix[l], 'w_in': w_in[l], 'b_f': b_f[l],
             'conv_w': conv_w[l], 'conv_b': conv_b[l], 'dt_bias': dt_bias[l], 'a_log': a_log[l],
             'd_skip': d_skip[l], 'ssd_norm_w': ssd_norm_w[l], 'w_out': w_out[l], 'g_ffn': g_ffn[l],
             'router_w': router_w[l], 'router_b': router_b[l], 'w_gate_up': w_gate_up[l],
             'b_gate_up': b_gate_up[l], 'w_down': w_down[l], 'b_down': b_down[l]}
        xp, kp, vp, lfp, cvp, ssp = layer_step(
            xp, c_prompt,
            jnp.zeros((bp, 0, N_HEADS_A, HEAD_DIM_A), xp.dtype),
            jnp.zeros((bp, 0, N_HEADS_A, HEAD_DIM_A), xp.dtype),
            jnp.zeros((bp, 0, N_HEADS_A), jnp.float32),
            jnp.zeros((bp, CONV_W - 1, CONV_DIM), xp.dtype),
            jnp.zeros((bp, N_HEADS_S, HEAD_DIM_S, D_STATE), jnp.float32), p)
        xs, ksn, vsn, lfs, cvs, sss = layer_step(
            xs, c_sample, cache_k[l], cache_v[l], cache_logf[l], state_conv[l], state_ssm[l], p)
        kp_l.append(kp); vp_l.append(vp); lfp_l.append(lfp); cvp_l.append(cvp); ssp_l.append(ssp)
        ks_l.append(ksn); vs_l.append(vsn); lfs_l.append(lfs); cvs_l.append(cvs); sss_l.append(sss)
    y_prompt = rmsnorm(xp, g_final)
    y_sample = rmsnorm(xs, g_final)
    return (y_prompt, y_sample,
            jnp.stack(kp_l), jnp.stack(vp_l), jnp.stack(lfp_l), jnp.stack(cvp_l), jnp.stack(ssp_l),
            jnp.stack(ks_l), jnp.stack(vs_l), jnp.stack(lfs_l), jnp.stack(cvs_l), jnp.stack(sss_l))
```

```python
import functools
import math

import jax
import jax.numpy as jnp
from jax import lax
from jax.experimental import pallas as pl
from jax.experimental.pallas import tpu as pltpu

F32 = jnp.float32
BF16 = jnp.bfloat16
HIGHEST = lax.Precision.HIGHEST

D_MODEL = 1024
N_HEADS_A = 8
HEAD_DIM_A = 128
D_ATTN = N_HEADS_A * HEAD_DIM_A
D_INNER = 1024
HEAD_DIM_S = 64
N_HEADS_S = D_INNER // HEAD_DIM_S
N_GROUPS_S = 2
D_STATE = 128
CONV_W = 4
CONV_DIM = D_INNER + 2 * N_GROUPS_S * D_STATE
N_EXPERTS = 32
TOP_K = 4
D_FF = 1024
SWIGLU_LIMIT = 7.0
SWIGLU_ALPHA = 1.702
EPS = 1e-5
LANES = 128
SUBLANES = 8
VMEM_LIMIT = 56 * 1024 * 1024

ROW_TILE = 512
ATT_TILE = 512
DEC_KV_TILE = 512
CUM_TILE = 256
SSD_CHUNK = 128
ROUTE_TILE = 256
MOE_ROWS = 512
COMBINE_TILE = 256


def _params(sem, vmem=VMEM_LIMIT):
    return pltpu.CompilerParams(dimension_semantics=sem, vmem_limit_bytes=vmem)


def _row_blocking(b, l, tile):
    if l >= tile:
        assert l % tile == 0
        return 1, tile
    nb = min(b, max(1, tile // l))
    while b % nb:
        nb -= 1
    return nb, l


def _split3(x):
    hi = x.astype(BF16)
    r1 = x - hi.astype(F32)
    mid = r1.astype(BF16)
    lo = (r1 - mid.astype(F32)).astype(BF16)
    return hi, mid, lo


def _tri_dot(tri, x):
    hi, mid, lo = _split3(x)
    d = lambda p: jnp.dot(tri, p, preferred_element_type=F32)
    return d(hi) + d(mid) + d(lo)


def _dot_tri(x, tri):
    hi, mid, lo = _split3(x)
    d = lambda p: jnp.dot(p, tri, preferred_element_type=F32)
    return d(hi) + d(mid) + d(lo)


def _silu(x):
    return x * jax.nn.sigmoid(x)


def _softplus(x):
    return jnp.maximum(x, 0.0) + jnp.log1p(jnp.exp(-jnp.abs(x)))


def _log_sigmoid(x):
    return jnp.minimum(x, 0.0) - jnp.log1p(jnp.exp(-jnp.abs(x)))


def _adaln_kernel(c_ref, w_ref, b_ref, o_ref):
    s = _silu(c_ref[...])
    o_ref[...] = jnp.dot(s, w_ref[...], precision=HIGHEST, preferred_element_type=F32) + b_ref[...]


def _adaln(c, w, b):
    m, d = c.shape
    n = w.shape[1]
    tn = 512
    return pl.pallas_call(
        _adaln_kernel,
        grid=(n // tn,),
        in_specs=[pl.BlockSpec((m, d), lambda j: (0, 0)),
                  pl.BlockSpec((d, tn), lambda j: (0, j)),
                  pl.BlockSpec((1, tn), lambda j: (0, j))],
        out_specs=pl.BlockSpec((m, tn), lambda j: (0, j)),
        out_shape=jax.ShapeDtypeStruct((m, n), F32),
        compiler_params=_params(("parallel",)),
    )(c, w, b.reshape(1, n))


def _inproj_kernel(x_ref, g_ref, sc_ref, sh_ref, wq_ref, wk_ref, wv_ref, wz_ref, wx_ref, ws_ref,
                   q_ref, kf_ref, vf_ref, kb_ref, vb_ref, z_ref, xbc_ref, sm_ref):
    x = x_ref[...]
    nb, tl, d = x.shape
    ms = jnp.mean(x * x, axis=-1, keepdims=True)
    h = x * lax.rsqrt(ms + EPS) * g_ref[...]
    h = h * (1.0 + sc_ref[...]) + sh_ref[...]
    hb = h.reshape(nb * tl, d).astype(BF16)

    def mm(w_ref):
        return jnp.dot(hb, w_ref[...], preferred_element_type=F32)

    q = mm(wq_ref) * (1.0 / math.sqrt(HEAD_DIM_A))
    q_ref[...] = q.astype(BF16).reshape(nb, tl, -1)
    k = mm(wk_ref)
    kf_ref[...] = k.reshape(nb, tl, -1)
    kb_ref[...] = k.astype(BF16).reshape(nb, tl, -1)
    v = mm(wv_ref)
    vf_ref[...] = v.reshape(nb, tl, -1)
    vb_ref[...] = v.astype(BF16).reshape(nb, tl, -1)
    z_ref[...] = mm(wz_ref).reshape(nb, tl, -1)
    xbc_ref[...] = mm(wx_ref).reshape(nb, tl, -1)
    sm_ref[...] = mm(ws_ref).reshape(nb, tl, -1)


def _inproj(x, g, sc, sh, wq, wk, wv, wz, wx, ws):
    b, l, d = x.shape
    nb, tl = _row_blocking(b, l, ROW_TILE)
    grid = (b // nb, l // tl)
    row = lambda n: pl.BlockSpec((nb, tl, n), lambda i, j: (i, j, 0))
    mod = pl.BlockSpec((nb, 1, d), lambda i, j: (i, 0, 0))
    wspec = lambda w: pl.BlockSpec(w.shape, lambda i, j: (0, 0), pipeline_mode=pl.Buffered(1))
    outs = [(D_ATTN, BF16), (D_ATTN, F32), (D_ATTN, F32), (D_ATTN, BF16), (D_ATTN, BF16),
            (D_INNER, F32), (CONV_DIM, F32), (LANES, F32)]
    return pl.pallas_call(
        _inproj_kernel,
        grid=grid,
        in_specs=[row(d), pl.BlockSpec((1, 1, d), lambda i, j: (0, 0, 0)), mod, mod,
                  wspec(wq), wspec(wk), wspec(wv), wspec(wz), wspec(wx), wspec(ws)],
        out_specs=[row(n) for n, _ in outs],
        out_shape=[jax.ShapeDtypeStruct((b, l, n), dt) for n, dt in outs],
        compiler_params=_params(("parallel", "parallel")),
    )(x, g.reshape(1, 1, d), sc.reshape(b, 1, d), sh.reshape(b, 1, d), wq, wk, wv, wz, wx, ws)


def _cumsum_kernel(v_ref, bias_ref, carry_ref, logf_ref, cum_ref, cumt_ref, carry_scr, *, apply_logsig):
    @pl.when(pl.program_id(1) == 0)
    def _():
        carry_scr[...] = carry_ref[0]

    v = v_ref[0]
    tl = v.shape[0]
    lf = _log_sigmoid(v + bias_ref[...]) if apply_logsig else v
    row = lax.broadcasted_iota(jnp.int32, (tl, tl), 0)
    col = lax.broadcasted_iota(jnp.int32, (tl, tl), 1)
    tri = (row >= col).astype(BF16)
    cs = _tri_dot(tri, lf) + carry_scr[...]
    logf_ref[0] = lf
    cum_ref[0] = cs
    cumt_ref[0] = cs.T[:N_HEADS_A, :]
    carry_scr[...] = cs[tl - 1:tl, :]


def _cumsum(vals, bias, carry, apply_logsig):
    b, l, n = vals.shape
    tl = min(CUM_TILE, l)
    assert l % tl == 0 and tl % LANES == 0
    return pl.pallas_call(
        functools.partial(_cumsum_kernel, apply_logsig=apply_logsig),
        grid=(b, l // tl),
        in_specs=[pl.BlockSpec((1, tl, n), lambda i, j: (i, j, 0)),
                  pl.BlockSpec((1, n), lambda i, j: (0, 0)),
                  pl.BlockSpec((1, 1, n), lambda i, j: (i, 0, 0))],
        out_specs=[pl.BlockSpec((1, tl, n), lambda i, j: (i, j, 0)),
                   pl.BlockSpec((1, tl, n), lambda i, j: (i, j, 0)),
                   pl.BlockSpec((1, N_HEADS_A, tl), lambda i, j: (i, 0, j))],
        out_shape=[jax.ShapeDtypeStruct((b, l, n), F32),
                   jax.ShapeDtypeStruct((b, l, n), F32),
                   jax.ShapeDtypeStruct((b, N_HEADS_A, l), F32)],
        scratch_shapes=[pltpu.VMEM((1, n), F32)],
        compiler_params=_params(("parallel", "arbitrary")),
    )(vals, bias, carry)


_NT = (((1,), (1,)), ((), ()))


def _softmax_step(s, v, m_prev, l_prev, acc_prev):
    m_new = jnp.maximum(m_prev, jnp.max(s, axis=-1, keepdims=True))
    p = jnp.exp(s - m_new)
    alpha = jnp.exp(m_prev - m_new)
    l_new = alpha * l_prev + jnp.sum(p, axis=-1, keepdims=True)
    acc_new = alpha * acc_prev + jnp.dot(p.astype(BF16), v, preferred_element_type=F32)
    return m_new, l_new, acc_new


def _attn_kernel(q_ref, k_ref, v_ref, cum_ref, cumt_ref, o_ref, m_scr, l_scr, acc_scr, *, tile):
    h = pl.program_id(1)
    i = pl.program_id(2)
    q = q_ref[0]
    lane = lax.broadcasted_iota(jnp.int32, (tile, LANES), 1)
    cq = jnp.sum(jnp.where(lane == h, cum_ref[0], 0.0), axis=-1, keepdims=True)
    m_scr[...] = jnp.full(m_scr.shape, -jnp.inf, F32)
    l_scr[...] = jnp.zeros(l_scr.shape, F32)
    acc_scr[...] = jnp.zeros(acc_scr.shape, F32)

    def logits(j):
        start = pl.multiple_of(j * tile, tile)
        kj = k_ref[0, pl.ds(start, tile), :]
        vj = v_ref[0, pl.ds(start, tile), :]
        ck = cumt_ref[0, pl.ds(h, 1), pl.ds(start, tile)]
        s = lax.dot_general(q, kj, _NT, preferred_element_type=F32) + (cq - ck)
        return s, vj

    def update(s, vj):
        m, l, acc = _softmax_step(s, vj, m_scr[...], l_scr[...], acc_scr[...])
        m_scr[...] = m
        l_scr[...] = l
        acc_scr[...] = acc

    def body(j, carry):
        update(*logits(j))
        return carry

    lax.fori_loop(0, i, body, 0)
    s, vj = logits(i)
    row = lax.broadcasted_iota(jnp.int32, (tile, tile), 0)
    col = lax.broadcasted_iota(jnp.int32, (tile, tile), 1)
    update(jnp.where(col <= row, s, -jnp.inf), vj)
    o_ref[0] = (acc_scr[...] / l_scr[...]).astype(o_ref.dtype)


def _attention(q, k, v, cum, cumt):
    b, l, _ = q.shape
    tile = ATT_TILE if l >= 2 * ATT_TILE else LANES
    assert l % tile == 0
    dh = HEAD_DIM_A
    return pl.pallas_call(
        functools.partial(_attn_kernel, tile=tile),
        grid=(b, N_HEADS_A, l // tile),
        in_specs=[pl.BlockSpec((1, tile, dh), lambda bi, h, i: (bi, i, h)),
                  pl.BlockSpec((1, l, dh), lambda bi, h, i: (bi, 0, h)),
                  pl.BlockSpec((1, l, dh), lambda bi, h, i: (bi, 0, h)),
                  pl.BlockSpec((1, tile, LANES), lambda bi, h, i: (bi, i, 0)),
                  pl.BlockSpec((1, N_HEADS_A, l), lambda bi, h, i: (bi, 0, 0))],
        out_specs=pl.BlockSpec((1, tile, dh), lambda bi, h, i: (bi, i, h)),
        out_shape=jax.ShapeDtypeStruct((b, l, D_ATTN), BF16),
        scratch_shapes=[pltpu.VMEM((tile, 1), F32), pltpu.VMEM((tile, 1), F32), pltpu.VMEM((tile, dh), F32)],
        compiler_params=_params(("parallel", "parallel", "arbitrary")),
    )(q, k, v, cum, cumt)


def _attn_cached_kernel(q_ref, kn_ref, vn_ref, kc_ref, vc_ref, cumn_ref, cumtp_ref, cumtn_ref, o_ref,
                        m_scr, l_scr, acc_scr):
    j = pl.program_id(1)
    ld = q_ref.shape[1]
    dh = HEAD_DIM_A

    @pl.when(j == 0)
    def _():
        m_scr[...] = jnp.full(m_scr.shape, -jnp.inf, F32)
        l_scr[...] = jnp.zeros(l_scr.shape, F32)
        acc_scr[...] = jnp.zeros(acc_scr.shape, F32)

    def head_step(h, kh, vh, ck, mask):
        sl = slice(h * dh, (h + 1) * dh)
        s = lax.dot_general(q_ref[0, :, sl], kh, _NT, preferred_element_type=F32)
        s = s + (cumn_ref[0, :ld, h:h + 1] - ck)
        if mask is not None:
            s = jnp.where(mask, s, -jnp.inf)
        m, l, acc = _softmax_step(s, vh, m_scr[h], l_scr[h], acc_scr[:, sl])
        m_scr[h] = m
        l_scr[h] = l
        acc_scr[:, sl] = acc

    for h in range(N_HEADS_A):
        sl = slice(h * dh, (h + 1) * dh)
        head_step(h, kc_ref[0, :, sl].astype(BF16), vc_ref[0, :, sl].astype(BF16),
                  cumtp_ref[0, h:h + 1, :], None)

    @pl.when(j == pl.num_programs(1) - 1)
    def _():
        row = lax.broadcasted_iota(jnp.int32, (ld, ld), 0)
        col = lax.broadcasted_iota(jnp.int32, (ld, ld), 1)
        for h in range(N_HEADS_A):
            sl = slice(h * dh, (h + 1) * dh)
            head_step(h, kn_ref[0, :, sl], vn_ref[0, :, sl], cumtn_ref[0, h:h + 1, :ld], col <= row)
            o_ref[0, :, sl] = (acc_scr[:, sl] / l_scr[h]).astype(o_ref.dtype)


def _attention_cached(q, kn, vn, kc, vc, cum_new, cumt_past, cumt_new):
    b, ld, _ = q.shape
    lp = kc.shape[1]
    tk = min(DEC_KV_TILE, lp)
    assert lp % tk == 0
    lpad = cum_new.shape[1]
    new = pl.BlockSpec((1, ld, D_ATTN), lambda bi, j: (bi, 0, 0))
    past = pl.BlockSpec((1, tk, D_ATTN), lambda bi, j: (bi, j, 0))
    return pl.pallas_call(
        _attn_cached_kernel,
        grid=(b, lp // tk),
        in_specs=[new, new, new, past, past,
                  pl.BlockSpec((1, lpad, LANES), lambda bi, j: (bi, 0, 0)),
                  pl.BlockSpec((1, N_HEADS_A, tk), lambda bi, j: (bi, 0, j)),
                  pl.BlockSpec((1, N_HEADS_A, lpad), lambda bi, j: (bi, 0, 0))],
        out_specs=new,
        out_shape=jax.ShapeDtypeStruct((b, ld, D_ATTN), BF16),
        scratch_shapes=[pltpu.VMEM((N_HEADS_A, ld, 1), F32), pltpu.VMEM((N_HEADS_A, ld, 1), F32),
                        pltpu.VMEM((ld, D_ATTN), F32)],
        compiler_params=_params(("parallel", "arbitrary")),
    )(q, kn, vn, kc, vc, cum_new, cumt_past, cumt_new)


def _expand_heads(a):
    r = a.shape[0]
    low = lax.broadcasted_iota(jnp.int32, (r, LANES), 1) < HEAD_DIM_S
    return jnp.concatenate(
        [jnp.where(low, a[:, 2 * j:2 * j + 1], a[:, 2 * j + 1:2 * j + 2]) for j in range(N_HEADS_S // 2)], axis=1)


def _ssd_kernel(xbc_ref, z_ref, dt_ref, dtt_ref, past_ref, h0_ref, cw_ref, cb_ref, dtb_ref, dtbt_ref,
                alog_ref, alogt_ref, dsk_ref, nw_ref, o_ref, hout_ref, xbuf, ht_scr):
    c = pl.program_id(1)
    lc = xbc_ref.shape[1]
    hist = SUBLANES

    @pl.when(c == 0)
    def _():
        xbuf[0:hist, :] = past_ref[0]
        ht_scr[...] = h0_ref[0]

    xbuf[hist:hist + lc, :] = xbc_ref[0]
    u = cb_ref[...]
    for w in range(CONV_W):
        off = hist - (CONV_W - 1) + w
        u = u + xbuf[off:off + lc, :] * cw_ref[w:w + 1, :]
    xbuf[0:hist, :] = xbuf[lc:lc + hist, :]
    u = _silu(u)
    xs = u[:, :D_INNER]
    gn = N_GROUPS_S * D_STATE
    bm = u[:, D_INNER:D_INNER + gn].astype(BF16)
    cm = u[:, D_INNER + gn:].astype(BF16)

    dt = _softplus(dt_ref[0] + dtb_ref[...])
    dtt = _softplus(dtt_ref[0] + dtbt_ref[...])
    a = dt * (-jnp.exp(alog_ref[...]))
    at = dtt * (-jnp.exp(alogt_ref[...]))
    row = lax.broadcasted_iota(jnp.int32, (lc, lc), 0)
    col = lax.broadcasted_iota(jnp.int32, (lc, lc), 1)
    causal = col <= row
    a_cs = _tri_dot(causal.astype(BF16), a)
    a_cst = _dot_tri(at, (row <= col).astype(BF16))
    total = a_cs[lc - 1:lc, :]
    dt_e = _expand_heads(dt)
    w_e = _expand_heads(dt * jnp.exp(total - a_cs))
    ea_e = _expand_heads(jnp.exp(a_cs))
    cd_e = _expand_heads(jnp.exp(total))
    xdt = xs * dt_e
    xdw = (xs * w_e).astype(BF16)

    cbs = [lax.dot_general(cm[:, g * D_STATE:(g + 1) * D_STATE], bm[:, g * D_STATE:(g + 1) * D_STATE], _NT,
                           preferred_element_type=F32) for g in range(N_GROUPS_S)]
    low = lax.broadcasted_iota(jnp.int32, (lc, LANES), 1) < HEAD_DIM_S
    heads_per_group = N_HEADS_S // N_GROUPS_S
    yd = []
    for j in range(N_HEADS_S // 2):
        ms = []
        for hh in (2 * j, 2 * j + 1):
            seg = a_cs[:, hh:hh + 1] - a_cst[hh:hh + 1, :]
            dec = jnp.where(causal, jnp.exp(jnp.where(causal, seg, 0.0)), 0.0)
            ms.append((cbs[hh // heads_per_group] * dec).astype(BF16))
        xb = xdt[:, j * LANES:(j + 1) * LANES]
        rhs = jnp.concatenate([jnp.where(low, xb, 0.0), jnp.where(low, 0.0, xb)], axis=0).astype(BF16)
        yd.append(jnp.dot(jnp.concatenate(ms, axis=1), rhs, preferred_element_type=F32))
    y = jnp.concatenate(yd, axis=1)

    half = D_INNER // N_GROUPS_S
    ht = ht_scr[...]
    htb = ht.astype(BF16)
    y_off = jnp.concatenate(
        [jnp.dot(cm[:, g * D_STATE:(g + 1) * D_STATE], htb[:, g * half:(g + 1) * half],
                 preferred_element_type=F32) for g in range(N_GROUPS_S)], axis=1)
    st = jnp.concatenate(
        [lax.dot_general(bm[:, g * D_STATE:(g + 1) * D_STATE], xdw[:, g * half:(g + 1) * half],
                         (((0,), (0,)), ((), ())), preferred_element_type=F32) for g in range(N_GROUPS_S)], axis=1)
    ht_new = cd_e * ht + st
    ht_scr[...] = ht_new
    hout_ref[0] = ht_new

    y = y + y_off * ea_e + dsk_ref[...] * xs
    gz = y * _silu(z_ref[0])
    outs = []
    for g in range(N_GROUPS_S):
        gg = gz[:, g * half:(g + 1) * half]
        outs.append(gg * lax.rsqrt(jnp.mean(gg * gg, axis=-1, keepdims=True) + EPS))
    o_ref[0] = (jnp.concatenate(outs, axis=1) * nw_ref[...]).astype(o_ref.dtype)


def _ssd(xbc, z, dt_raw, conv_past, h0t, conv_w, conv_b, dt_bias, a_log, d_skip, norm_w):
    b, l, _ = xbc.shape
    lc = min(SSD_CHUNK, l)
    assert l % lc == 0 and lc % SUBLANES == 0 and lc >= SUBLANES
    nh = N_HEADS_S
    dtt = jnp.swapaxes(dt_raw, 1, 2)
    past = jnp.pad(conv_past, ((0, 0), (SUBLANES - (CONV_W - 1), 0), (0, 0)))
    const = lambda shape: pl.BlockSpec(shape, lambda i, j: tuple(0 for _ in shape))
    o, hout = pl.pallas_call(
        _ssd_kernel,
        grid=(b, l // lc),
        in_specs=[pl.BlockSpec((1, lc, CONV_DIM), lambda i, j: (i, j, 0)),
                  pl.BlockSpec((1, lc, D_INNER), lambda i, j: (i, j, 0)),
                  pl.BlockSpec((1, lc, nh), lambda i, j: (i, j, 0)),
                  pl.BlockSpec((1, nh, lc), lambda i, j: (i, 0, j)),
                  pl.BlockSpec((1, SUBLANES, CONV_DIM), lambda i, j: (i, 0, 0)),
                  pl.BlockSpec((1, D_STATE, D_INNER), lambda i, j: (i, 0, 0)),
                  const((CONV_W, CONV_DIM)), const((1, CONV_DIM)),
                  const((1, nh)), const((nh, 1)), const((1, nh)), const((nh, 1)),
                  const((1, D_INNER)), const((1, D_INNER))],
        out_specs=[pl.BlockSpec((1, lc, D_INNER), lambda i, j: (i, j, 0)),
                   pl.BlockSpec((1, D_STATE, D_INNER), lambda i, j: (i, 0, 0))],
        out_shape=[jax.ShapeDtypeStruct((b, l, D_INNER), BF16),
                   jax.ShapeDtypeStruct((b, D_STATE, D_INNER), F32)],
        scratch_shapes=[pltpu.VMEM((lc + SUBLANES, CONV_DIM), F32), pltpu.VMEM((D_STATE, D_INNER), F32)],
        compiler_params=_params(("parallel", "arbitrary")),
    )(xbc, z, dt_raw, dtt, past, h0t, conv_w, conv_b.reshape(1, -1),
      dt_bias.reshape(1, nh), dt_bias.reshape(nh, 1), a_log.reshape(1, nh), a_log.reshape(nh, 1),
      jnp.repeat(d_skip, HEAD_DIM_S).reshape(1, -1), norm_w.reshape(1, -1))
    return o, hout


def _outproj_kernel(x_ref, oa_ref, os_ref, gt_ref, sc_ref, sh_ref, g_ref, wa_ref, ws_ref, rw_ref, rb_ref,
                    x1_ref, h2_ref, lg_ref):
    nb, tl, d = x_ref.shape
    oa = oa_ref[...].reshape(nb * tl, -1)
    os_ = os_ref[...].reshape(nb * tl, -1)
    mix = jnp.dot(oa, wa_ref[...], preferred_element_type=F32) + jnp.dot(os_, ws_ref[...], preferred_element_type=F32)
    x1 = x_ref[...] + gt_ref[...] * mix.reshape(nb, tl, d)
    x1_ref[...] = x1
    ms = jnp.mean(x1 * x1, axis=-1, keepdims=True)
    h2 = x1 * lax.rsqrt(ms + EPS) * g_ref[...]
    h2 = h2 * (1.0 + sc_ref[...]) + sh_ref[...]
    h2_ref[...] = h2
    lg = jnp.dot(h2.reshape(nb * tl, d), rw_ref[...], precision=HIGHEST, preferred_element_type=F32) + rb_ref[...]
    lg_ref[...] = lg.reshape(nb, tl, -1)


def _outproj(x, oa, os_, gt, sc, sh, g, wa, ws, rw, rb):
    b, l, d = x.shape
    nb, tl = _row_blocking(b, l, ROW_TILE)
    row = lambda n: pl.BlockSpec((nb, tl, n), lambda i, j: (i, j, 0))
    mod = pl.BlockSpec((nb, 1, d), lambda i, j: (i, 0, 0))
    const = lambda a: pl.BlockSpec(a.shape, lambda i, j: tuple(0 for _ in a.shape))
    g3 = g.reshape(1, 1, d)
    return pl.pallas_call(
        _outproj_kernel,
        grid=(b // nb, l // tl),
        in_specs=[row(d), row(D_ATTN), row(D_INNER), mod, mod, mod, const(g3), const(wa), const(ws),
                  const(rw), const(rb)],
        out_specs=[row(d), row(d), row(LANES)],
        out_shape=[jax.ShapeDtypeStruct((b, l, d), F32), jax.ShapeDtypeStruct((b, l, d), F32),
                   jax.ShapeDtypeStruct((b, l, LANES), F32)],
        compiler_params=_params(("parallel", "parallel")),
    )(x, oa, os_, gt.reshape(b, 1, d), sc.reshape(b, 1, d), sh.reshape(b, 1, d), g3, wa, ws, rw, rb)


def _route_kernel(lg_ref, idx_ref, gate_ref, cnt_ref, carry_scr):
    @pl.when(pl.program_id(0) == 0)
    def _():
        carry_scr[...] = jnp.zeros(carry_scr.shape, F32)

    v = lg_ref[...]
    tr = v.shape[0]
    lane = lax.broadcasted_iota(jnp.int32, (tr, LANES), 1)
    lane_f = lane.astype(F32)
    tops, idxs = [], []
    onehot = jnp.zeros((tr, LANES), F32)
    for _ in range(TOP_K):
        m = jnp.max(v, axis=-1, keepdims=True)
        idx = jnp.min(jnp.where(v == m, lane_f, float(LANES)), axis=-1, keepdims=True)
        hit = lane_f == idx
        v = jnp.where(hit, -jnp.inf, v)
        onehot = onehot + hit.astype(F32)
        tops.append(m)
        idxs.append(idx)
    es = [jnp.exp(t - tops[0]) for t in tops]
    denom = es[0] + es[1] + es[2] + es[3]

    row = lax.broadcasted_iota(jnp.int32, (tr, tr), 0)
    col = lax.broadcasted_iota(jnp.int32, (tr, tr), 1)
    before = jnp.dot((col < row).astype(BF16), onehot.astype(BF16), preferred_element_type=F32) + carry_scr[...]

    idx_out = jnp.zeros((tr, LANES), F32)
    gate_out = jnp.zeros((tr, LANES), F32)
    for k in range(TOP_K):
        rank = jnp.sum(jnp.where(lane_f == idxs[k], before, 0.0), axis=-1, keepdims=True)
        idx_out = jnp.where(lane == k, idxs[k], idx_out)
        idx_out = jnp.where(lane == TOP_K + k, rank, idx_out)
        gate_out = jnp.where(lane == k, es[k] / denom, gate_out)
    idx_ref[...] = idx_out.astype(jnp.int32)
    gate_ref[...] = gate_out
    carry_scr[...] = carry_scr[...] + jnp.sum(onehot, axis=0, keepdims=True)
    cnt_ref[...] = carry_scr[...]


def _route(logits):
    t, n = logits.shape
    tr = min(ROUTE_TILE, t)
    assert t % tr == 0
    return pl.pallas_call(
        _route_kernel,
        grid=(t // tr,),
        in_specs=[pl.BlockSpec((tr, n), lambda i: (i, 0))],
        out_specs=[pl.BlockSpec((tr, n), lambda i: (i, 0)), pl.BlockSpec((tr, n), lambda i: (i, 0)),
                   pl.BlockSpec((1, n), lambda i: (0, 0))],
        out_shape=[jax.ShapeDtypeStruct((t, n), jnp.int32), jax.ShapeDtypeStruct((t, n), F32),
                   jax.ShapeDtypeStruct((1, n), F32)],
        scratch_shapes=[pltpu.VMEM((1, n), F32)],
        compiler_params=_params(("arbitrary",)),
    )(logits)


def _gather_rows(idx_ref, src_hbm, dst, sem, n_rows):
    def body(r, carry):
        t = idx_ref[0, 0, r]
        pltpu.make_async_copy(src_hbm.at[pl.ds(t, 1), :], dst.at[pl.ds(r, 1), :], sem).start()
        return carry
    lax.fori_loop(0, n_rows, body, 0, unroll=8)


def _wait_rows(src_hbm, dst, sem, n_rows):
    pltpu.make_async_copy(src_hbm.at[pl.ds(0, n_rows), :], dst, sem).wait()


def _experts_kernel(be_ref, idx0_ref, idxn_ref, h_hbm, wgu_ref, bgu_ref, wdn_ref, bdn_ref, o_ref, xbuf, sem):
    i = pl.program_id(0)
    n = pl.num_programs(0)
    m = xbuf.shape[1]
    slot = lax.rem(i, 2)

    @pl.when(i == 0)
    def _():
        _gather_rows(idx0_ref, h_hbm, xbuf.at[0], sem.at[0], m)

    @pl.when(i + 1 < n)
    def _():
        _gather_rows(idxn_ref, h_hbm, xbuf.at[1 - slot], sem.at[1 - slot], m)

    _wait_rows(h_hbm, xbuf.at[slot], sem.at[slot], m)
    x = xbuf[slot].astype(BF16)
    gu = jnp.dot(x, wgu_ref[0], preferred_element_type=F32) + bgu_ref[0]
    gate = jnp.minimum(gu[:, :D_FF], SWIGLU_LIMIT)
    up = jnp.clip(gu[:, D_FF:], -SWIGLU_LIMIT, SWIGLU_LIMIT)
    act = (up + 1.0) * gate * jax.nn.sigmoid(SWIGLU_ALPHA * gate)
    o_ref[...] = jnp.dot(act.astype(BF16), wdn_ref[0], preferred_element_type=F32) + bdn_ref[0]


def _experts(block_e, slot_tok, h2, wgu, bgu, wdn, bdn):
    n_blocks = block_e.shape[0]
    m = MOE_ROWS
    d = h2.shape[1]
    idx = slot_tok.reshape(n_blocks, 1, m)
    smem = lambda f: pl.BlockSpec((1, 1, m), f, memory_space=pltpu.SMEM)
    grid_spec = pltpu.PrefetchScalarGridSpec(
        num_scalar_prefetch=1,
        grid=(n_blocks,),
        in_specs=[smem(lambda i, be: (0, 0, 0)),
                  smem(lambda i, be: (jnp.minimum(i + 1, n_blocks - 1), 0, 0)),
                  pl.BlockSpec(memory_space=pl.ANY),
                  pl.BlockSpec((1, d, 2 * D_FF), lambda i, be: (be[i], 0, 0)),
                  pl.BlockSpec((1, 1, 2 * D_FF), lambda i, be: (be[i], 0, 0)),
                  pl.BlockSpec((1, D_FF, d), lambda i, be: (be[i], 0, 0)),
                  pl.BlockSpec((1, 1, d), lambda i, be: (be[i], 0, 0))],
        out_specs=pl.BlockSpec((m, d), lambda i, be: (i, 0)),
        scratch_shapes=[pltpu.VMEM((2, m, d), F32), pltpu.SemaphoreType.DMA((2,))],
    )
    return pl.pallas_call(
        _experts_kernel,
        grid_spec=grid_spec,
        out_shape=jax.ShapeDtypeStruct((n_blocks * m, d), F32),
        compiler_params=_params(("arbitrary",)),
    )(block_e, idx, idx, h2, wgu, bgu.reshape(N_EXPERTS, 1, -1), wdn, bdn.reshape(N_EXPERTS, 1, -1))


def _combine_kernel(idx0_ref, idxn_ref, y_hbm, x1_ref, gate_ref, gt_ref, g_ref, o_ref, ybuf, sem):
    i = pl.program_id(0)
    n = pl.num_programs(0)
    nb, tl, d = x1_ref.shape
    tc = nb * tl
    slot = lax.rem(i, 2)

    @pl.when(i == 0)
    def _():
        _gather_rows(idx0_ref, y_hbm, ybuf.at[0], sem.at[0], TOP_K * tc)

    @pl.when(i + 1 < n)
    def _():
        _gather_rows(idxn_ref, y_hbm, ybuf.at[1 - slot], sem.at[1 - slot], TOP_K * tc)

    _wait_rows(y_hbm, ybuf.at[slot], sem.at[slot], TOP_K * tc)
    gates = gate_ref[...]
    y = jnp.zeros((tc, d), F32)
    for k in range(TOP_K):
        y = y + gates[:, k:k + 1] * ybuf[slot, k * tc:(k + 1) * tc, :]
    x2 = x1_ref[...] + gt_ref[...] * y.reshape(nb, tl, d)
    ms = jnp.mean(x2 * x2, axis=-1, keepdims=True)
    o_ref[...] = x2 * lax.rsqrt(ms + EPS) * g_ref[...]


def _combine(dest, y_slots, x1, gates, gt, g_final):
    b, l, d = x1.shape
    nb, tl = _row_blocking(b, l, COMBINE_TILE)
    tc = nb * tl
    t = b * l
    n_steps = t // tc
    idx = jnp.swapaxes(dest.reshape(n_steps, tc, TOP_K), 1, 2).reshape(n_steps, 1, TOP_K * tc)
    steps_per_batch_row = l // tl
    tok = lambda i: (i // steps_per_batch_row, i % steps_per_batch_row, 0)
    smem = lambda f: pl.BlockSpec((1, 1, TOP_K * tc), f, memory_space=pltpu.SMEM)
    return pl.pallas_call(
        _combine_kernel,
        grid=(n_steps,),
        in_specs=[smem(lambda i: (0, 0, 0)),
                  smem(lambda i: (jnp.minimum(i + 1, n_steps - 1), 0, 0)),
                  pl.BlockSpec(memory_space=pl.ANY),
                  pl.BlockSpec((nb, tl, d), tok),
                  pl.BlockSpec((tc, LANES), lambda i: (i, 0)),
                  pl.BlockSpec((nb, 1, d), lambda i: (i // steps_per_batch_row, 0, 0)),
                  pl.BlockSpec((1, 1, d), lambda i: (0, 0, 0))],
        out_specs=pl.BlockSpec((nb, tl, d), tok),
        out_shape=jax.ShapeDtypeStruct((b, l, d), F32),
        scratch_shapes=[pltpu.VMEM((2, TOP_K * tc, d), F32), pltpu.SemaphoreType.DMA((2,))],
        compiler_params=_params(("arbitrary",)),
    )(idx, idx, y_slots, x1, gates, gt.reshape(b, 1, d), g_final.reshape(1, 1, d))


def _moe(h2, logits, x1, gt_m, g_final, wts):
    b, l, d = x1.shape
    t = b * l
    m = MOE_ROWS
    idx, gates, counts = _route(logits.reshape(t, LANES))
    top_e = idx[:, :TOP_K]
    rank = idx[:, TOP_K:2 * TOP_K]
    counts = counts[0, :N_EXPERTS].astype(jnp.int32)
    padded = (counts + m - 1) // m * m
    ends_p = jnp.cumsum(padded)
    starts_p = ends_p - padded
    dest = starts_p[top_e] + rank
    n_blocks = (t * TOP_K + N_EXPERTS * (m - 1) + m - 1) // m
    tok = jnp.broadcast_to(jnp.arange(t, dtype=jnp.int32)[:, None], (t, TOP_K))
    slot_tok = jnp.zeros((n_blocks * m,), jnp.int32).at[dest.reshape(-1)].set(tok.reshape(-1))
    block_e = jnp.minimum(
        jnp.searchsorted(ends_p, jnp.arange(n_blocks, dtype=jnp.int32) * m, side='right'),
        N_EXPERTS - 1).astype(jnp.int32)
    y_slots = _experts(block_e, slot_tok, h2.reshape(t, d), *wts)
    return _combine(dest, y_slots, x1, gates, gt_m, g_final)


def _layer(x, mod, k_past, v_past, logf_past, conv_past, ssm_past, p, g_final):
    b, l, d = x.shape
    sh_a, sc_a, gt_a, sh_m, sc_m, gt_m = jnp.split(mod, 6, axis=-1)
    q, kf, vf, kb, vb, z, xbc, sm = _inproj(x, p['g_mix'], sc_a, sh_a, *p['w_in'])
    dt_raw = sm[:, :, N_HEADS_A:N_HEADS_A + N_HEADS_S]

    zero_carry = jnp.zeros((b, 1, LANES), F32)
    if k_past is None:
        logf, cum, cumt = _cumsum(sm, p['b_f'], zero_carry, True)
        o_a = _attention(q, kb, vb, cum, cumt)
        conv_past = jnp.zeros((b, CONV_W - 1, CONV_DIM), F32)
        h0t = jnp.zeros((b, D_STATE, D_INNER), F32)
    else:
        lp = k_past.shape[1]
        past = jnp.pad(logf_past, ((0, 0), (0, 0), (0, LANES - N_HEADS_A)))
        _, cum_p, cumt_p = _cumsum(past, p['b_f'], zero_carry, False)
        lpad = -(-l // LANES) * LANES
        sm_pad = jnp.pad(sm, ((0, 0), (0, lpad - l), (0, 0)))
        logf, cum_n, cumt_n = _cumsum(sm_pad, p['b_f'], cum_p[:, lp - 1:lp, :], True)
        logf = logf[:, :l]
        o_a = _attention_cached(q, kb, vb, k_past.reshape(b, lp, D_ATTN), v_past.reshape(b, lp, D_ATTN),
                                cum_n, cumt_p, cumt_n)
        h0t = jnp.swapaxes(ssm_past.reshape(b, D_INNER, D_STATE), 1, 2)
    o_s, hout = _ssd(xbc, z, dt_raw, conv_past, h0t, p['conv_w'], p['conv_b'], p['dt_bias'], p['a_log'],
                     p['d_skip'], p['ssd_norm_w'])
    x1, h2, logits = _outproj(x, o_a, o_s, gt_a, sc_m, sh_m, p['g_ffn'], p['w_out_a'], p['w_out_s'],
                              p['router_w'], p['router_b'])
    y = _moe(h2, logits, x1, gt_m, g_final, p['experts'])

    assert l >= CONV_W - 1
    conv_new = xbc[:, l - (CONV_W - 1):]
    ssm_new = jnp.swapaxes(hout, 1, 2).reshape(b, N_HEADS_S, HEAD_DIM_S, D_STATE)
    return (y, kf.reshape(b, l, N_HEADS_A, HEAD_DIM_A), vf.reshape(b, l, N_HEADS_A, HEAD_DIM_A),
            logf[:, :, :N_HEADS_A], conv_new, ssm_new)


def kernel(x_prompt, x_sample, cache_k, cache_v, cache_logf, state_conv, state_ssm, c_prompt, c_sample, w_ada, b_ada, g_mix, w_in, b_f, conv_w, conv_b, dt_bias, a_log, d_skip, ssd_norm_w, w_out, g_ffn, router_w, router_b, w_gate_up, b_gate_up, w_down, b_down, g_final):
    assert w_ada.shape[0] == 1, "single-layer operation"
    bp = x_prompt.shape[0]
    w = w_in[0]
    q_end, k_end, v_end = D_ATTN, 2 * D_ATTN, 3 * D_ATTN
    f_end = v_end + N_HEADS_A
    z_end = f_end + D_INNER
    xbc_end = z_end + CONV_DIM
    w_small = jnp.concatenate(
        [w[:, v_end:f_end], w[:, xbc_end:], jnp.zeros((D_MODEL, LANES - N_HEADS_A - N_HEADS_S), F32)], axis=1)
    cast = lambda a: a.astype(BF16)
    p = {
        'g_mix': g_mix[0],
        'w_in': (cast(w[:, :q_end]), cast(w[:, q_end:k_end]), cast(w[:, k_end:v_end]), cast(w[:, f_end:z_end]),
                 cast(w[:, z_end:xbc_end]), cast(w_small)),
        'b_f': jnp.pad(b_f[0], (0, LANES - N_HEADS_A)).reshape(1, LANES),
        'conv_w': conv_w[0], 'conv_b': conv_b[0], 'dt_bias': dt_bias[0], 'a_log': a_log[0],
        'd_skip': d_skip[0], 'ssd_norm_w': ssd_norm_w[0],
        'w_out_a': cast(w_out[0, :D_ATTN]), 'w_out_s': cast(w_out[0, D_ATTN:]),
        'g_ffn': g_ffn[0],
        'router_w': jnp.pad(router_w[0], ((0, 0), (0, LANES - N_EXPERTS))),
        'router_b': jnp.pad(router_b[0], (0, LANES - N_EXPERTS), constant_values=-jnp.inf).reshape(1, LANES),
        'experts': (cast(w_gate_up[0]), b_gate_up[0], cast(w_down[0]), b_down[0]),
    }
    mod = _adaln(jnp.concatenate([c_prompt, c_sample], axis=0), w_ada[0], b_ada[0])
    outs_p = _layer(x_prompt, mod[:bp], None, None, None, None, None, p, g_final)
    outs_s = _layer(x_sample, mod[bp:], cache_k[0], cache_v[0], cache_logf[0], state_conv[0], state_ssm[0], p, g_final)
    stack = lambda a: a[None]
    return (outs_p[0], outs_s[0]) + tuple(stack(a) for a in outs_p[1:]) + tuple(stack(a) for a in outs_s[1:])
```

```python
import functools
import math

import jax
import jax.numpy as jnp
from jax import lax
from jax.experimental import pallas as pl
from jax.experimental.pallas import tpu as pltpu

F32 = jnp.float32
BF16 = jnp.bfloat16
HIGHEST = lax.Precision.HIGHEST

D_MODEL = 1024
N_HEADS_A = 8
HEAD_DIM_A = 128
D_ATTN = N_HEADS_A * HEAD_DIM_A
D_INNER = 1024
HEAD_DIM_S = 64
N_HEADS_S = D_INNER // HEAD_DIM_S
N_GROUPS_S = 2
D_STATE = 128
CONV_W = 4
CONV_DIM = D_INNER + 2 * N_GROUPS_S * D_STATE
N_EXPERTS = 32
TOP_K = 4
D_FF = 1024
SWIGLU_LIMIT = 7.0
SWIGLU_ALPHA = 1.702
EPS = 1e-5
LANES = 128
SUBLANES = 8
ROW_TILE_SUBLANES = D_MODEL // LANES
assert ROW_TILE_SUBLANES == SUBLANES
VMEM_LIMIT = 56 * 1024 * 1024
LOG2E = 1.4426950408889634
ATT_HEADS_PER_STEP = 4

ROW_TILE = 512
ATT_TILE = 512
DEC_KV_TILE = 512
CUM_TILE = 256
SSD_CHUNK = 128
ROUTE_TILE = 256
MOE_ROWS = 512
MOE_ROWS_SMALL = 128
COMBINE_TILE = 256


def _params(sem, vmem=VMEM_LIMIT):
    return pltpu.CompilerParams(dimension_semantics=sem, vmem_limit_bytes=vmem)


def _row_blocking(b, l, tile):
    if l >= tile:
        assert l % tile == 0
        return 1, tile
    nb = min(b, max(1, tile // l))
    while b % nb:
        nb -= 1
    return nb, l


def _split3(x):
    hi = x.astype(BF16)
    r1 = x - hi.astype(F32)
    mid = r1.astype(BF16)
    lo = (r1 - mid.astype(F32)).astype(BF16)
    return hi, mid, lo


def _tri_dot(tri, x):
    hi, mid, lo = _split3(x)
    d = lambda p: jnp.dot(tri, p, preferred_element_type=F32)
    return d(hi) + d(mid) + d(lo)


def _dot_tri(x, tri):
    hi, mid, lo = _split3(x)
    d = lambda p: jnp.dot(p, tri, preferred_element_type=F32)
    return d(hi) + d(mid) + d(lo)


def _silu(x):
    return x * jax.nn.sigmoid(x)


def _softplus(x):
    return jnp.maximum(x, 0.0) + jnp.log1p(jnp.exp(-jnp.abs(x)))


def _log_sigmoid(x):
    return jnp.minimum(x, 0.0) - jnp.log1p(jnp.exp(-jnp.abs(x)))


def _adaln_kernel(c_ref, w_ref, b_ref, o_ref):
    s = _silu(c_ref[...])
    o_ref[...] = jnp.dot(s, w_ref[...], precision=HIGHEST, preferred_element_type=F32) + b_ref[...]


def _adaln(c, w, b):
    m, d = c.shape
    n = w.shape[1]
    tn = 512
    return pl.pallas_call(
        _adaln_kernel,
        grid=(n // tn,),
        in_specs=[pl.BlockSpec((m, d), lambda j: (0, 0)),
                  pl.BlockSpec((d, tn), lambda j: (0, j)),
                  pl.BlockSpec((1, tn), lambda j: (0, j))],
        out_specs=pl.BlockSpec((m, tn), lambda j: (0, j)),
        out_shape=jax.ShapeDtypeStruct((m, n), F32),
        compiler_params=_params(("parallel",)),
    )(c, w, b.reshape(1, n))


def _inproj_kernel(x_ref, g_ref, sc_ref, sh_ref, wq_ref, wk_ref, wv_ref, wz_ref, wx_ref, ws_ref,
                   q_ref, kf_ref, vf_ref, kb_ref, vb_ref, z_ref, xbc_ref, sm_ref):
    x = x_ref[...]
    nb, tl, d = x.shape
    ms = jnp.mean(x * x, axis=-1, keepdims=True)
    h = x * lax.rsqrt(ms + EPS) * g_ref[...]
    h = h * (1.0 + sc_ref[...]) + sh_ref[...]
    hb = h.reshape(nb * tl, d).astype(BF16)

    def mm(w_ref):
        return jnp.dot(hb, w_ref[...], preferred_element_type=F32)

    q = mm(wq_ref) * (LOG2E / math.sqrt(HEAD_DIM_A))
    q_ref[...] = q.astype(BF16).reshape(nb, tl, -1)
    k = mm(wk_ref)
    kf_ref[...] = k.reshape(nb, tl, -1)
    kb_ref[...] = k.astype(BF16).reshape(nb, tl, -1)
    v = mm(wv_ref)
    vf_ref[...] = v.reshape(nb, tl, -1)
    vb_ref[...] = v.astype(BF16).reshape(nb, tl, -1)
    z_ref[...] = mm(wz_ref).reshape(nb, tl, -1)
    xbc_ref[...] = mm(wx_ref).reshape(nb, tl, -1)
    sm_ref[...] = mm(ws_ref).reshape(nb, tl, -1)


def _inproj(x, g, sc, sh, wq, wk, wv, wz, wx, ws):
    b, l, d = x.shape
    nb, tl = _row_blocking(b, l, ROW_TILE)
    grid = (b // nb, l // tl)
    row = lambda n: pl.BlockSpec((nb, tl, n), lambda i, j: (i, j, 0))
    mod = pl.BlockSpec((nb, 1, d), lambda i, j: (i, 0, 0))
    wspec = lambda w: pl.BlockSpec(w.shape, lambda i, j: (0, 0), pipeline_mode=pl.Buffered(1))
    outs = [(D_ATTN, BF16), (D_ATTN, F32), (D_ATTN, F32), (D_ATTN, BF16), (D_ATTN, BF16),
            (D_INNER, F32), (CONV_DIM, F32), (LANES, F32)]
    return pl.pallas_call(
        _inproj_kernel,
        grid=grid,
        in_specs=[row(d), pl.BlockSpec((1, 1, d), lambda i, j: (0, 0, 0)), mod, mod,
                  wspec(wq), wspec(wk), wspec(wv), wspec(wz), wspec(wx), wspec(ws)],
        out_specs=[row(n) for n, _ in outs],
        out_shape=[jax.ShapeDtypeStruct((b, l, n), dt) for n, dt in outs],
        compiler_params=_params(("parallel", "parallel")),
    )(x, g.reshape(1, 1, d), sc.reshape(b, 1, d), sh.reshape(b, 1, d), wq, wk, wv, wz, wx, ws)


def _running_sum(v_ref, bias_ref, carry_ref, carry_scr, apply_logsig):
    @pl.when(pl.program_id(1) == 0)
    def _():
        carry_scr[...] = carry_ref[0]

    v = v_ref[0]
    tl = v.shape[0]
    lf = _log_sigmoid(v + bias_ref[...]) if apply_logsig else v
    row = lax.broadcasted_iota(jnp.int32, (tl, tl), 0)
    col = lax.broadcasted_iota(jnp.int32, (tl, tl), 1)
    tri = (row >= col).astype(BF16)
    cs = _tri_dot(tri, lf) + carry_scr[...]
    carry_scr[...] = cs[tl - 1:tl, :]
    return lf, cs


def _cumsum_kernel(v_ref, bias_ref, carry_ref, logf_ref, cum_ref, cumt_ref, carry_scr, *, apply_logsig):
    lf, cs = _running_sum(v_ref, bias_ref, carry_ref, carry_scr, apply_logsig)
    logf_ref[0] = lf
    cum_ref[0] = cs
    cumt_ref[0] = cs.T[:N_HEADS_A, :]


def _cumsum(vals, bias, carry, apply_logsig):
    b, l, n = vals.shape
    tl = min(CUM_TILE, l)
    assert l % tl == 0 and tl % LANES == 0
    return pl.pallas_call(
        functools.partial(_cumsum_kernel, apply_logsig=apply_logsig),
        grid=(b, l // tl),
        in_specs=[pl.BlockSpec((1, tl, n), lambda i, j: (i, j, 0)),
                  pl.BlockSpec((1, n), lambda i, j: (0, 0)),
                  pl.BlockSpec((1, 1, n), lambda i, j: (i, 0, 0))],
        out_specs=[pl.BlockSpec((1, tl, n), lambda i, j: (i, j, 0)),
                   pl.BlockSpec((1, tl, n), lambda i, j: (i, j, 0)),
                   pl.BlockSpec((1, N_HEADS_A, tl), lambda i, j: (i, 0, j))],
        out_shape=[jax.ShapeDtypeStruct((b, l, n), F32),
                   jax.ShapeDtypeStruct((b, l, n), F32),
                   jax.ShapeDtypeStruct((b, N_HEADS_A, l), F32)],
        scratch_shapes=[pltpu.VMEM((1, n), F32)],
        compiler_params=_params(("parallel", "arbitrary")),
    )(vals, bias, carry)


N_BIAS_TERMS = 3


def _cumsum_bias_kernel(v_ref, bias_ref, carry_ref, logf_ref, qx_ref, kx_ref, carry_scr):
    lf, cs = _running_sum(v_ref, bias_ref, carry_ref, carry_scr, True)
    logf_ref[0] = lf
    tl = cs.shape[0]
    c2 = cs * LOG2E
    lane = lax.broadcasted_iota(jnp.int32, (tl, LANES), 1)
    ones_q = jnp.where((lane >= N_BIAS_TERMS) & (lane < 2 * N_BIAS_TERMS), 1.0, 0.0)
    ones_k = jnp.where(lane < N_BIAS_TERMS, 1.0, 0.0)
    for h in range(N_HEADS_A):
        terms = [t.astype(F32) for t in _split3(c2[:, h:h + 1])]
        qx, kx = ones_q, ones_k
        for n, t in enumerate(terms):
            qx = jnp.where(lane == n, t, qx)
            kx = jnp.where(lane == N_BIAS_TERMS + n, -t, kx)
        qx_ref[0, :, h * LANES:(h + 1) * LANES] = qx.astype(BF16)
        kx_ref[0, :, h * LANES:(h + 1) * LANES] = kx.astype(BF16)


def _cumsum_bias(vals, bias):
    b, l, n = vals.shape
    tl = min(CUM_TILE, l)
    assert l % tl == 0
    row = lambda w: pl.BlockSpec((1, tl, w), lambda i, j: (i, j, 0))
    return pl.pallas_call(
        _cumsum_bias_kernel,
        grid=(b, l // tl),
        in_specs=[row(n), pl.BlockSpec((1, n), lambda i, j: (0, 0)), pl.BlockSpec((1, 1, n), lambda i, j: (i, 0, 0))],
        out_specs=[row(n), row(D_ATTN), row(D_ATTN)],
        out_shape=[jax.ShapeDtypeStruct((b, l, n), F32), jax.ShapeDtypeStruct((b, l, D_ATTN), BF16),
                   jax.ShapeDtypeStruct((b, l, D_ATTN), BF16)],
        scratch_shapes=[pltpu.VMEM((1, n), F32)],
        compiler_params=_params(("parallel", "arbitrary")),
    )(vals, bias, jnp.zeros((b, 1, n), F32))


_NT = (((1,), (1,)), ((), ()))


def _softmax_step(s, v, m_prev, l_prev, acc_prev):
    m_new = jnp.maximum(m_prev, jnp.max(s, axis=-1, keepdims=True))
    p = jnp.exp2(s - m_new)
    alpha = jnp.exp2(m_prev - m_new)
    l_new = alpha * l_prev + jnp.sum(p, axis=-1, keepdims=True)
    acc_new = alpha * acc_prev + jnp.dot(p.astype(BF16), v, preferred_element_type=F32)
    return m_new, l_new, acc_new


def _attn_kernel(q_ref, qx_ref, k_ref, kx_ref, v_ref, o_ref, m_scr, acc_scr, *, tile, heads):
    i = pl.program_id(2)
    dh = HEAD_DIM_A
    n_chunks = tile // LANES
    ones = jnp.ones((tile, dh), BF16)
    m_scr[...] = jnp.full(m_scr.shape, -jnp.inf, F32)
    acc_scr[...] = jnp.zeros(acc_scr.shape, F32)
    q2 = [jnp.concatenate([q_ref[0, :, hh * dh:(hh + 1) * dh], qx_ref[0, :, hh * dh:(hh + 1) * dh]], axis=1)
          for hh in range(heads)]

    def update(j, mask):
        start = pl.multiple_of(j * tile, tile)
        rows = pl.ds(start, tile)
        for hh in range(heads):
            sl = slice(hh * dh, (hh + 1) * dh)
            k2 = jnp.concatenate([k_ref[0, rows, sl], kx_ref[0, rows, sl]], axis=1)
            v2 = jnp.concatenate([v_ref[0, rows, sl], ones], axis=1)
            s = lax.dot_general(q2[hh], k2, _NT, preferred_element_type=F32)
            if mask is not None:
                s = jnp.where(mask, s, -jnp.inf)
            chunks = [s[:, c * LANES:(c + 1) * LANES] for c in range(n_chunks)]
            cmax = functools.reduce(jnp.maximum, chunks)
            m_prev = m_scr[hh]
            m_new = jnp.maximum(m_prev, jnp.max(cmax, axis=-1, keepdims=True))
            alpha = jnp.exp2(m_prev - m_new)
            p = jnp.concatenate([jnp.exp2(c - m_new) for c in chunks], axis=1).astype(BF16)
            pv = jnp.dot(p, v2, preferred_element_type=F32)
            m_scr[hh] = m_new
            acc_scr[hh, :, :dh] = alpha * acc_scr[hh, :, :dh] + pv[:, :dh]
            acc_scr[hh, :, dh:] = alpha * acc_scr[hh, :, dh:] + pv[:, dh:]

    def body(j, carry):
        update(j, None)
        return carry

    lax.fori_loop(0, i, body, 0)
    row = lax.broadcasted_iota(jnp.int32, (tile, tile), 0)
    col = lax.broadcasted_iota(jnp.int32, (tile, tile), 1)
    update(i, col <= row)
    for hh in range(heads):
        o_ref[0, :, hh * dh:(hh + 1) * dh] = (acc_scr[hh, :, :dh] / acc_scr[hh, :, dh:]).astype(o_ref.dtype)


def _attention(q, qx, k, kx, v):
    b, l, _ = q.shape
    tile = ATT_TILE if l >= 2 * ATT_TILE else LANES
    assert l % tile == 0
    heads = ATT_HEADS_PER_STEP
    w = heads * HEAD_DIM_A
    qspec = pl.BlockSpec((1, tile, w), lambda bi, h, i: (bi, i, h))
    kspec = pl.BlockSpec((1, l, w), lambda bi, h, i: (bi, 0, h))
    return pl.pallas_call(
        functools.partial(_attn_kernel, tile=tile, heads=heads),
        grid=(b, N_HEADS_A // heads, l // tile),
        in_specs=[qspec, qspec, kspec, kspec, kspec],
        out_specs=qspec,
        out_shape=jax.ShapeDtypeStruct((b, l, D_ATTN), BF16),
        scratch_shapes=[pltpu.VMEM((heads, tile, LANES), F32), pltpu.VMEM((heads, tile, 2 * HEAD_DIM_A), F32)],
        compiler_params=_params(("parallel", "parallel", "arbitrary")),
    )(q, qx, k, kx, v)


def _attn_cached_kernel(q_ref, kn_ref, vn_ref, kc_ref, vc_ref, cumn_ref, cumtp_ref, cumtn_ref, o_ref,
                        m_scr, l_scr, acc_scr):
    j = pl.program_id(1)
    ld = q_ref.shape[1]
    dh = HEAD_DIM_A

    @pl.when(j == 0)
    def _():
        m_scr[...] = jnp.full(m_scr.shape, -jnp.inf, F32)
        l_scr[...] = jnp.zeros(l_scr.shape, F32)
        acc_scr[...] = jnp.zeros(acc_scr.shape, F32)

    def head_step(h, kh, vh, ck, mask):
        sl = slice(h * dh, (h + 1) * dh)
        s = lax.dot_general(q_ref[0, :, sl], kh, _NT, preferred_element_type=F32)
        s = s + (cumn_ref[0, :ld, h:h + 1] - ck) * LOG2E
        if mask is not None:
            s = jnp.where(mask, s, -jnp.inf)
        m, l, acc = _softmax_step(s, vh, m_scr[h], l_scr[h], acc_scr[:, sl])
        m_scr[h] = m
        l_scr[h] = l
        acc_scr[:, sl] = acc

    for h in range(N_HEADS_A):
        sl = slice(h * dh, (h + 1) * dh)
        head_step(h, kc_ref[0, :, h, :].astype(BF16), vc_ref[0, :, h, :].astype(BF16),
                  cumtp_ref[0, h:h + 1, :], None)

    @pl.when(j == pl.num_programs(1) - 1)
    def _():
        row = lax.broadcasted_iota(jnp.int32, (ld, ld), 0)
        col = lax.broadcasted_iota(jnp.int32, (ld, ld), 1)
        for h in range(N_HEADS_A):
            sl = slice(h * dh, (h + 1) * dh)
            head_step(h, kn_ref[0, :, sl], vn_ref[0, :, sl], cumtn_ref[0, h:h + 1, :ld], col <= row)
            o_ref[0, :, sl] = (acc_scr[:, sl] / l_scr[h]).astype(o_ref.dtype)


def _attention_cached(q, kn, vn, kc, vc, cum_new, cumt_past, cumt_new):
    b, ld, _ = q.shape
    lp = kc.shape[1]
    tk = min(DEC_KV_TILE, lp)
    assert lp % tk == 0
    lpad = cum_new.shape[1]
    new = pl.BlockSpec((1, ld, D_ATTN), lambda bi, j: (bi, 0, 0))
    past = pl.BlockSpec((1, tk, N_HEADS_A, HEAD_DIM_A), lambda bi, j: (bi, j, 0, 0))
    return pl.pallas_call(
        _attn_cached_kernel,
        grid=(b, lp // tk),
        in_specs=[new, new, new, past, past,
                  pl.BlockSpec((1, lpad, LANES), lambda bi, j: (bi, 0, 0)),
                  pl.BlockSpec((1, N_HEADS_A, tk), lambda bi, j: (bi, 0, j)),
                  pl.BlockSpec((1, N_HEADS_A, lpad), lambda bi, j: (bi, 0, 0))],
        out_specs=new,
        out_shape=jax.ShapeDtypeStruct((b, ld, D_ATTN), BF16),
        scratch_shapes=[pltpu.VMEM((N_HEADS_A, ld, 1), F32), pltpu.VMEM((N_HEADS_A, ld, 1), F32),
                        pltpu.VMEM((ld, D_ATTN), F32)],
        compiler_params=_params(("parallel", "arbitrary")),
    )(q, kn, vn, kc, vc, cum_new, cumt_past, cumt_new)


def _expand_heads(a):
    r = a.shape[0]
    low = lax.broadcasted_iota(jnp.int32, (r, LANES), 1) < HEAD_DIM_S
    return jnp.concatenate(
        [jnp.where(low, a[:, 2 * j:2 * j + 1], a[:, 2 * j + 1:2 * j + 2]) for j in range(N_HEADS_S // 2)], axis=1)


def _ssd_kernel(xbc_ref, z_ref, dt_ref, dtt_ref, past_ref, h0_ref, cw_ref, cb_ref, dtb_ref, dtbt_ref,
                alog_ref, alogt_ref, dsk_ref, nw_ref, o_ref, hout_ref, xbuf, ht_scr):
    c = pl.program_id(1)
    lc = xbc_ref.shape[1]
    hist = SUBLANES

    @pl.when(c == 0)
    def _():
        xbuf[0:hist, :] = past_ref[0]
        ht_scr[...] = h0_ref[0]

    xbuf[hist:hist + lc, :] = xbc_ref[0]
    u = cb_ref[...]
    for w in range(CONV_W):
        off = hist - (CONV_W - 1) + w
        u = u + xbuf[off:off + lc, :] * cw_ref[w:w + 1, :]
    xbuf[0:hist, :] = xbuf[lc:lc + hist, :]
    u = _silu(u)
    xs = u[:, :D_INNER]
    gn = N_GROUPS_S * D_STATE
    bm = u[:, D_INNER:D_INNER + gn].astype(BF16)
    cm = u[:, D_INNER + gn:].astype(BF16)

    dt = _softplus(dt_ref[0] + dtb_ref[...])
    dtt = _softplus(dtt_ref[0] + dtbt_ref[...])
    a = dt * (-jnp.exp(alog_ref[...]))
    at = dtt * (-jnp.exp(alogt_ref[...]))
    row = lax.broadcasted_iota(jnp.int32, (lc, lc), 0)
    col = lax.broadcasted_iota(jnp.int32, (lc, lc), 1)
    causal = col <= row
    a_cs = _tri_dot(causal.astype(BF16), a)
    a_cst = _dot_tri(at, (row <= col).astype(BF16))
    total = a_cs[lc - 1:lc, :]
    dt_e = _expand_heads(dt)
    w_e = _expand_heads(dt * jnp.exp(total - a_cs))
    ea_e = _expand_heads(jnp.exp(a_cs))
    cd_e = _expand_heads(jnp.exp(total))
    xdt = xs * dt_e
    xdw = (xs * w_e).astype(BF16)

    cbs = [lax.dot_general(cm[:, g * D_STATE:(g + 1) * D_STATE], bm[:, g * D_STATE:(g + 1) * D_STATE], _NT,
                           preferred_element_type=F32) for g in range(N_GROUPS_S)]
    low = lax.broadcasted_iota(jnp.int32, (lc, LANES), 1) < HEAD_DIM_S
    heads_per_group = N_HEADS_S // N_GROUPS_S
    yd = []
    for j in range(N_HEADS_S // 2):
        ms = []
        for hh in (2 * j, 2 * j + 1):
            seg = a_cs[:, hh:hh + 1] - a_cst[hh:hh + 1, :]
            dec = jnp.where(causal, jnp.exp(jnp.where(causal, seg, 0.0)), 0.0)
            ms.append((cbs[hh // heads_per_group] * dec).astype(BF16))
        xb = xdt[:, j * LANES:(j + 1) * LANES]
        rhs = jnp.concatenate([jnp.where(low, xb, 0.0), jnp.where(low, 0.0, xb)], axis=0).astype(BF16)
        yd.append(jnp.dot(jnp.concatenate(ms, axis=1), rhs, preferred_element_type=F32))
    y = jnp.concatenate(yd, axis=1)

    half = D_INNER // N_GROUPS_S
    ht = ht_scr[...]
    htb = ht.astype(BF16)
    y_off = jnp.concatenate(
        [jnp.dot(cm[:, g * D_STATE:(g + 1) * D_STATE], htb[:, g * half:(g + 1) * half],
                 preferred_element_type=F32) for g in range(N_GROUPS_S)], axis=1)
    st = jnp.concatenate(
        [lax.dot_general(bm[:, g * D_STATE:(g + 1) * D_STATE], xdw[:, g * half:(g + 1) * half],
                         (((0,), (0,)), ((), ())), preferred_element_type=F32) for g in range(N_GROUPS_S)], axis=1)
    ht_new = cd_e * ht + st
    ht_scr[...] = ht_new
    hout_ref[0] = ht_new

    y = y + y_off * ea_e + dsk_ref[...] * xs
    gz = y * _silu(z_ref[0])
    outs = []
    for g in range(N_GROUPS_S):
        gg = gz[:, g * half:(g + 1) * half]
        outs.append(gg * lax.rsqrt(jnp.mean(gg * gg, axis=-1, keepdims=True) + EPS))
    o_ref[0] = (jnp.concatenate(outs, axis=1) * nw_ref[...]).astype(o_ref.dtype)


def _ssd(xbc, z, dt_raw, conv_past, h0t, conv_w, conv_b, dt_bias, a_log, d_skip, norm_w):
    b, l, _ = xbc.shape
    lc = min(SSD_CHUNK, l)
    assert l % lc == 0 and lc % SUBLANES == 0 and lc >= SUBLANES
    nh = N_HEADS_S
    dtt = jnp.swapaxes(dt_raw, 1, 2)
    past = jnp.pad(conv_past, ((0, 0), (SUBLANES - (CONV_W - 1), 0), (0, 0)))
    const = lambda shape: pl.BlockSpec(shape, lambda i, j: tuple(0 for _ in shape))
    o, hout = pl.pallas_call(
        _ssd_kernel,
        grid=(b, l // lc),
        in_specs=[pl.BlockSpec((1, lc, CONV_DIM), lambda i, j: (i, j, 0)),
                  pl.BlockSpec((1, lc, D_INNER), lambda i, j: (i, j, 0)),
                  pl.BlockSpec((1, lc, nh), lambda i, j: (i, j, 0)),
                  pl.BlockSpec((1, nh, lc), lambda i, j: (i, 0, j)),
                  pl.BlockSpec((1, SUBLANES, CONV_DIM), lambda i, j: (i, 0, 0)),
                  pl.BlockSpec((1, D_STATE, D_INNER), lambda i, j: (i, 0, 0)),
                  const((CONV_W, CONV_DIM)), const((1, CONV_DIM)),
                  const((1, nh)), const((nh, 1)), const((1, nh)), const((nh, 1)),
                  const((1, D_INNER)), const((1, D_INNER))],
        out_specs=[pl.BlockSpec((1, lc, D_INNER), lambda i, j: (i, j, 0)),
                   pl.BlockSpec((1, D_STATE, D_INNER), lambda i, j: (i, 0, 0))],
        out_shape=[jax.ShapeDtypeStruct((b, l, D_INNER), BF16),
                   jax.ShapeDtypeStruct((b, D_STATE, D_INNER), F32)],
        scratch_shapes=[pltpu.VMEM((lc + SUBLANES, CONV_DIM), F32), pltpu.VMEM((D_STATE, D_INNER), F32)],
        compiler_params=_params(("parallel", "arbitrary")),
    )(xbc, z, dt_raw, dtt, past, h0t, conv_w, conv_b.reshape(1, -1),
      dt_bias.reshape(1, nh), dt_bias.reshape(nh, 1), a_log.reshape(1, nh), a_log.reshape(nh, 1),
      jnp.repeat(d_skip, HEAD_DIM_S).reshape(1, -1), norm_w.reshape(1, -1))
    return o, hout


def _outproj_kernel(x_ref, oa_ref, os_ref, gt_ref, sc_ref, sh_ref, g_ref, wa_ref, ws_ref, rw_ref, rb_ref,
                    x1_ref, h2_ref, lg_ref):
    nb, tl, d = x_ref.shape
    oa = oa_ref[...].reshape(nb * tl, -1)
    os_ = os_ref[...].reshape(nb * tl, -1)
    mix = jnp.dot(oa, wa_ref[...], preferred_element_type=F32) + jnp.dot(os_, ws_ref[...], preferred_element_type=F32)
    x1 = x_ref[...] + gt_ref[...] * mix.reshape(nb, tl, d)
    x1_ref[...] = x1
    ms = jnp.mean(x1 * x1, axis=-1, keepdims=True)
    h2 = x1 * lax.rsqrt(ms + EPS) * g_ref[...]
    h2 = h2 * (1.0 + sc_ref[...]) + sh_ref[...]
    h2 = h2.reshape(nb * tl, d)
    _store_row_tiles(h2_ref, h2)
    lg = jnp.dot(h2, rw_ref[...], precision=HIGHEST, preferred_element_type=F32) + rb_ref[...]
    lg_ref[...] = lg.reshape(nb, tl, -1)


def _outproj(x, oa, os_, gt, sc, sh, g, wa, ws, rw, rb):
    b, l, d = x.shape
    nb, tl = _row_blocking(b, l, ROW_TILE)
    row = lambda n: pl.BlockSpec((nb, tl, n), lambda i, j: (i, j, 0))
    mod = pl.BlockSpec((nb, 1, d), lambda i, j: (i, 0, 0))
    const = lambda a: pl.BlockSpec(a.shape, lambda i, j: tuple(0 for _ in a.shape))
    g3 = g.reshape(1, 1, d)
    return pl.pallas_call(
        _outproj_kernel,
        grid=(b // nb, l // tl),
        in_specs=[row(d), row(D_ATTN), row(D_INNER), mod, mod, mod, const(g3), const(wa), const(ws),
                  const(rw), const(rb)],
        out_specs=[row(d), pl.BlockSpec((nb * tl * ROW_TILE_SUBLANES, LANES), lambda i, j: (i * (l // tl) + j, 0)),
                   row(LANES)],
        out_shape=[jax.ShapeDtypeStruct((b, l, d), F32), jax.ShapeDtypeStruct((b * l * ROW_TILE_SUBLANES, LANES), F32),
                   jax.ShapeDtypeStruct((b, l, LANES), F32)],
        compiler_params=_params(("parallel", "parallel")),
    )(x, oa, os_, gt.reshape(b, 1, d), sc.reshape(b, 1, d), sh.reshape(b, 1, d), g3, wa, ws, rw, rb)


def _route_kernel(lg_ref, idx_ref, gate_ref, cnt_ref, carry_scr):
    @pl.when(pl.program_id(0) == 0)
    def _():
        carry_scr[...] = jnp.zeros(carry_scr.shape, F32)

    v = lg_ref[...]
    tr = v.shape[0]
    lane = lax.broadcasted_iota(jnp.int32, (tr, LANES), 1)
    lane_f = lane.astype(F32)
    tops, idxs = [], []
    onehot = jnp.zeros((tr, LANES), F32)
    for _ in range(TOP_K):
        m = jnp.max(v, axis=-1, keepdims=True)
        idx = jnp.min(jnp.where(v == m, lane_f, float(LANES)), axis=-1, keepdims=True)
        hit = lane_f == idx
        v = jnp.where(hit, -jnp.inf, v)
        onehot = onehot + hit.astype(F32)
        tops.append(m)
        idxs.append(idx)
    es = [jnp.exp(t - tops[0]) for t in tops]
    denom = es[0] + es[1] + es[2] + es[3]

    row = lax.broadcasted_iota(jnp.int32, (tr, tr), 0)
    col = lax.broadcasted_iota(jnp.int32, (tr, tr), 1)
    before = jnp.dot((col < row).astype(BF16), onehot.astype(BF16), preferred_element_type=F32) + carry_scr[...]

    idx_out = jnp.zeros((tr, LANES), F32)
    gate_out = jnp.zeros((tr, LANES), F32)
    for k in range(TOP_K):
        rank = jnp.sum(jnp.where(lane_f == idxs[k], before, 0.0), axis=-1, keepdims=True)
        idx_out = jnp.where(lane == k, idxs[k], idx_out)
        idx_out = jnp.where(lane == TOP_K + k, rank, idx_out)
        gate_out = jnp.where(lane == k, es[k] / denom, gate_out)
    idx_ref[...] = idx_out.astype(jnp.int32)
    gate_ref[...] = gate_out
    carry_scr[...] = carry_scr[...] + jnp.sum(onehot, axis=0, keepdims=True)
    cnt_ref[...] = carry_scr[...]


def _route(logits):
    t, n = logits.shape
    tr = min(ROUTE_TILE, t)
    assert t % tr == 0
    return pl.pallas_call(
        _route_kernel,
        grid=(t // tr,),
        in_specs=[pl.BlockSpec((tr, n), lambda i: (i, 0))],
        out_specs=[pl.BlockSpec((tr, n), lambda i: (i, 0)), pl.BlockSpec((tr, n), lambda i: (i, 0)),
                   pl.BlockSpec((1, n), lambda i: (0, 0))],
        out_shape=[jax.ShapeDtypeStruct((t, n), jnp.int32), jax.ShapeDtypeStruct((t, n), F32),
                   jax.ShapeDtypeStruct((1, n), F32)],
        scratch_shapes=[pltpu.VMEM((1, n), F32)],
        compiler_params=_params(("arbitrary",)),
    )(logits)


def _row_tile(ref, r):
    start = r * ROW_TILE_SUBLANES
    if not isinstance(r, int):
        start = pl.multiple_of(start, ROW_TILE_SUBLANES)
    return ref.at[pl.ds(start, ROW_TILE_SUBLANES), :]


def _store_row_tiles(ref, x, tile0=0):
    rows = x.shape[0]
    for s in range(x.shape[1] // LANES):
        ref[pl.ds(tile0 + s, rows, stride=ROW_TILE_SUBLANES), :] = x[:, s * LANES:(s + 1) * LANES]


def _load_row_tiles(ref, row0, rows):
    return jnp.concatenate(
        [ref[pl.ds(row0 * ROW_TILE_SUBLANES + s, rows, stride=ROW_TILE_SUBLANES), :] for s in range(ROW_TILE_SUBLANES)],
        axis=1)


def _gather_rows(idx_ref, src_hbm, dst, sem, n_rows):
    group = 8
    assert n_rows % group == 0

    def body(g, carry):
        for u in range(group):
            r = g * group + u
            pltpu.make_async_copy(_row_tile(src_hbm, idx_ref[0, 0, r]), _row_tile(dst, r), sem).start(priority=u % 2)
        return carry
    lax.fori_loop(0, n_rows // group, body, 0)


def _wait_rows(src_hbm, dst, sem, n_rows):
    pltpu.make_async_copy(src_hbm.at[pl.ds(0, n_rows * ROW_TILE_SUBLANES), :], dst, sem).wait()


EXPERT_COL_CHUNK = 256


def _experts_kernel(be_ref, idx0_ref, idxn_ref, h_hbm, wgu_ref, bgu_ref, wdn_ref, bdn_ref, o_ref, xbuf, sem):
    i = pl.program_id(0)
    n = pl.num_programs(0)
    m = xbuf.shape[1] // ROW_TILE_SUBLANES
    slot = lax.rem(i, 2)
    nxt = 1 - slot
    cw = EXPERT_COL_CHUNK
    tiles_per_piece = cw // LANES
    n_ff, n_out = D_FF // cw, ROW_TILE_SUBLANES // tiles_per_piece
    rows_per_piece = m // (n_ff + n_out)
    assert rows_per_piece * (n_ff + n_out) == m

    @pl.when(i == 0)
    def _():
        _gather_rows(idx0_ref, h_hbm, xbuf.at[0], sem.at[0], m)

    _wait_rows(h_hbm, xbuf.at[slot], sem.at[slot], m)

    def prefetch(piece):
        for r in range(piece * rows_per_piece, (piece + 1) * rows_per_piece):
            pltpu.make_async_copy(_row_tile(h_hbm, idxn_ref[0, 0, r]), _row_tile(xbuf.at[nxt], r),
                                  sem.at[nxt]).start(priority=r % 2)

    x = _load_row_tiles(xbuf.at[slot], 0, m).astype(BF16)
    acts = []
    for c in range(n_ff):
        g = jnp.dot(x, wgu_ref[0, :, c * cw:(c + 1) * cw], preferred_element_type=F32) + bgu_ref[0, :, c * cw:(c + 1) * cw]
        u = (jnp.dot(x, wgu_ref[0, :, D_FF + c * cw:D_FF + (c + 1) * cw], preferred_element_type=F32)
             + bgu_ref[0, :, D_FF + c * cw:D_FF + (c + 1) * cw])
        gate = jnp.minimum(g, SWIGLU_LIMIT)
        up = jnp.clip(u, -SWIGLU_LIMIT, SWIGLU_LIMIT)
        acts.append(((up + 1.0) * gate * jax.nn.sigmoid(SWIGLU_ALPHA * gate)).astype(BF16))
        prefetch(c)
    act = jnp.concatenate(acts, axis=1)
    for c in range(n_out):
        y = jnp.dot(act, wdn_ref[0, :, c * cw:(c + 1) * cw], preferred_element_type=F32) + bdn_ref[0, :, c * cw:(c + 1) * cw]
        _store_row_tiles(o_ref, y, tile0=c * tiles_per_piece)
        prefetch(n_ff + c)

    @pl.when(i == n - 1)
    def _():
        _wait_rows(h_hbm, xbuf.at[nxt], sem.at[nxt], m)


def _experts(block_e, slot_tok, h2, wgu, bgu, wdn, bdn, m):
    n_blocks = block_e.shape[0]
    d = D_MODEL
    mt = m * ROW_TILE_SUBLANES
    idx = slot_tok.reshape(n_blocks, 1, m)
    smem = lambda f: pl.BlockSpec((1, 1, m), f, memory_space=pltpu.SMEM)
    grid_spec = pltpu.PrefetchScalarGridSpec(
        num_scalar_prefetch=1,
        grid=(n_blocks,),
        in_specs=[smem(lambda i, be: (0, 0, 0)),
                  smem(lambda i, be: (jnp.minimum(i + 1, n_blocks - 1), 0, 0)),
                  pl.BlockSpec(memory_space=pl.ANY),
                  pl.BlockSpec((1, d, 2 * D_FF), lambda i, be: (be[i], 0, 0)),
                  pl.BlockSpec((1, 1, 2 * D_FF), lambda i, be: (be[i], 0, 0)),
                  pl.BlockSpec((1, D_FF, d), lambda i, be: (be[i], 0, 0)),
                  pl.BlockSpec((1, 1, d), lambda i, be: (be[i], 0, 0))],
        out_specs=pl.BlockSpec((mt, LANES), lambda i, be: (i, 0)),
        scratch_shapes=[pltpu.VMEM((2, mt, LANES), F32), pltpu.SemaphoreType.DMA((2,))],
    )
    return pl.pallas_call(
        _experts_kernel,
        grid_spec=grid_spec,
        out_shape=jax.ShapeDtypeStruct((n_blocks * mt, LANES), F32),
        compiler_params=_params(("arbitrary",)),
    )(block_e, idx, idx, h2, wgu, bgu.reshape(N_EXPERTS, 1, -1), wdn, bdn.reshape(N_EXPERTS, 1, -1))


def _combine_kernel(idx0_ref, idxn_ref, y_hbm, x1_ref, gate_ref, gt_ref, g_ref, o_ref, ybuf, sem):
    i = pl.program_id(0)
    n = pl.num_programs(0)
    nb, tl, d = x1_ref.shape
    tc = nb * tl
    slot = lax.rem(i, 2)

    @pl.when(i == 0)
    def _():
        _gather_rows(idx0_ref, y_hbm, ybuf.at[0], sem.at[0], TOP_K * tc)

    @pl.when(i + 1 < n)
    def _():
        _gather_rows(idxn_ref, y_hbm, ybuf.at[1 - slot], sem.at[1 - slot], TOP_K * tc)

    _wait_rows(y_hbm, ybuf.at[slot], sem.at[slot], TOP_K * tc)
    gates = gate_ref[...]
    y = gates[:, 0:1] * _load_row_tiles(ybuf.at[slot], 0, tc)
    for k in range(1, TOP_K):
        y = y + gates[:, k:k + 1] * _load_row_tiles(ybuf.at[slot], k * tc, tc)
    x2 = x1_ref[...] + gt_ref[...] * y.reshape(nb, tl, d)
    ms = jnp.mean(x2 * x2, axis=-1, keepdims=True)
    o_ref[...] = x2 * lax.rsqrt(ms + EPS) * g_ref[...]


def _combine(dest, y_slots, x1, gates, gt, g_final):
    b, l, d = x1.shape
    nb, tl = _row_blocking(b, l, COMBINE_TILE)
    tc = nb * tl
    t = b * l
    n_steps = t // tc
    idx = jnp.swapaxes(dest.reshape(n_steps, tc, TOP_K), 1, 2).reshape(n_steps, 1, TOP_K * tc)
    steps_per_batch_row = l // tl
    tok = lambda i: (i // steps_per_batch_row, i % steps_per_batch_row, 0)
    smem = lambda f: pl.BlockSpec((1, 1, TOP_K * tc), f, memory_space=pltpu.SMEM)
    return pl.pallas_call(
        _combine_kernel,
        grid=(n_steps,),
        in_specs=[smem(lambda i: (0, 0, 0)),
                  smem(lambda i: (jnp.minimum(i + 1, n_steps - 1), 0, 0)),
                  pl.BlockSpec(memory_space=pl.ANY),
                  pl.BlockSpec((nb, tl, d), tok),
                  pl.BlockSpec((tc, LANES), lambda i: (i, 0)),
                  pl.BlockSpec((nb, 1, d), lambda i: (i // steps_per_batch_row, 0, 0)),
                  pl.BlockSpec((1, 1, d), lambda i: (0, 0, 0))],
        out_specs=pl.BlockSpec((nb, tl, d), tok),
        out_shape=jax.ShapeDtypeStruct((b, l, d), F32),
        scratch_shapes=[pltpu.VMEM((2, TOP_K * tc * ROW_TILE_SUBLANES, LANES), F32), pltpu.SemaphoreType.DMA((2,))],
        compiler_params=_params(("arbitrary",)),
    )(idx, idx, y_slots, x1, gates, gt.reshape(b, 1, d), g_final.reshape(1, 1, d))


def _moe(h2, logits, x1, gt_m, g_final, wts):
    b, l, d = x1.shape
    t = b * l
    m = MOE_ROWS if t * TOP_K >= 4 * N_EXPERTS * MOE_ROWS else MOE_ROWS_SMALL
    idx, gates, counts = _route(logits.reshape(t, LANES))
    top_e = idx[:, :TOP_K]
    rank = idx[:, TOP_K:2 * TOP_K]
    counts = counts[0, :N_EXPERTS].astype(jnp.int32)
    padded = (counts + m - 1) // m * m
    ends_p = jnp.cumsum(padded)
    starts_p = ends_p - padded
    dest = starts_p[top_e] + rank
    n_blocks = (t * TOP_K + N_EXPERTS * (m - 1) + m - 1) // m
    tok = jnp.broadcast_to(jnp.arange(t, dtype=jnp.int32)[:, None], (t, TOP_K))
    slot_tok = jnp.zeros((n_blocks * m,), jnp.int32).at[dest.reshape(-1)].set(tok.reshape(-1), unique_indices=True)
    block_e = jnp.minimum(
        jnp.searchsorted(ends_p, jnp.arange(n_blocks, dtype=jnp.int32) * m, side='right'),
        N_EXPERTS - 1).astype(jnp.int32)
    y_slots = _experts(block_e, slot_tok, h2, *wts, m)
    return _combine(dest, y_slots, x1, gates, gt_m, g_final)


def _layer(x, mod, k_past, v_past, logf_past, conv_past, ssm_past, p, g_final):
    b, l, d = x.shape
    sh_a, sc_a, gt_a, sh_m, sc_m, gt_m = jnp.split(mod, 6, axis=-1)
    q, kf, vf, kb, vb, z, xbc, sm = _inproj(x, p['g_mix'], sc_a, sh_a, *p['w_in'])
    dt_raw = sm[:, :, N_HEADS_A:N_HEADS_A + N_HEADS_S]

    zero_carry = jnp.zeros((b, 1, LANES), F32)
    if k_past is None:
        logf, qx, kx = _cumsum_bias(sm, p['b_f'])
        o_a = _attention(q, qx, kb, kx, vb)
        conv_past = jnp.zeros((b, CONV_W - 1, CONV_DIM), F32)
        h0t = jnp.zeros((b, D_STATE, D_INNER), F32)
    else:
        lp = k_past.shape[1]
        past = jnp.pad(logf_past, ((0, 0), (0, 0), (0, LANES - N_HEADS_A)))
        _, cum_p, cumt_p = _cumsum(past, p['b_f'], zero_carry, False)
        lpad = -(-l // LANES) * LANES
        sm_pad = jnp.pad(sm, ((0, 0), (0, lpad - l), (0, 0)))
        logf, cum_n, cumt_n = _cumsum(sm_pad, p['b_f'], cum_p[:, lp - 1:lp, :], True)
        logf = logf[:, :l]
        o_a = _attention_cached(q, kb, vb, k_past, v_past, cum_n, cumt_p, cumt_n)
        h0t = jnp.swapaxes(ssm_past.reshape(b, D_INNER, D_STATE), 1, 2)
    o_s, hout = _ssd(xbc, z, dt_raw, conv_past, h0t, p['conv_w'], p['conv_b'], p['dt_bias'], p['a_log'],
                     p['d_skip'], p['ssd_norm_w'])
    x1, h2, logits = _outproj(x, o_a, o_s, gt_a, sc_m, sh_m, p['g_ffn'], p['w_out_a'], p['w_out_s'],
                              p['router_w'], p['router_b'])
    y = _moe(h2, logits, x1, gt_m, g_final, p['experts'])

    assert l >= CONV_W - 1
    conv_new = xbc[:, l - (CONV_W - 1):]
    ssm_new = jnp.swapaxes(hout, 1, 2).reshape(b, N_HEADS_S, HEAD_DIM_S, D_STATE)
    return (y, kf.reshape(b, l, N_HEADS_A, HEAD_DIM_A), vf.reshape(b, l, N_HEADS_A, HEAD_DIM_A),
            logf[:, :, :N_HEADS_A], conv_new, ssm_new)


def kernel(x_prompt, x_sample, cache_k, cache_v, cache_logf, state_conv, state_ssm, c_prompt, c_sample, w_ada, b_ada, g_mix, w_in, b_f, conv_w, conv_b, dt_bias, a_log, d_skip, ssd_norm_w, w_out, g_ffn, router_w, router_b, w_gate_up, b_gate_up, w_down, b_down, g_final):
    assert w_ada.shape[0] == 1, "single-layer operation"
    bp = x_prompt.shape[0]
    w = w_in[0]
    q_end, k_end, v_end = D_ATTN, 2 * D_ATTN, 3 * D_ATTN
    f_end = v_end + N_HEADS_A
    z_end = f_end + D_INNER
    xbc_end = z_end + CONV_DIM
    w_small = jnp.concatenate(
        [w[:, v_end:f_end], w[:, xbc_end:], jnp.zeros((D_MODEL, LANES - N_HEADS_A - N_HEADS_S), F32)], axis=1)
    cast = lambda a: a.astype(BF16)
    p = {
        'g_mix': g_mix[0],
        'w_in': (cast(w[:, :q_end]), cast(w[:, q_end:k_end]), cast(w[:, k_end:v_end]), cast(w[:, f_end:z_end]),
                 cast(w[:, z_end:xbc_end]), cast(w_small)),
        'b_f': jnp.pad(b_f[0], (0, LANES - N_HEADS_A)).reshape(1, LANES),
        'conv_w': conv_w[0], 'conv_b': conv_b[0], 'dt_bias': dt_bias[0], 'a_log': a_log[0],
        'd_skip': d_skip[0], 'ssd_norm_w': ssd_norm_w[0],
        'w_out_a': cast(w_out[0, :D_ATTN]), 'w_out_s': cast(w_out[0, D_ATTN:]),
        'g_ffn': g_ffn[0],
        'router_w': jnp.pad(router_w[0], ((0, 0), (0, LANES - N_EXPERTS))),
        'router_b': jnp.pad(router_b[0], (0, LANES - N_EXPERTS), constant_values=-jnp.inf).reshape(1, LANES),
        'experts': (cast(w_gate_up[0]), b_gate_up[0], cast(w_down[0]), b_down[0]),
    }
    mod = _adaln(jnp.concatenate([c_prompt, c_sample], axis=0), w_ada[0], b_ada[0])
    outs_p = _layer(x_prompt, mod[:bp], None, None, None, None, None, p, g_final)
    outs_s = _layer(x_sample, mod[bp:], cache_k[0], cache_v[0], cache_logf[0], state_conv[0], state_ssm[0], p, g_final)
    stack = lambda a: a[None]
    return (outs_p[0], outs_s[0]) + tuple(stack(a) for a in outs_p[1:]) + tuple(stack(a) for a in outs_s[1:])
```

```python
import functools
import math

import jax
import jax.numpy as jnp
from jax import lax
from jax.experimental import pallas as pl
from jax.experimental.pallas import tpu as pltpu

F32 = jnp.float32
BF16 = jnp.bfloat16
HIGHEST = lax.Precision.HIGHEST

D_MODEL = 1024
N_HEADS_A = 8
HEAD_DIM_A = 128
D_ATTN = N_HEADS_A * HEAD_DIM_A
D_INNER = 1024
HEAD_DIM_S = 64
N_HEADS_S = D_INNER // HEAD_DIM_S
N_GROUPS_S = 2
D_STATE = 128
CONV_W = 4
CONV_DIM = D_INNER + 2 * N_GROUPS_S * D_STATE
N_EXPERTS = 32
TOP_K = 4
D_FF = 1024
SWIGLU_LIMIT = 7.0
SWIGLU_ALPHA = 1.702
EPS = 1e-5
LANES = 128
SUBLANES = 8
ROW_TILE_SUBLANES = D_MODEL // LANES
assert ROW_TILE_SUBLANES == SUBLANES
VMEM_LIMIT = 56 * 1024 * 1024
LOG2E = 1.4426950408889634
ATT_HEADS_PER_STEP = 4

ROW_TILE = 512
ATT_TILE = 512
DEC_KV_TILE = 512
CUM_TILE = 256
SSD_CHUNK = 128
ROUTE_TILE = 256
MOE_ROWS = 512
MOE_ROWS_SMALL = 128
COMBINE_TILE = 256


def _params(sem, vmem=VMEM_LIMIT):
    return pltpu.CompilerParams(dimension_semantics=sem, vmem_limit_bytes=vmem)


def _row_blocking(b, l, tile):
    if l >= tile:
        assert l % tile == 0
        return 1, tile
    nb = min(b, max(1, tile // l))
    while b % nb:
        nb -= 1
    return nb, l


def _split3(x):
    hi = x.astype(BF16)
    r1 = x - hi.astype(F32)
    mid = r1.astype(BF16)
    lo = (r1 - mid.astype(F32)).astype(BF16)
    return hi, mid, lo


def _split2(x):
    hi = x.astype(BF16)
    return hi, (x - hi.astype(F32)).astype(BF16)


def _tri_dot(tri, x):
    hi, mid, lo = _split3(x)
    d = lambda p: jnp.dot(tri, p, preferred_element_type=F32)
    return d(hi) + d(mid) + d(lo)


def _dot_tri(x, tri):
    hi, mid, lo = _split3(x)
    d = lambda p: jnp.dot(p, tri, preferred_element_type=F32)
    return d(hi) + d(mid) + d(lo)


def _silu(x):
    return x * jax.nn.sigmoid(x)


def _softplus(x):
    return jnp.maximum(x, 0.0) + jnp.log1p(jnp.exp(-jnp.abs(x)))


def _log_sigmoid(x):
    return jnp.minimum(x, 0.0) - jnp.log1p(jnp.exp(-jnp.abs(x)))


def _adaln_kernel(c_ref, w_ref, b_ref, o_ref):
    s = _silu(c_ref[...])
    o_ref[...] = jnp.dot(s, w_ref[...], precision=HIGHEST, preferred_element_type=F32) + b_ref[...]


def _adaln(c, w, b):
    m, d = c.shape
    n = w.shape[1]
    tn = 512
    return pl.pallas_call(
        _adaln_kernel,
        grid=(n // tn,),
        in_specs=[pl.BlockSpec((m, d), lambda j: (0, 0)),
                  pl.BlockSpec((d, tn), lambda j: (0, j)),
                  pl.BlockSpec((1, tn), lambda j: (0, j))],
        out_specs=pl.BlockSpec((m, tn), lambda j: (0, j)),
        out_shape=jax.ShapeDtypeStruct((m, n), F32),
        compiler_params=_params(("parallel",)),
    )(c, w, b.reshape(1, n))


def _inproj_kernel(x_ref, g_ref, sc_ref, sh_ref, wq_ref, wk_ref, wv_ref, wz_ref, wx_ref, ws_ref,
                   q_ref, kf_ref, vf_ref, kb_ref, vb_ref, z_ref, xbc_ref, sm_ref):
    x = x_ref[...]
    nb, tl, d = x.shape
    ms = jnp.mean(x * x, axis=-1, keepdims=True)
    h = x * lax.rsqrt(ms + EPS) * g_ref[...]
    h = h * (1.0 + sc_ref[...]) + sh_ref[...]
    hb = h.reshape(nb * tl, d).astype(BF16)

    def mm(w_ref):
        return jnp.dot(hb, w_ref[...], preferred_element_type=F32)

    q = mm(wq_ref) * (LOG2E / math.sqrt(HEAD_DIM_A))
    q_ref[...] = q.astype(BF16).reshape(nb, tl, -1)
    k = mm(wk_ref)
    kf_ref[...] = k.reshape(nb, tl, -1)
    kb_ref[...] = k.astype(BF16).reshape(nb, tl, -1)
    v = mm(wv_ref)
    vf_ref[...] = v.reshape(nb, tl, -1)
    vb_ref[...] = v.astype(BF16).reshape(nb, tl, -1)
    z_ref[...] = mm(wz_ref).reshape(nb, tl, -1)
    xbc_ref[...] = mm(wx_ref).reshape(nb, tl, -1)
    sm_ref[...] = mm(ws_ref).reshape(nb, tl, -1)


def _inproj(x, g, sc, sh, wq, wk, wv, wz, wx, ws):
    b, l, d = x.shape
    nb, tl = _row_blocking(b, l, ROW_TILE)
    grid = (b // nb, l // tl)
    row = lambda n: pl.BlockSpec((nb, tl, n), lambda i, j: (i, j, 0))
    mod = pl.BlockSpec((nb, 1, d), lambda i, j: (i, 0, 0))
    wspec = lambda w: pl.BlockSpec(w.shape, lambda i, j: (0, 0), pipeline_mode=pl.Buffered(1))
    outs = [(D_ATTN, BF16), (D_ATTN, F32), (D_ATTN, F32), (D_ATTN, BF16), (D_ATTN, BF16),
            (D_INNER, F32), (CONV_DIM, F32), (LANES, F32)]
    return pl.pallas_call(
        _inproj_kernel,
        grid=grid,
        in_specs=[row(d), pl.BlockSpec((1, 1, d), lambda i, j: (0, 0, 0)), mod, mod,
                  wspec(wq), wspec(wk), wspec(wv), wspec(wz), wspec(wx), wspec(ws)],
        out_specs=[row(n) for n, _ in outs],
        out_shape=[jax.ShapeDtypeStruct((b, l, n), dt) for n, dt in outs],
        compiler_params=_params(("parallel", "parallel")),
    )(x, g.reshape(1, 1, d), sc.reshape(b, 1, d), sh.reshape(b, 1, d), wq, wk, wv, wz, wx, ws)


def _running_sum(v_ref, bias_ref, carry_ref, carry_scr, apply_logsig):
    @pl.when(pl.program_id(1) == 0)
    def _():
        carry_scr[...] = carry_ref[0]

    v = v_ref[0]
    tl = v.shape[0]
    lf = _log_sigmoid(v + bias_ref[...]) if apply_logsig else v
    row = lax.broadcasted_iota(jnp.int32, (tl, tl), 0)
    col = lax.broadcasted_iota(jnp.int32, (tl, tl), 1)
    tri = (row >= col).astype(BF16)
    cs = _tri_dot(tri, lf) + carry_scr[...]
    carry_scr[...] = cs[tl - 1:tl, :]
    return lf, cs


def _cumsum_kernel(v_ref, bias_ref, carry_ref, logf_ref, cum_ref, cumt_ref, carry_scr, *, apply_logsig):
    lf, cs = _running_sum(v_ref, bias_ref, carry_ref, carry_scr, apply_logsig)
    logf_ref[0] = lf
    cum_ref[0] = cs
    cumt_ref[0] = cs.T[:N_HEADS_A, :]


def _cumsum(vals, bias, carry, apply_logsig):
    b, l, n = vals.shape
    tl = min(CUM_TILE, l)
    assert l % tl == 0 and tl % LANES == 0
    return pl.pallas_call(
        functools.partial(_cumsum_kernel, apply_logsig=apply_logsig),
        grid=(b, l // tl),
        in_specs=[pl.BlockSpec((1, tl, n), lambda i, j: (i, j, 0)),
                  pl.BlockSpec((1, n), lambda i, j: (0, 0)),
                  pl.BlockSpec((1, 1, n), lambda i, j: (i, 0, 0))],
        out_specs=[pl.BlockSpec((1, tl, n), lambda i, j: (i, j, 0)),
                   pl.BlockSpec((1, tl, n), lambda i, j: (i, j, 0)),
                   pl.BlockSpec((1, N_HEADS_A, tl), lambda i, j: (i, 0, j))],
        out_shape=[jax.ShapeDtypeStruct((b, l, n), F32),
                   jax.ShapeDtypeStruct((b, l, n), F32),
                   jax.ShapeDtypeStruct((b, N_HEADS_A, l), F32)],
        scratch_shapes=[pltpu.VMEM((1, n), F32)],
        compiler_params=_params(("parallel", "arbitrary")),
    )(vals, bias, carry)


N_BIAS_TERMS = 3


def _cumsum_bias_kernel(v_ref, bias_ref, carry_ref, logf_ref, qx_ref, kx_ref, carry_scr):
    lf, cs = _running_sum(v_ref, bias_ref, carry_ref, carry_scr, True)
    logf_ref[0] = lf
    tl = cs.shape[0]
    c2 = cs * LOG2E
    lane = lax.broadcasted_iota(jnp.int32, (tl, LANES), 1)
    ones_q = jnp.where((lane >= N_BIAS_TERMS) & (lane < 2 * N_BIAS_TERMS), 1.0, 0.0)
    ones_k = jnp.where(lane < N_BIAS_TERMS, 1.0, 0.0)
    for h in range(N_HEADS_A):
        terms = [t.astype(F32) for t in _split3(c2[:, h:h + 1])]
        qx, kx = ones_q, ones_k
        for n, t in enumerate(terms):
            qx = jnp.where(lane == n, t, qx)
            kx = jnp.where(lane == N_BIAS_TERMS + n, -t, kx)
        qx_ref[0, :, h * LANES:(h + 1) * LANES] = qx.astype(BF16)
        kx_ref[0, :, h * LANES:(h + 1) * LANES] = kx.astype(BF16)


def _cumsum_bias(vals, bias):
    b, l, n = vals.shape
    tl = min(CUM_TILE, l)
    assert l % tl == 0
    row = lambda w: pl.BlockSpec((1, tl, w), lambda i, j: (i, j, 0))
    return pl.pallas_call(
        _cumsum_bias_kernel,
        grid=(b, l // tl),
        in_specs=[row(n), pl.BlockSpec((1, n), lambda i, j: (0, 0)), pl.BlockSpec((1, 1, n), lambda i, j: (i, 0, 0))],
        out_specs=[row(n), row(D_ATTN), row(D_ATTN)],
        out_shape=[jax.ShapeDtypeStruct((b, l, n), F32), jax.ShapeDtypeStruct((b, l, D_ATTN), BF16),
                   jax.ShapeDtypeStruct((b, l, D_ATTN), BF16)],
        scratch_shapes=[pltpu.VMEM((1, n), F32)],
        compiler_params=_params(("parallel", "arbitrary")),
    )(vals, bias, jnp.zeros((b, 1, n), F32))


_NT = (((1,), (1,)), ((), ()))


def _softmax_step(s, v, m_prev, l_prev, acc_prev):
    m_new = jnp.maximum(m_prev, jnp.max(s, axis=-1, keepdims=True))
    p = jnp.exp2(s - m_new)
    alpha = jnp.exp2(m_prev - m_new)
    l_new = alpha * l_prev + jnp.sum(p, axis=-1, keepdims=True)
    acc_new = alpha * acc_prev + jnp.dot(p.astype(BF16), v, preferred_element_type=F32)
    return m_new, l_new, acc_new


def _attn_kernel(q_ref, qx_ref, k_ref, kx_ref, v_ref, o_ref, m_scr, acc_scr, *, tile, heads):
    i = pl.program_id(2)
    dh = HEAD_DIM_A
    n_chunks = tile // LANES
    ones = jnp.ones((tile, dh), BF16)
    m_scr[...] = jnp.full(m_scr.shape, -jnp.inf, F32)
    acc_scr[...] = jnp.zeros(acc_scr.shape, F32)
    q2 = [jnp.concatenate([q_ref[0, :, hh * dh:(hh + 1) * dh], qx_ref[0, :, hh * dh:(hh + 1) * dh]], axis=1)
          for hh in range(heads)]

    def update(j, mask):
        start = pl.multiple_of(j * tile, tile)
        rows = pl.ds(start, tile)
        for hh in range(heads):
            sl = slice(hh * dh, (hh + 1) * dh)
            k2 = jnp.concatenate([k_ref[0, rows, sl], kx_ref[0, rows, sl]], axis=1)
            v2 = jnp.concatenate([v_ref[0, rows, sl], ones], axis=1)
            s = lax.dot_general(q2[hh], k2, _NT, preferred_element_type=F32)
            if mask is not None:
                s = jnp.where(mask, s, -jnp.inf)
            chunks = [s[:, c * LANES:(c + 1) * LANES] for c in range(n_chunks)]
            cmax = functools.reduce(jnp.maximum, chunks)
            m_prev = m_scr[hh]
            m_new = jnp.maximum(m_prev, jnp.max(cmax, axis=-1, keepdims=True))
            alpha = jnp.exp2(m_prev - m_new)
            p = jnp.concatenate([jnp.exp2(c - m_new) for c in chunks], axis=1).astype(BF16)
            pv = jnp.dot(p, v2, preferred_element_type=F32)
            m_scr[hh] = m_new
            acc_scr[hh, :, :dh] = alpha * acc_scr[hh, :, :dh] + pv[:, :dh]
            acc_scr[hh, :, dh:] = alpha * acc_scr[hh, :, dh:] + pv[:, dh:]

    def body(j, carry):
        update(j, None)
        return carry

    lax.fori_loop(0, i, body, 0)
    row = lax.broadcasted_iota(jnp.int32, (tile, tile), 0)
    col = lax.broadcasted_iota(jnp.int32, (tile, tile), 1)
    update(i, col <= row)
    for hh in range(heads):
        o_ref[0, :, hh * dh:(hh + 1) * dh] = (acc_scr[hh, :, :dh] / acc_scr[hh, :, dh:]).astype(o_ref.dtype)


def _attention(q, qx, k, kx, v):
    b, l, _ = q.shape
    tile = ATT_TILE if l >= 2 * ATT_TILE else LANES
    assert l % tile == 0
    heads = ATT_HEADS_PER_STEP
    w = heads * HEAD_DIM_A
    qspec = pl.BlockSpec((1, tile, w), lambda bi, h, i: (bi, i, h))
    kspec = pl.BlockSpec((1, l, w), lambda bi, h, i: (bi, 0, h))
    return pl.pallas_call(
        functools.partial(_attn_kernel, tile=tile, heads=heads),
        grid=(b, N_HEADS_A // heads, l // tile),
        in_specs=[qspec, qspec, kspec, kspec, kspec],
        out_specs=qspec,
        out_shape=jax.ShapeDtypeStruct((b, l, D_ATTN), BF16),
        scratch_shapes=[pltpu.VMEM((heads, tile, LANES), F32), pltpu.VMEM((heads, tile, 2 * HEAD_DIM_A), F32)],
        compiler_params=_params(("parallel", "parallel", "arbitrary")),
    )(q, qx, k, kx, v)


def _attn_cached_kernel(q_ref, kn_ref, vn_ref, kc_ref, vc_ref, cumn_ref, cumtp_ref, cumtn_ref, o_ref,
                        m_scr, l_scr, acc_scr):
    j = pl.program_id(1)
    ld = q_ref.shape[1]
    dh = HEAD_DIM_A

    @pl.when(j == 0)
    def _():
        m_scr[...] = jnp.full(m_scr.shape, -jnp.inf, F32)
        l_scr[...] = jnp.zeros(l_scr.shape, F32)
        acc_scr[...] = jnp.zeros(acc_scr.shape, F32)

    def head_step(h, kh, vh, ck, mask):
        sl = slice(h * dh, (h + 1) * dh)
        s = lax.dot_general(q_ref[0, :, sl], kh, _NT, preferred_element_type=F32)
        s = s + (cumn_ref[0, :ld, h:h + 1] - ck) * LOG2E
        if mask is not None:
            s = jnp.where(mask, s, -jnp.inf)
        m, l, acc = _softmax_step(s, vh, m_scr[h], l_scr[h], acc_scr[:, sl])
        m_scr[h] = m
        l_scr[h] = l
        acc_scr[:, sl] = acc

    for h in range(N_HEADS_A):
        sl = slice(h * dh, (h + 1) * dh)
        rows = pl.ds(h, kc_ref.shape[1] // N_HEADS_A, stride=N_HEADS_A)
        head_step(h, kc_ref[0, rows, :].astype(BF16), vc_ref[0, rows, :].astype(BF16),
                  cumtp_ref[0, h:h + 1, :], None)

    @pl.when(j == pl.num_programs(1) - 1)
    def _():
        row = lax.broadcasted_iota(jnp.int32, (ld, ld), 0)
        col = lax.broadcasted_iota(jnp.int32, (ld, ld), 1)
        for h in range(N_HEADS_A):
            sl = slice(h * dh, (h + 1) * dh)
            head_step(h, kn_ref[0, :, sl], vn_ref[0, :, sl], cumtn_ref[0, h:h + 1, :ld], col <= row)
            o_ref[0, :, sl] = (acc_scr[:, sl] / l_scr[h]).astype(o_ref.dtype)


def _attention_cached(q, kn, vn, kc, vc, cum_new, cumt_past, cumt_new):
    b, ld, _ = q.shape
    lp = kc.shape[1]
    tk = min(DEC_KV_TILE, lp)
    assert lp % tk == 0
    lpad = cum_new.shape[1]
    new = pl.BlockSpec((1, ld, D_ATTN), lambda bi, j: (bi, 0, 0))
    past = pl.BlockSpec((1, tk * N_HEADS_A, HEAD_DIM_A), lambda bi, j: (bi, j, 0))
    kc = kc.reshape(b, lp * N_HEADS_A, HEAD_DIM_A)
    vc = vc.reshape(b, lp * N_HEADS_A, HEAD_DIM_A)
    return pl.pallas_call(
        _attn_cached_kernel,
        grid=(b, lp // tk),
        in_specs=[new, new, new, past, past,
                  pl.BlockSpec((1, lpad, LANES), lambda bi, j: (bi, 0, 0)),
                  pl.BlockSpec((1, N_HEADS_A, tk), lambda bi, j: (bi, 0, j)),
                  pl.BlockSpec((1, N_HEADS_A, lpad), lambda bi, j: (bi, 0, 0))],
        out_specs=new,
        out_shape=jax.ShapeDtypeStruct((b, ld, D_ATTN), BF16),
        scratch_shapes=[pltpu.VMEM((N_HEADS_A, ld, 1), F32), pltpu.VMEM((N_HEADS_A, ld, 1), F32),
                        pltpu.VMEM((ld, D_ATTN), F32)],
        compiler_params=_params(("parallel", "arbitrary")),
    )(q, kn, vn, kc, vc, cum_new, cumt_past, cumt_new)


def _expand_heads(a):
    r = a.shape[0]
    low = lax.broadcasted_iota(jnp.int32, (r, LANES), 1) < HEAD_DIM_S
    return jnp.concatenate(
        [jnp.where(low, a[:, 2 * j:2 * j + 1], a[:, 2 * j + 1:2 * j + 2]) for j in range(N_HEADS_S // 2)], axis=1)


def _ssd_kernel(xbc_ref, z_ref, dt_ref, dtt_ref, past_ref, h0_ref, cw_ref, cb_ref, dtb_ref, dtbt_ref,
                alog_ref, alogt_ref, dsk_ref, nw_ref, o_ref, hout_ref, xbuf, ht_scr):
    c = pl.program_id(1)
    lc = xbc_ref.shape[1]
    hist = SUBLANES

    @pl.when(c == 0)
    def _():
        xbuf[0:hist, :] = past_ref[0]
        ht_scr[...] = h0_ref[0]

    xbuf[hist:hist + lc, :] = xbc_ref[0]
    u = cb_ref[...]
    for w in range(CONV_W):
        off = hist - (CONV_W - 1) + w
        u = u + xbuf[off:off + lc, :] * cw_ref[w:w + 1, :]
    xbuf[0:hist, :] = xbuf[lc:lc + hist, :]
    u = _silu(u)
    xs = u[:, :D_INNER]
    gn = N_GROUPS_S * D_STATE
    bm = u[:, D_INNER:D_INNER + gn].astype(BF16)
    cm = u[:, D_INNER + gn:].astype(BF16)

    dt = _softplus(dt_ref[0] + dtb_ref[...])
    dtt = _softplus(dtt_ref[0] + dtbt_ref[...])
    a = dt * (-jnp.exp(alog_ref[...]))
    at = dtt * (-jnp.exp(alogt_ref[...]))
    row = lax.broadcasted_iota(jnp.int32, (lc, lc), 0)
    col = lax.broadcasted_iota(jnp.int32, (lc, lc), 1)
    causal = col <= row
    a_cs = _tri_dot(causal.astype(BF16), a)
    a_cst = _dot_tri(at, (row <= col).astype(BF16))
    total = a_cs[lc - 1:lc, :]
    dt_e = _expand_heads(dt)
    w_e = _expand_heads(dt * jnp.exp(total - a_cs))
    ea_e = _expand_heads(jnp.exp(a_cs))
    cd_e = _expand_heads(jnp.exp(total))
    xdt = xs * dt_e
    xdw = (xs * w_e).astype(BF16)

    cbs = [lax.dot_general(cm[:, g * D_STATE:(g + 1) * D_STATE], bm[:, g * D_STATE:(g + 1) * D_STATE], _NT,
                           preferred_element_type=F32) for g in range(N_GROUPS_S)]
    low = lax.broadcasted_iota(jnp.int32, (lc, LANES), 1) < HEAD_DIM_S
    heads_per_group = N_HEADS_S // N_GROUPS_S
    yd = []
    for j in range(N_HEADS_S // 2):
        ms = []
        for hh in (2 * j, 2 * j + 1):
            seg = a_cs[:, hh:hh + 1] - a_cst[hh:hh + 1, :]
            dec = jnp.where(causal, jnp.exp(jnp.where(causal, seg, 0.0)), 0.0)
            ms.append((cbs[hh // heads_per_group] * dec).astype(BF16))
        xb = xdt[:, j * LANES:(j + 1) * LANES]
        rhs = jnp.concatenate([jnp.where(low, xb, 0.0), jnp.where(low, 0.0, xb)], axis=0).astype(BF16)
        yd.append(jnp.dot(jnp.concatenate(ms, axis=1), rhs, preferred_element_type=F32))
    y = jnp.concatenate(yd, axis=1)

    half = D_INNER // N_GROUPS_S
    ht = ht_scr[...]
    htb = ht.astype(BF16)
    y_off = jnp.concatenate(
        [jnp.dot(cm[:, g * D_STATE:(g + 1) * D_STATE], htb[:, g * half:(g + 1) * half],
                 preferred_element_type=F32) for g in range(N_GROUPS_S)], axis=1)
    st = jnp.concatenate(
        [lax.dot_general(bm[:, g * D_STATE:(g + 1) * D_STATE], xdw[:, g * half:(g + 1) * half],
                         (((0,), (0,)), ((), ())), preferred_element_type=F32) for g in range(N_GROUPS_S)], axis=1)
    ht_new = cd_e * ht + st
    ht_scr[...] = ht_new
    hout_ref[0] = ht_new

    y = y + y_off * ea_e + dsk_ref[...] * xs
    gz = y * _silu(z_ref[0])
    outs = []
    for g in range(N_GROUPS_S):
        gg = gz[:, g * half:(g + 1) * half]
        outs.append(gg * lax.rsqrt(jnp.mean(gg * gg, axis=-1, keepdims=True) + EPS))
    o_ref[0] = (jnp.concatenate(outs, axis=1) * nw_ref[...]).astype(o_ref.dtype)


def _ssd(xbc, z, dt_raw, conv_past, h0t, conv_w, conv_b, dt_bias, a_log, d_skip, norm_w):
    b, l, _ = xbc.shape
    lc = min(SSD_CHUNK, l)
    assert l % lc == 0 and lc % SUBLANES == 0 and lc >= SUBLANES
    nh = N_HEADS_S
    dtt = jnp.swapaxes(dt_raw, 1, 2)
    past = jnp.pad(conv_past, ((0, 0), (SUBLANES - (CONV_W - 1), 0), (0, 0)))
    const = lambda shape: pl.BlockSpec(shape, lambda i, j: tuple(0 for _ in shape))
    o, hout = pl.pallas_call(
        _ssd_kernel,
        grid=(b, l // lc),
        in_specs=[pl.BlockSpec((1, lc, CONV_DIM), lambda i, j: (i, j, 0)),
                  pl.BlockSpec((1, lc, D_INNER), lambda i, j: (i, j, 0)),
                  pl.BlockSpec((1, lc, nh), lambda i, j: (i, j, 0)),
                  pl.BlockSpec((1, nh, lc), lambda i, j: (i, 0, j)),
                  pl.BlockSpec((1, SUBLANES, CONV_DIM), lambda i, j: (i, 0, 0)),
                  pl.BlockSpec((1, D_STATE, D_INNER), lambda i, j: (i, 0, 0)),
                  const((CONV_W, CONV_DIM)), const((1, CONV_DIM)),
                  const((1, nh)), const((nh, 1)), const((1, nh)), const((nh, 1)),
                  const((1, D_INNER)), const((1, D_INNER))],
        out_specs=[pl.BlockSpec((1, lc, D_INNER), lambda i, j: (i, j, 0)),
                   pl.BlockSpec((1, D_STATE, D_INNER), lambda i, j: (i, 0, 0))],
        out_shape=[jax.ShapeDtypeStruct((b, l, D_INNER), BF16),
                   jax.ShapeDtypeStruct((b, D_STATE, D_INNER), F32)],
        scratch_shapes=[pltpu.VMEM((lc + SUBLANES, CONV_DIM), F32), pltpu.VMEM((D_STATE, D_INNER), F32)],
        compiler_params=_params(("parallel", "arbitrary")),
    )(xbc, z, dt_raw, dtt, past, h0t, conv_w, conv_b.reshape(1, -1),
      dt_bias.reshape(1, nh), dt_bias.reshape(nh, 1), a_log.reshape(1, nh), a_log.reshape(nh, 1),
      jnp.repeat(d_skip, HEAD_DIM_S).reshape(1, -1), norm_w.reshape(1, -1))
    return o, hout


def _outproj_kernel(x_ref, oa_ref, os_ref, gt_ref, sc_ref, sh_ref, g_ref, wa_ref, ws_ref, rw_ref, rb_ref,
                    x1_ref, h2_ref, lg_ref):
    nb, tl, d = x_ref.shape
    oa = oa_ref[...].reshape(nb * tl, -1)
    os_ = os_ref[...].reshape(nb * tl, -1)
    mix = jnp.dot(oa, wa_ref[...], preferred_element_type=F32) + jnp.dot(os_, ws_ref[...], preferred_element_type=F32)
    x1 = x_ref[...] + gt_ref[...] * mix.reshape(nb, tl, d)
    x1_ref[...] = x1
    ms = jnp.mean(x1 * x1, axis=-1, keepdims=True)
    h2 = x1 * lax.rsqrt(ms + EPS) * g_ref[...]
    h2 = h2 * (1.0 + sc_ref[...]) + sh_ref[...]
    h2 = h2.reshape(nb * tl, d)
    _store_row_tiles(h2_ref, h2)
    h_hi, h_lo = _split2(h2)
    w_hi, w_lo = rw_ref[0], rw_ref[1]
    lg = (jnp.dot(h_hi, w_hi, preferred_element_type=F32) + jnp.dot(h_lo, w_hi, preferred_element_type=F32)
          + jnp.dot(h_hi, w_lo, preferred_element_type=F32)) + rb_ref[...]
    lg_ref[...] = lg.reshape(nb, tl, -1)


def _outproj(x, oa, os_, gt, sc, sh, g, wa, ws, rw, rb):
    b, l, d = x.shape
    nb, tl = _row_blocking(b, l, ROW_TILE)
    row = lambda n: pl.BlockSpec((nb, tl, n), lambda i, j: (i, j, 0))
    mod = pl.BlockSpec((nb, 1, d), lambda i, j: (i, 0, 0))
    const = lambda a: pl.BlockSpec(a.shape, lambda i, j: tuple(0 for _ in a.shape))
    g3 = g.reshape(1, 1, d)
    return pl.pallas_call(
        _outproj_kernel,
        grid=(b // nb, l // tl),
        in_specs=[row(d), row(D_ATTN), row(D_INNER), mod, mod, mod, const(g3), const(wa), const(ws),
                  const(rw), const(rb)],
        out_specs=[row(d), pl.BlockSpec((nb * tl * ROW_TILE_SUBLANES, LANES), lambda i, j: (i * (l // tl) + j, 0)),
                   row(LANES)],
        out_shape=[jax.ShapeDtypeStruct((b, l, d), F32), jax.ShapeDtypeStruct((b * l * ROW_TILE_SUBLANES, LANES), F32),
                   jax.ShapeDtypeStruct((b, l, LANES), F32)],
        compiler_params=_params(("parallel", "parallel")),
    )(x, oa, os_, gt.reshape(b, 1, d), sc.reshape(b, 1, d), sh.reshape(b, 1, d), g3, wa, ws, rw, rb)


def _route_kernel(lg_ref, idx_ref, gate_ref, cnt_ref, carry_scr):
    @pl.when(pl.program_id(0) == 0)
    def _():
        carry_scr[...] = jnp.zeros(carry_scr.shape, F32)

    v = lg_ref[...]
    tr = v.shape[0]
    lane = lax.broadcasted_iota(jnp.int32, (tr, LANES), 1)
    lane_f = lane.astype(F32)
    tops, idxs = [], []
    onehot = jnp.zeros((tr, LANES), F32)
    for _ in range(TOP_K):
        m = jnp.max(v, axis=-1, keepdims=True)
        idx = jnp.min(jnp.where(v == m, lane_f, float(LANES)), axis=-1, keepdims=True)
        hit = lane_f == idx
        v = jnp.where(hit, -jnp.inf, v)
        onehot = onehot + hit.astype(F32)
        tops.append(m)
        idxs.append(idx)
    es = [jnp.exp(t - tops[0]) for t in tops]
    denom = es[0] + es[1] + es[2] + es[3]

    row = lax.broadcasted_iota(jnp.int32, (tr, tr), 0)
    col = lax.broadcasted_iota(jnp.int32, (tr, tr), 1)
    before = jnp.dot((col < row).astype(BF16), onehot.astype(BF16), preferred_element_type=F32) + carry_scr[...]

    idx_out = jnp.zeros((tr, LANES), F32)
    gate_out = jnp.zeros((tr, LANES), F32)
    for k in range(TOP_K):
        rank = jnp.sum(jnp.where(lane_f == idxs[k], before, 0.0), axis=-1, keepdims=True)
        idx_out = jnp.where(lane == k, idxs[k], idx_out)
        idx_out = jnp.where(lane == TOP_K + k, rank, idx_out)
        gate_out = jnp.where(lane == k, es[k] / denom, gate_out)
    idx_ref[...] = idx_out.astype(jnp.int32)
    gate_ref[...] = gate_out
    carry_scr[...] = carry_scr[...] + jnp.sum(onehot, axis=0, keepdims=True)
    cnt_ref[...] = carry_scr[...]


def _route(logits):
    t, n = logits.shape
    tr = min(ROUTE_TILE, t)
    assert t % tr == 0
    return pl.pallas_call(
        _route_kernel,
        grid=(t // tr,),
        in_specs=[pl.BlockSpec((tr, n), lambda i: (i, 0))],
        out_specs=[pl.BlockSpec((tr, n), lambda i: (i, 0)), pl.BlockSpec((tr, n), lambda i: (i, 0)),
                   pl.BlockSpec((1, n), lambda i: (0, 0))],
        out_shape=[jax.ShapeDtypeStruct((t, n), jnp.int32), jax.ShapeDtypeStruct((t, n), F32),
                   jax.ShapeDtypeStruct((1, n), F32)],
        scratch_shapes=[pltpu.VMEM((1, n), F32)],
        compiler_params=_params(("arbitrary",)),
    )(logits)


def _row_tile(ref, r):
    start = r * ROW_TILE_SUBLANES
    if not isinstance(r, int):
        start = pl.multiple_of(start, ROW_TILE_SUBLANES)
    return ref.at[pl.ds(start, ROW_TILE_SUBLANES), :]


def _store_row_tiles(ref, x, tile0=0):
    rows = x.shape[0]
    for s in range(x.shape[1] // LANES):
        ref[pl.ds(tile0 + s, rows, stride=ROW_TILE_SUBLANES), :] = x[:, s * LANES:(s + 1) * LANES]


def _load_row_tiles(ref, row0, rows):
    return jnp.concatenate(
        [ref[pl.ds(row0 * ROW_TILE_SUBLANES + s, rows, stride=ROW_TILE_SUBLANES), :] for s in range(ROW_TILE_SUBLANES)],
        axis=1)


def _gather_rows(idx_ref, src_hbm, dst, sem, n_rows):
    group = 8
    assert n_rows % group == 0

    def body(g, carry):
        for u in range(group):
            r = g * group + u
            pltpu.make_async_copy(_row_tile(src_hbm, idx_ref[0, 0, r]), _row_tile(dst, r), sem).start(priority=u % 2)
        return carry
    lax.fori_loop(0, n_rows // group, body, 0)


def _wait_rows(src_hbm, dst, sem, n_rows):
    pltpu.make_async_copy(src_hbm.at[pl.ds(0, n_rows * ROW_TILE_SUBLANES), :], dst, sem).wait()


EXPERT_COL_CHUNK = 256


GATHER_BUFS = 3


def _experts_kernel(be_ref, idx0_ref, idx1_ref, idxn_ref, h_hbm, wgu_ref, bgu_ref, wdn_ref, bdn_ref, o_ref, xbuf, sem):
    i = pl.program_id(0)
    n = pl.num_programs(0)
    m = xbuf.shape[1] // ROW_TILE_SUBLANES
    slot = lax.rem(i, GATHER_BUFS)
    nxt = lax.rem(i + 2, GATHER_BUFS)
    cw = EXPERT_COL_CHUNK
    tiles_per_piece = cw // LANES
    n_ff, n_out = D_FF // cw, ROW_TILE_SUBLANES // tiles_per_piece
    rows_per_piece = m // (n_ff + n_out)
    assert rows_per_piece * (n_ff + n_out) == m

    @pl.when(i == 0)
    def _():
        _gather_rows(idx0_ref, h_hbm, xbuf.at[0], sem.at[0], m)
        _gather_rows(idx1_ref, h_hbm, xbuf.at[1], sem.at[1], m)

    _wait_rows(h_hbm, xbuf.at[slot], sem.at[slot], m)

    def prefetch(piece):
        for r in range(piece * rows_per_piece, (piece + 1) * rows_per_piece):
            pltpu.make_async_copy(_row_tile(h_hbm, idxn_ref[0, 0, r]), _row_tile(xbuf.at[nxt], r),
                                  sem.at[nxt]).start(priority=r % 2)

    x = _load_row_tiles(xbuf.at[slot], 0, m).astype(BF16)
    acts = []
    for c in range(n_ff):
        g = jnp.dot(x, wgu_ref[0, :, c * cw:(c + 1) * cw], preferred_element_type=F32) + bgu_ref[0, :, c * cw:(c + 1) * cw]
        u = (jnp.dot(x, wgu_ref[0, :, D_FF + c * cw:D_FF + (c + 1) * cw], preferred_element_type=F32)
             + bgu_ref[0, :, D_FF + c * cw:D_FF + (c + 1) * cw])
        gate = jnp.minimum(g, SWIGLU_LIMIT)
        up = jnp.clip(u, -SWIGLU_LIMIT, SWIGLU_LIMIT)
        acts.append(((up + 1.0) * gate * jax.nn.sigmoid(SWIGLU_ALPHA * gate)).astype(BF16))
        prefetch(c)
    act = jnp.concatenate(acts, axis=1)
    for c in range(n_out):
        y = jnp.dot(act, wdn_ref[0, :, c * cw:(c + 1) * cw], preferred_element_type=F32) + bdn_ref[0, :, c * cw:(c + 1) * cw]
        _store_row_tiles(o_ref, y, tile0=c * tiles_per_piece)
        prefetch(n_ff + c)

    @pl.when(i == n - 1)
    def _():
        for ahead in (1, 2):
            s = lax.rem(i + ahead, GATHER_BUFS)
            _wait_rows(h_hbm, xbuf.at[s], sem.at[s], m)


def _experts(block_e, slot_tok, h2, wgu, bgu, wdn, bdn, m):
    n_blocks = block_e.shape[0]
    d = D_MODEL
    mt = m * ROW_TILE_SUBLANES
    idx = slot_tok.reshape(n_blocks, 1, m)
    smem = lambda f: pl.BlockSpec((1, 1, m), f, memory_space=pltpu.SMEM)
    grid_spec = pltpu.PrefetchScalarGridSpec(
        num_scalar_prefetch=1,
        grid=(n_blocks,),
        in_specs=[smem(lambda i, be: (0, 0, 0)),
                  smem(lambda i, be: (min(1, n_blocks - 1), 0, 0)),
                  smem(lambda i, be: (jnp.minimum(i + 2, n_blocks - 1), 0, 0)),
                  pl.BlockSpec(memory_space=pl.ANY),
                  pl.BlockSpec((1, d, 2 * D_FF), lambda i, be: (be[i], 0, 0)),
                  pl.BlockSpec((1, 1, 2 * D_FF), lambda i, be: (be[i], 0, 0)),
                  pl.BlockSpec((1, D_FF, d), lambda i, be: (be[i], 0, 0)),
                  pl.BlockSpec((1, 1, d), lambda i, be: (be[i], 0, 0))],
        out_specs=pl.BlockSpec((mt, LANES), lambda i, be: (i, 0)),
        scratch_shapes=[pltpu.VMEM((GATHER_BUFS, mt, LANES), F32), pltpu.SemaphoreType.DMA((GATHER_BUFS,))],
    )
    return pl.pallas_call(
        _experts_kernel,
        grid_spec=grid_spec,
        out_shape=jax.ShapeDtypeStruct((n_blocks * mt, LANES), F32),
        compiler_params=_params(("arbitrary",)),
    )(block_e, idx, idx, idx, h2, wgu, bgu.reshape(N_EXPERTS, 1, -1), wdn, bdn.reshape(N_EXPERTS, 1, -1))


def _combine_kernel(idx0_ref, idx1_ref, idxn_ref, y_hbm, x1_ref, gate_ref, gt_ref, g_ref, o_ref, ybuf, sem):
    i = pl.program_id(0)
    n = pl.num_programs(0)
    nb, tl, d = x1_ref.shape
    tc = nb * tl
    slot = lax.rem(i, GATHER_BUFS)
    nxt = lax.rem(i + 2, GATHER_BUFS)

    @pl.when(i == 0)
    def _():
        _gather_rows(idx0_ref, y_hbm, ybuf.at[0], sem.at[0], TOP_K * tc)
        _gather_rows(idx1_ref, y_hbm, ybuf.at[1], sem.at[1], TOP_K * tc)

    _wait_rows(y_hbm, ybuf.at[slot], sem.at[slot], TOP_K * tc)

    def prefetch(k):
        for r in range(k * tc, (k + 1) * tc):
            pltpu.make_async_copy(_row_tile(y_hbm, idxn_ref[0, 0, r]), _row_tile(ybuf.at[nxt], r),
                                  sem.at[nxt]).start(priority=r % 2)

    gates = gate_ref[...]
    y = gates[:, 0:1] * _load_row_tiles(ybuf.at[slot], 0, tc)
    prefetch(0)
    for k in range(1, TOP_K):
        y = y + gates[:, k:k + 1] * _load_row_tiles(ybuf.at[slot], k * tc, tc)
        prefetch(k)
    x2 = x1_ref[...] + gt_ref[...] * y.reshape(nb, tl, d)
    ms = jnp.mean(x2 * x2, axis=-1, keepdims=True)
    o_ref[...] = x2 * lax.rsqrt(ms + EPS) * g_ref[...]

    @pl.when(i == n - 1)
    def _():
        for ahead in (1, 2):
            s = lax.rem(i + ahead, GATHER_BUFS)
            _wait_rows(y_hbm, ybuf.at[s], sem.at[s], TOP_K * tc)


def _combine(dest, y_slots, x1, gates, gt, g_final):
    b, l, d = x1.shape
    nb, tl = _row_blocking(b, l, COMBINE_TILE)
    tc = nb * tl
    t = b * l
    n_steps = t // tc
    idx = jnp.swapaxes(dest.reshape(n_steps, tc, TOP_K), 1, 2).reshape(n_steps, 1, TOP_K * tc)
    steps_per_batch_row = l // tl
    tok = lambda i: (i // steps_per_batch_row, i % steps_per_batch_row, 0)
    smem = lambda f: pl.BlockSpec((1, 1, TOP_K * tc), f, memory_space=pltpu.SMEM)
    return pl.pallas_call(
        _combine_kernel,
        grid=(n_steps,),
        in_specs=[smem(lambda i: (0, 0, 0)),
                  smem(lambda i: (min(1, n_steps - 1), 0, 0)),
                  smem(lambda i: (jnp.minimum(i + 2, n_steps - 1), 0, 0)),
                  pl.BlockSpec(memory_space=pl.ANY),
                  pl.BlockSpec((nb, tl, d), tok),
                  pl.BlockSpec((tc, LANES), lambda i: (i, 0)),
                  pl.BlockSpec((nb, 1, d), lambda i: (i // steps_per_batch_row, 0, 0)),
                  pl.BlockSpec((1, 1, d), lambda i: (0, 0, 0))],
        out_specs=pl.BlockSpec((nb, tl, d), tok),
        out_shape=jax.ShapeDtypeStruct((b, l, d), F32),
        scratch_shapes=[pltpu.VMEM((GATHER_BUFS, TOP_K * tc * ROW_TILE_SUBLANES, LANES), F32),
                        pltpu.SemaphoreType.DMA((GATHER_BUFS,))],
        compiler_params=_params(("arbitrary",)),
    )(idx, idx, idx, y_slots, x1, gates, gt.reshape(b, 1, d), g_final.reshape(1, 1, d))


def _moe(h2, logits, x1, gt_m, g_final, wts):
    b, l, d = x1.shape
    t = b * l
    m = MOE_ROWS if t * TOP_K >= 4 * N_EXPERTS * MOE_ROWS else MOE_ROWS_SMALL
    idx, gates, counts = _route(logits.reshape(t, LANES))
    top_e = idx[:, :TOP_K]
    rank = idx[:, TOP_K:2 * TOP_K]
    counts = counts[0, :N_EXPERTS].astype(jnp.int32)
    padded = (counts + m - 1) // m * m
    ends_p = jnp.cumsum(padded)
    starts_p = ends_p - padded
    dest = starts_p[top_e] + rank
    n_blocks = (t * TOP_K + N_EXPERTS * (m - 1) + m - 1) // m
    tok = jnp.broadcast_to(jnp.arange(t, dtype=jnp.int32)[:, None], (t, TOP_K))
    slot_tok = jnp.zeros((n_blocks * m,), jnp.int32).at[dest.reshape(-1)].set(tok.reshape(-1), unique_indices=True)
    block_start = jnp.arange(n_blocks, dtype=jnp.int32)[:, None] * m
    block_e = jnp.minimum(jnp.sum((ends_p[None, :] <= block_start).astype(jnp.int32), axis=1), N_EXPERTS - 1)
    y_slots = _experts(block_e, slot_tok, h2, *wts, m)
    return _combine(dest, y_slots, x1, gates, gt_m, g_final)


def _layer(x, mod, k_past, v_past, logf_past, conv_past, ssm_past, p, g_final):
    b, l, d = x.shape
    sh_a, sc_a, gt_a, sh_m, sc_m, gt_m = jnp.split(mod, 6, axis=-1)
    q, kf, vf, kb, vb, z, xbc, sm = _inproj(x, p['g_mix'], sc_a, sh_a, *p['w_in'])
    dt_raw = sm[:, :, N_HEADS_A:N_HEADS_A + N_HEADS_S]

    zero_carry = jnp.zeros((b, 1, LANES), F32)
    if k_past is None:
        logf, qx, kx = _cumsum_bias(sm, p['b_f'])
        o_a = _attention(q, qx, kb, kx, vb)
        conv_past = jnp.zeros((b, CONV_W - 1, CONV_DIM), F32)
        h0t = jnp.zeros((b, D_STATE, D_INNER), F32)
    else:
        lp = k_past.shape[1]
        past = jnp.pad(logf_past, ((0, 0), (0, 0), (0, LANES - N_HEADS_A)))
        _, cum_p, cumt_p = _cumsum(past, p['b_f'], zero_carry, False)
        lpad = -(-l // LANES) * LANES
        sm_pad = jnp.pad(sm, ((0, 0), (0, lpad - l), (0, 0)))
        logf, cum_n, cumt_n = _cumsum(sm_pad, p['b_f'], cum_p[:, lp - 1:lp, :], True)
        logf = logf[:, :l]
        o_a = _attention_cached(q, kb, vb, k_past, v_past, cum_n, cumt_p, cumt_n)
        h0t = jnp.swapaxes(ssm_past.reshape(b, D_INNER, D_STATE), 1, 2)
    o_s, hout = _ssd(xbc, z, dt_raw, conv_past, h0t, p['conv_w'], p['conv_b'], p['dt_bias'], p['a_log'],
                     p['d_skip'], p['ssd_norm_w'])
    x1, h2, logits = _outproj(x, o_a, o_s, gt_a, sc_m, sh_m, p['g_ffn'], p['w_out_a'], p['w_out_s'],
                              p['router_w'], p['router_b'])
    y = _moe(h2, logits, x1, gt_m, g_final, p['experts'])

    assert l >= CONV_W - 1
    conv_new = xbc[:, l - (CONV_W - 1):]
    ssm_new = jnp.swapaxes(hout, 1, 2).reshape(b, N_HEADS_S, HEAD_DIM_S, D_STATE)
    return (y, kf.reshape(b, l, N_HEADS_A, HEAD_DIM_A), vf.reshape(b, l, N_HEADS_A, HEAD_DIM_A),
            logf[:, :, :N_HEADS_A], conv_new, ssm_new)


def kernel(x_prompt, x_sample, cache_k, cache_v, cache_logf, state_conv, state_ssm, c_prompt, c_sample, w_ada, b_ada, g_mix, w_in, b_f, conv_w, conv_b, dt_bias, a_log, d_skip, ssd_norm_w, w_out, g_ffn, router_w, router_b, w_gate_up, b_gate_up, w_down, b_down, g_final):
    assert w_ada.shape[0] == 1, "single-layer operation"
    bp = x_prompt.shape[0]
    w = w_in[0]
    q_end, k_end, v_end = D_ATTN, 2 * D_ATTN, 3 * D_ATTN
    f_end = v_end + N_HEADS_A
    z_end = f_end + D_INNER
    xbc_end = z_end + CONV_DIM
    w_small = jnp.concatenate(
        [w[:, v_end:f_end], w[:, xbc_end:], jnp.zeros((D_MODEL, LANES - N_HEADS_A - N_HEADS_S), F32)], axis=1)
    cast = lambda a: a.astype(BF16)
    p = {
        'g_mix': g_mix[0],
        'w_in': (cast(w[:, :q_end]), cast(w[:, q_end:k_end]), cast(w[:, k_end:v_end]), cast(w[:, f_end:z_end]),
                 cast(w[:, z_end:xbc_end]), cast(w_small)),
        'b_f': jnp.pad(b_f[0], (0, LANES - N_HEADS_A)).reshape(1, LANES),
        'conv_w': conv_w[0], 'conv_b': conv_b[0], 'dt_bias': dt_bias[0], 'a_log': a_log[0],
        'd_skip': d_skip[0], 'ssd_norm_w': ssd_norm_w[0],
        'w_out_a': cast(w_out[0, :D_ATTN]), 'w_out_s': cast(w_out[0, D_ATTN:]),
        'g_ffn': g_ffn[0],
        'router_w': jnp.stack(_split2(jnp.pad(router_w[0], ((0, 0), (0, LANES - N_EXPERTS))))),
        'router_b': jnp.pad(router_b[0], (0, LANES - N_EXPERTS), constant_values=-jnp.inf).reshape(1, LANES),
        'experts': (cast(w_gate_up[0]), b_gate_up[0], cast(w_down[0]), b_down[0]),
    }
    mod = _adaln(jnp.concatenate([c_prompt, c_sample], axis=0), w_ada[0], b_ada[0])
    outs_p = _layer(x_prompt, mod[:bp], None, None, None, None, None, p, g_final)
    outs_s = _layer(x_sample, mod[bp:], cache_k[0], cache_v[0], cache_logf[0], state_conv[0], state_ssm[0], p, g_final)
    stack = lambda a: a[None]
    return (outs_p[0], outs_s[0]) + tuple(stack(a) for a in outs_p[1:]) + tuple(stack(a) for a in outs_s[1:])
```

```python
import functools
import math

import jax
import jax.numpy as jnp
from jax import lax
from jax.experimental import pallas as pl
from jax.experimental.pallas import tpu as pltpu

F32 = jnp.float32
BF16 = jnp.bfloat16
HIGHEST = lax.Precision.HIGHEST

D_MODEL = 1024
N_HEADS_A = 8
HEAD_DIM_A = 128
D_ATTN = N_HEADS_A * HEAD_DIM_A
D_INNER = 1024
HEAD_DIM_S = 64
N_HEADS_S = D_INNER // HEAD_DIM_S
N_GROUPS_S = 2
D_STATE = 128
CONV_W = 4
CONV_DIM = D_INNER + 2 * N_GROUPS_S * D_STATE
N_EXPERTS = 32
TOP_K = 4
D_FF = 1024
SWIGLU_LIMIT = 7.0
SWIGLU_ALPHA = 1.702
EPS = 1e-5
LANES = 128
SUBLANES = 8
ROW_TILE_SUBLANES = D_MODEL // LANES
assert ROW_TILE_SUBLANES == SUBLANES
VMEM_LIMIT = 56 * 1024 * 1024
LOG2E = 1.4426950408889634
ATT_HEADS_PER_STEP = 4

ROW_TILE = 512
ATT_TILE = 512
DEC_KV_TILE = 512
CUM_TILE = 256
SSD_CHUNK = 128
ROUTE_TILE = 256
MOE_ROWS = 512
MOE_ROWS_SMALL = 128
COMBINE_TILE = 256
DISPATCH_TILE = 256


def _params(sem, vmem=VMEM_LIMIT):
    return pltpu.CompilerParams(dimension_semantics=sem, vmem_limit_bytes=vmem)


def _row_blocking(b, l, tile):
    if l >= tile:
        assert l % tile == 0
        return 1, tile
    nb = min(b, max(1, tile // l))
    while b % nb:
        nb -= 1
    return nb, l


def _split3(x):
    hi = x.astype(BF16)
    r1 = x - hi.astype(F32)
    mid = r1.astype(BF16)
    lo = (r1 - mid.astype(F32)).astype(BF16)
    return hi, mid, lo


def _split2(x):
    hi = x.astype(BF16)
    return hi, (x - hi.astype(F32)).astype(BF16)


def _tri_dot(tri, x):
    hi, mid, lo = _split3(x)
    d = lambda p: jnp.dot(tri, p, preferred_element_type=F32)
    return d(hi) + d(mid) + d(lo)


def _dot_tri(x, tri):
    hi, mid, lo = _split3(x)
    d = lambda p: jnp.dot(p, tri, preferred_element_type=F32)
    return d(hi) + d(mid) + d(lo)


def _silu(x):
    return x * jax.nn.sigmoid(x)


def _softplus(x):
    return jnp.maximum(x, 0.0) + jnp.log1p(jnp.exp(-jnp.abs(x)))


def _log_sigmoid(x):
    return jnp.minimum(x, 0.0) - jnp.log1p(jnp.exp(-jnp.abs(x)))


def _adaln_kernel(c_ref, w_ref, b_ref, o_ref):
    s = _silu(c_ref[...])
    o_ref[...] = jnp.dot(s, w_ref[...], precision=HIGHEST, preferred_element_type=F32) + b_ref[...]


def _adaln(c, w, b):
    m, d = c.shape
    n = w.shape[1]
    tn = 512
    return pl.pallas_call(
        _adaln_kernel,
        grid=(n // tn,),
        in_specs=[pl.BlockSpec((m, d), lambda j: (0, 0)),
                  pl.BlockSpec((d, tn), lambda j: (0, j)),
                  pl.BlockSpec((1, tn), lambda j: (0, j))],
        out_specs=pl.BlockSpec((m, tn), lambda j: (0, j)),
        out_shape=jax.ShapeDtypeStruct((m, n), F32),
        compiler_params=_params(("parallel",)),
    )(c, w, b.reshape(1, n))


def _inproj_kernel(x_ref, g_ref, sc_ref, sh_ref, wq_ref, wk_ref, wv_ref, wz_ref, wx_ref, ws_ref,
                   q_ref, kf_ref, vf_ref, kb_ref, vb_ref, z_ref, xbc_ref, sm_ref):
    x = x_ref[...]
    nb, tl, d = x.shape
    ms = jnp.mean(x * x, axis=-1, keepdims=True)
    h = x * lax.rsqrt(ms + EPS) * g_ref[...]
    h = h * (1.0 + sc_ref[...]) + sh_ref[...]
    hb = h.reshape(nb * tl, d).astype(BF16)

    def mm(w_ref):
        return jnp.dot(hb, w_ref[...], preferred_element_type=F32)

    q = mm(wq_ref) * (LOG2E / math.sqrt(HEAD_DIM_A))
    q_ref[...] = q.astype(BF16).reshape(nb, tl, -1)
    k = mm(wk_ref)
    kf_ref[...] = k.reshape(nb, tl, -1)
    kb_ref[...] = k.astype(BF16).reshape(nb, tl, -1)
    v = mm(wv_ref)
    vf_ref[...] = v.reshape(nb, tl, -1)
    vb_ref[...] = v.astype(BF16).reshape(nb, tl, -1)
    z_ref[...] = mm(wz_ref).reshape(nb, tl, -1)
    xbc_ref[...] = mm(wx_ref).reshape(nb, tl, -1)
    sm_ref[...] = mm(ws_ref).reshape(nb, tl, -1)


def _inproj(x, g, sc, sh, wq, wk, wv, wz, wx, ws):
    b, l, d = x.shape
    nb, tl = _row_blocking(b, l, ROW_TILE)
    grid = (b // nb, l // tl)
    row = lambda n: pl.BlockSpec((nb, tl, n), lambda i, j: (i, j, 0))
    mod = pl.BlockSpec((nb, 1, d), lambda i, j: (i, 0, 0))
    wspec = lambda w: pl.BlockSpec(w.shape, lambda i, j: (0, 0), pipeline_mode=pl.Buffered(1))
    outs = [(D_ATTN, BF16), (D_ATTN, F32), (D_ATTN, F32), (D_ATTN, BF16), (D_ATTN, BF16),
            (D_INNER, F32), (CONV_DIM, F32), (LANES, F32)]
    return pl.pallas_call(
        _inproj_kernel,
        grid=grid,
        in_specs=[row(d), pl.BlockSpec((1, 1, d), lambda i, j: (0, 0, 0)), mod, mod,
                  wspec(wq), wspec(wk), wspec(wv), wspec(wz), wspec(wx), wspec(ws)],
        out_specs=[row(n) for n, _ in outs],
        out_shape=[jax.ShapeDtypeStruct((b, l, n), dt) for n, dt in outs],
        compiler_params=_params(("parallel", "parallel")),
    )(x, g.reshape(1, 1, d), sc.reshape(b, 1, d), sh.reshape(b, 1, d), wq, wk, wv, wz, wx, ws)


def _running_sum(v_ref, bias_ref, carry_ref, carry_scr, apply_logsig):
    @pl.when(pl.program_id(1) == 0)
    def _():
        carry_scr[...] = carry_ref[0]

    v = v_ref[0]
    tl = v.shape[0]
    lf = _log_sigmoid(v + bias_ref[...]) if apply_logsig else v
    row = lax.broadcasted_iota(jnp.int32, (tl, tl), 0)
    col = lax.broadcasted_iota(jnp.int32, (tl, tl), 1)
    tri = (row >= col).astype(BF16)
    cs = _tri_dot(tri, lf) + carry_scr[...]
    carry_scr[...] = cs[tl - 1:tl, :]
    return lf, cs


def _cumsum_kernel(v_ref, bias_ref, carry_ref, logf_ref, cum_ref, cumt_ref, carry_scr, *, apply_logsig):
    lf, cs = _running_sum(v_ref, bias_ref, carry_ref, carry_scr, apply_logsig)
    logf_ref[0] = lf
    cum_ref[0] = cs
    cumt_ref[0] = cs.T[:N_HEADS_A, :]


def _cumsum(vals, bias, carry, apply_logsig):
    b, l, n = vals.shape
    tl = min(CUM_TILE, l)
    assert l % tl == 0 and tl % LANES == 0
    return pl.pallas_call(
        functools.partial(_cumsum_kernel, apply_logsig=apply_logsig),
        grid=(b, l // tl),
        in_specs=[pl.BlockSpec((1, tl, n), lambda i, j: (i, j, 0)),
                  pl.BlockSpec((1, n), lambda i, j: (0, 0)),
                  pl.BlockSpec((1, 1, n), lambda i, j: (i, 0, 0))],
        out_specs=[pl.BlockSpec((1, tl, n), lambda i, j: (i, j, 0)),
                   pl.BlockSpec((1, tl, n), lambda i, j: (i, j, 0)),
                   pl.BlockSpec((1, N_HEADS_A, tl), lambda i, j: (i, 0, j))],
        out_shape=[jax.ShapeDtypeStruct((b, l, n), F32),
                   jax.ShapeDtypeStruct((b, l, n), F32),
                   jax.ShapeDtypeStruct((b, N_HEADS_A, l), F32)],
        scratch_shapes=[pltpu.VMEM((1, n), F32)],
        compiler_params=_params(("parallel", "arbitrary")),
    )(vals, bias, carry)


N_BIAS_TERMS = 3


def _cumsum_bias_kernel(v_ref, bias_ref, carry_ref, logf_ref, qx_ref, kx_ref, carry_scr):
    lf, cs = _running_sum(v_ref, bias_ref, carry_ref, carry_scr, True)
    logf_ref[0] = lf
    tl = cs.shape[0]
    c2 = cs * LOG2E
    lane = lax.broadcasted_iota(jnp.int32, (tl, LANES), 1)
    ones_q = jnp.where((lane >= N_BIAS_TERMS) & (lane < 2 * N_BIAS_TERMS), 1.0, 0.0)
    ones_k = jnp.where(lane < N_BIAS_TERMS, 1.0, 0.0)
    for h in range(N_HEADS_A):
        terms = [t.astype(F32) for t in _split3(c2[:, h:h + 1])]
        qx, kx = ones_q, ones_k
        for n, t in enumerate(terms):
            qx = jnp.where(lane == n, t, qx)
            kx = jnp.where(lane == N_BIAS_TERMS + n, -t, kx)
        qx_ref[0, :, h * LANES:(h + 1) * LANES] = qx.astype(BF16)
        kx_ref[0, :, h * LANES:(h + 1) * LANES] = kx.astype(BF16)


def _cumsum_bias(vals, bias):
    b, l, n = vals.shape
    tl = min(CUM_TILE, l)
    assert l % tl == 0
    row = lambda w: pl.BlockSpec((1, tl, w), lambda i, j: (i, j, 0))
    return pl.pallas_call(
        _cumsum_bias_kernel,
        grid=(b, l // tl),
        in_specs=[row(n), pl.BlockSpec((1, n), lambda i, j: (0, 0)), pl.BlockSpec((1, 1, n), lambda i, j: (i, 0, 0))],
        out_specs=[row(n), row(D_ATTN), row(D_ATTN)],
        out_shape=[jax.ShapeDtypeStruct((b, l, n), F32), jax.ShapeDtypeStruct((b, l, D_ATTN), BF16),
                   jax.ShapeDtypeStruct((b, l, D_ATTN), BF16)],
        scratch_shapes=[pltpu.VMEM((1, n), F32)],
        compiler_params=_params(("parallel", "arbitrary")),
    )(vals, bias, jnp.zeros((b, 1, n), F32))


_NT = (((1,), (1,)), ((), ()))


def _softmax_step(s, v, m_prev, l_prev, acc_prev):
    m_new = jnp.maximum(m_prev, jnp.max(s, axis=-1, keepdims=True))
    p = jnp.exp2(s - m_new)
    alpha = jnp.exp2(m_prev - m_new)
    l_new = alpha * l_prev + jnp.sum(p, axis=-1, keepdims=True)
    acc_new = alpha * acc_prev + jnp.dot(p.astype(BF16), v, preferred_element_type=F32)
    return m_new, l_new, acc_new


def _attn_kernel(q_ref, qx_ref, k_ref, kx_ref, v_ref, o_ref, m_scr, acc_scr, *, tile, heads):
    i = pl.program_id(2)
    dh = HEAD_DIM_A
    n_chunks = tile // LANES
    ones = jnp.ones((tile, dh), BF16)
    m_scr[...] = jnp.full(m_scr.shape, -jnp.inf, F32)
    acc_scr[...] = jnp.zeros(acc_scr.shape, F32)
    q2 = [jnp.concatenate([q_ref[0, :, hh * dh:(hh + 1) * dh], qx_ref[0, :, hh * dh:(hh + 1) * dh]], axis=1)
          for hh in range(heads)]

    def update(j, mask):
        start = pl.multiple_of(j * tile, tile)
        rows = pl.ds(start, tile)
        for hh in range(heads):
            sl = slice(hh * dh, (hh + 1) * dh)
            k2 = jnp.concatenate([k_ref[0, rows, sl], kx_ref[0, rows, sl]], axis=1)
            v2 = jnp.concatenate([v_ref[0, rows, sl], ones], axis=1)
            s = lax.dot_general(q2[hh], k2, _NT, preferred_element_type=F32)
            if mask is not None:
                s = jnp.where(mask, s, -jnp.inf)
            chunks = [s[:, c * LANES:(c + 1) * LANES] for c in range(n_chunks)]
            cmax = functools.reduce(jnp.maximum, chunks)
            m_prev = m_scr[hh]
            m_new = jnp.maximum(m_prev, jnp.max(cmax, axis=-1, keepdims=True))
            alpha = jnp.exp2(m_prev - m_new)
            p = jnp.concatenate([jnp.exp2(c - m_new) for c in chunks], axis=1).astype(BF16)
            pv = jnp.dot(p, v2, preferred_element_type=F32)
            m_scr[hh] = m_new
            acc_scr[hh, :, :dh] = alpha * acc_scr[hh, :, :dh] + pv[:, :dh]
            acc_scr[hh, :, dh:] = alpha * acc_scr[hh, :, dh:] + pv[:, dh:]

    def body(j, carry):
        update(j, None)
        return carry

    lax.fori_loop(0, i, body, 0)
    row = lax.broadcasted_iota(jnp.int32, (tile, tile), 0)
    col = lax.broadcasted_iota(jnp.int32, (tile, tile), 1)
    update(i, col <= row)
    for hh in range(heads):
        o_ref[0, :, hh * dh:(hh + 1) * dh] = (acc_scr[hh, :, :dh] / acc_scr[hh, :, dh:]).astype(o_ref.dtype)


def _attention(q, qx, k, kx, v):
    b, l, _ = q.shape
    tile = ATT_TILE if l >= 2 * ATT_TILE else LANES
    assert l % tile == 0
    heads = ATT_HEADS_PER_STEP
    w = heads * HEAD_DIM_A
    qspec = pl.BlockSpec((1, tile, w), lambda bi, h, i: (bi, i, h))
    kspec = pl.BlockSpec((1, l, w), lambda bi, h, i: (bi, 0, h))
    return pl.pallas_call(
        functools.partial(_attn_kernel, tile=tile, heads=heads),
        grid=(b, N_HEADS_A // heads, l // tile),
        in_specs=[qspec, qspec, kspec, kspec, kspec],
        out_specs=qspec,
        out_shape=jax.ShapeDtypeStruct((b, l, D_ATTN), BF16),
        scratch_shapes=[pltpu.VMEM((heads, tile, LANES), F32), pltpu.VMEM((heads, tile, 2 * HEAD_DIM_A), F32)],
        compiler_params=_params(("parallel", "parallel", "arbitrary")),
    )(q, qx, k, kx, v)


def _attn_cached_kernel(q_ref, kn_ref, vn_ref, kc_ref, vc_ref, cumn_ref, cumtp_ref, cumtn_ref, o_ref,
                        m_scr, l_scr, acc_scr):
    j = pl.program_id(1)
    ld = q_ref.shape[1]
    dh = HEAD_DIM_A

    @pl.when(j == 0)
    def _():
        m_scr[...] = jnp.full(m_scr.shape, -jnp.inf, F32)
        l_scr[...] = jnp.zeros(l_scr.shape, F32)
        acc_scr[...] = jnp.zeros(acc_scr.shape, F32)

    def head_step(h, kh, vh, ck, mask):
        sl = slice(h * dh, (h + 1) * dh)
        s = lax.dot_general(q_ref[0, :, sl], kh, _NT, preferred_element_type=F32)
        s = s + (cumn_ref[0, :ld, h:h + 1] - ck) * LOG2E
        if mask is not None:
            s = jnp.where(mask, s, -jnp.inf)
        m, l, acc = _softmax_step(s, vh, m_scr[h], l_scr[h], acc_scr[:, sl])
        m_scr[h] = m
        l_scr[h] = l
        acc_scr[:, sl] = acc

    for h in range(N_HEADS_A):
        sl = slice(h * dh, (h + 1) * dh)
        rows = pl.ds(h, kc_ref.shape[1] // N_HEADS_A, stride=N_HEADS_A)
        head_step(h, kc_ref[0, rows, :].astype(BF16), vc_ref[0, rows, :].astype(BF16),
                  cumtp_ref[0, h:h + 1, :], None)

    @pl.when(j == pl.num_programs(1) - 1)
    def _():
        row = lax.broadcasted_iota(jnp.int32, (ld, ld), 0)
        col = lax.broadcasted_iota(jnp.int32, (ld, ld), 1)
        for h in range(N_HEADS_A):
            sl = slice(h * dh, (h + 1) * dh)
            head_step(h, kn_ref[0, :, sl], vn_ref[0, :, sl], cumtn_ref[0, h:h + 1, :ld], col <= row)
            o_ref[0, :, sl] = (acc_scr[:, sl] / l_scr[h]).astype(o_ref.dtype)


def _attention_cached(q, kn, vn, kc, vc, cum_new, cumt_past, cumt_new):
    b, ld, _ = q.shape
    lp = kc.shape[1]
    tk = min(DEC_KV_TILE, lp)
    assert lp % tk == 0
    lpad = cum_new.shape[1]
    new = pl.BlockSpec((1, ld, D_ATTN), lambda bi, j: (bi, 0, 0))
    past = pl.BlockSpec((1, tk * N_HEADS_A, HEAD_DIM_A), lambda bi, j: (bi, j, 0))
    kc = kc.reshape(b, lp * N_HEADS_A, HEAD_DIM_A)
    vc = vc.reshape(b, lp * N_HEADS_A, HEAD_DIM_A)
    return pl.pallas_call(
        _attn_cached_kernel,
        grid=(b, lp // tk),
        in_specs=[new, new, new, past, past,
                  pl.BlockSpec((1, lpad, LANES), lambda bi, j: (bi, 0, 0)),
                  pl.BlockSpec((1, N_HEADS_A, tk), lambda bi, j: (bi, 0, j)),
                  pl.BlockSpec((1, N_HEADS_A, lpad), lambda bi, j: (bi, 0, 0))],
        out_specs=new,
        out_shape=jax.ShapeDtypeStruct((b, ld, D_ATTN), BF16),
        scratch_shapes=[pltpu.VMEM((N_HEADS_A, ld, 1), F32), pltpu.VMEM((N_HEADS_A, ld, 1), F32),
                        pltpu.VMEM((ld, D_ATTN), F32)],
        compiler_params=_params(("parallel", "arbitrary")),
    )(q, kn, vn, kc, vc, cum_new, cumt_past, cumt_new)


def _expand_heads(a):
    r = a.shape[0]
    low = lax.broadcasted_iota(jnp.int32, (r, LANES), 1) < HEAD_DIM_S
    return jnp.concatenate(
        [jnp.where(low, a[:, 2 * j:2 * j + 1], a[:, 2 * j + 1:2 * j + 2]) for j in range(N_HEADS_S // 2)], axis=1)


def _ssd_kernel(xbc_ref, z_ref, dt_ref, dtt_ref, past_ref, h0_ref, cw_ref, cb_ref, dtb_ref, dtbt_ref,
                alog_ref, alogt_ref, dsk_ref, nw_ref, o_ref, hout_ref, xbuf, ht_scr):
    c = pl.program_id(1)
    lc = xbc_ref.shape[1]
    hist = SUBLANES

    @pl.when(c == 0)
    def _():
        xbuf[0:hist, :] = past_ref[0]
        ht_scr[...] = h0_ref[0]

    xbuf[hist:hist + lc, :] = xbc_ref[0]
    u = cb_ref[...]
    for w in range(CONV_W):
        off = hist - (CONV_W - 1) + w
        u = u + xbuf[off:off + lc, :] * cw_ref[w:w + 1, :]
    xbuf[0:hist, :] = xbuf[lc:lc + hist, :]
    u = _silu(u)
    xs = u[:, :D_INNER]
    gn = N_GROUPS_S * D_STATE
    bm = u[:, D_INNER:D_INNER + gn].astype(BF16)
    cm = u[:, D_INNER + gn:].astype(BF16)

    dt = _softplus(dt_ref[0] + dtb_ref[...])
    dtt = _softplus(dtt_ref[0] + dtbt_ref[...])
    a = dt * (-jnp.exp(alog_ref[...]))
    at = dtt * (-jnp.exp(alogt_ref[...]))
    row = lax.broadcasted_iota(jnp.int32, (lc, lc), 0)
    col = lax.broadcasted_iota(jnp.int32, (lc, lc), 1)
    causal = col <= row
    a_cs = _tri_dot(causal.astype(BF16), a)
    a_cst = _dot_tri(at, (row <= col).astype(BF16))
    total = a_cs[lc - 1:lc, :]
    dt_e = _expand_heads(dt)
    w_e = _expand_heads(dt * jnp.exp(total - a_cs))
    ea_e = _expand_heads(jnp.exp(a_cs))
    cd_e = _expand_heads(jnp.exp(total))
    xdt = xs * dt_e
    xdw = (xs * w_e).astype(BF16)

    cbs = [lax.dot_general(cm[:, g * D_STATE:(g + 1) * D_STATE], bm[:, g * D_STATE:(g + 1) * D_STATE], _NT,
                           preferred_element_type=F32) for g in range(N_GROUPS_S)]
    low = lax.broadcasted_iota(jnp.int32, (lc, LANES), 1) < HEAD_DIM_S
    heads_per_group = N_HEADS_S // N_GROUPS_S
    yd = []
    for j in range(N_HEADS_S // 2):
        ms = []
        for hh in (2 * j, 2 * j + 1):
            seg = a_cs[:, hh:hh + 1] - a_cst[hh:hh + 1, :]
            dec = jnp.where(causal, jnp.exp(jnp.where(causal, seg, 0.0)), 0.0)
            ms.append((cbs[hh // heads_per_group] * dec).astype(BF16))
        xb = xdt[:, j * LANES:(j + 1) * LANES]
        rhs = jnp.concatenate([jnp.where(low, xb, 0.0), jnp.where(low, 0.0, xb)], axis=0).astype(BF16)
        yd.append(jnp.dot(jnp.concatenate(ms, axis=1), rhs, preferred_element_type=F32))
    y = jnp.concatenate(yd, axis=1)

    half = D_INNER // N_GROUPS_S
    ht = ht_scr[...]
    htb = ht.astype(BF16)
    y_off = jnp.concatenate(
        [jnp.dot(cm[:, g * D_STATE:(g + 1) * D_STATE], htb[:, g * half:(g + 1) * half],
                 preferred_element_type=F32) for g in range(N_GROUPS_S)], axis=1)
    st = jnp.concatenate(
        [lax.dot_general(bm[:, g * D_STATE:(g + 1) * D_STATE], xdw[:, g * half:(g + 1) * half],
                         (((0,), (0,)), ((), ())), preferred_element_type=F32) for g in range(N_GROUPS_S)], axis=1)
    ht_new = cd_e * ht + st
    ht_scr[...] = ht_new
    hout_ref[0] = ht_new

    y = y + y_off * ea_e + dsk_ref[...] * xs
    gz = y * _silu(z_ref[0])
    outs = []
    for g in range(N_GROUPS_S):
        gg = gz[:, g * half:(g + 1) * half]
        outs.append(gg * lax.rsqrt(jnp.mean(gg * gg, axis=-1, keepdims=True) + EPS))
    o_ref[0] = (jnp.concatenate(outs, axis=1) * nw_ref[...]).astype(o_ref.dtype)


def _ssd(xbc, z, dt_raw, conv_past, h0t, conv_w, conv_b, dt_bias, a_log, d_skip, norm_w):
    b, l, _ = xbc.shape
    lc = min(SSD_CHUNK, l)
    assert l % lc == 0 and lc % SUBLANES == 0 and lc >= SUBLANES
    nh = N_HEADS_S
    dtt = jnp.swapaxes(dt_raw, 1, 2)
    past = jnp.pad(conv_past, ((0, 0), (SUBLANES - (CONV_W - 1), 0), (0, 0)))
    const = lambda shape: pl.BlockSpec(shape, lambda i, j: tuple(0 for _ in shape))
    o, hout = pl.pallas_call(
        _ssd_kernel,
        grid=(b, l // lc),
        in_specs=[pl.BlockSpec((1, lc, CONV_DIM), lambda i, j: (i, j, 0)),
                  pl.BlockSpec((1, lc, D_INNER), lambda i, j: (i, j, 0)),
                  pl.BlockSpec((1, lc, nh), lambda i, j: (i, j, 0)),
                  pl.BlockSpec((1, nh, lc), lambda i, j: (i, 0, j)),
                  pl.BlockSpec((1, SUBLANES, CONV_DIM), lambda i, j: (i, 0, 0)),
                  pl.BlockSpec((1, D_STATE, D_INNER), lambda i, j: (i, 0, 0)),
                  const((CONV_W, CONV_DIM)), const((1, CONV_DIM)),
                  const((1, nh)), const((nh, 1)), const((1, nh)), const((nh, 1)),
                  const((1, D_INNER)), const((1, D_INNER))],
        out_specs=[pl.BlockSpec((1, lc, D_INNER), lambda i, j: (i, j, 0)),
                   pl.BlockSpec((1, D_STATE, D_INNER), lambda i, j: (i, 0, 0))],
        out_shape=[jax.ShapeDtypeStruct((b, l, D_INNER), BF16),
                   jax.ShapeDtypeStruct((b, D_STATE, D_INNER), F32)],
        scratch_shapes=[pltpu.VMEM((lc + SUBLANES, CONV_DIM), F32), pltpu.VMEM((D_STATE, D_INNER), F32)],
        compiler_params=_params(("parallel", "arbitrary")),
    )(xbc, z, dt_raw, dtt, past, h0t, conv_w, conv_b.reshape(1, -1),
      dt_bias.reshape(1, nh), dt_bias.reshape(nh, 1), a_log.reshape(1, nh), a_log.reshape(nh, 1),
      jnp.repeat(d_skip, HEAD_DIM_S).reshape(1, -1), norm_w.reshape(1, -1))
    return o, hout


def _outproj_kernel(x_ref, oa_ref, os_ref, gt_ref, sc_ref, sh_ref, g_ref, wa_ref, ws_ref, rw_ref, rb_ref,
                    x1_ref, h2_ref, lg_ref):
    nb, tl, d = x_ref.shape
    oa = oa_ref[...].reshape(nb * tl, -1)
    os_ = os_ref[...].reshape(nb * tl, -1)
    mix = jnp.dot(oa, wa_ref[...], preferred_element_type=F32) + jnp.dot(os_, ws_ref[...], preferred_element_type=F32)
    x1 = x_ref[...] + gt_ref[...] * mix.reshape(nb, tl, d)
    x1_ref[...] = x1
    ms = jnp.mean(x1 * x1, axis=-1, keepdims=True)
    h2 = x1 * lax.rsqrt(ms + EPS) * g_ref[...]
    h2 = h2 * (1.0 + sc_ref[...]) + sh_ref[...]
    h2 = h2.reshape(nb * tl, d)
    _store_row_tiles(h2_ref, h2)
    h_hi, h_lo = _split2(h2)
    w_hi, w_lo = rw_ref[0], rw_ref[1]
    lg = (jnp.dot(h_hi, w_hi, preferred_element_type=F32) + jnp.dot(h_lo, w_hi, preferred_element_type=F32)
          + jnp.dot(h_hi, w_lo, preferred_element_type=F32)) + rb_ref[...]
    lg_ref[...] = lg.reshape(nb, tl, -1)


def _outproj(x, oa, os_, gt, sc, sh, g, wa, ws, rw, rb):
    b, l, d = x.shape
    nb, tl = _row_blocking(b, l, ROW_TILE)
    row = lambda n: pl.BlockSpec((nb, tl, n), lambda i, j: (i, j, 0))
    mod = pl.BlockSpec((nb, 1, d), lambda i, j: (i, 0, 0))
    const = lambda a: pl.BlockSpec(a.shape, lambda i, j: tuple(0 for _ in a.shape))
    g3 = g.reshape(1, 1, d)
    return pl.pallas_call(
        _outproj_kernel,
        grid=(b // nb, l // tl),
        in_specs=[row(d), row(D_ATTN), row(D_INNER), mod, mod, mod, const(g3), const(wa), const(ws),
                  const(rw), const(rb)],
        out_specs=[row(d), pl.BlockSpec((nb * tl * ROW_TILE_SUBLANES, LANES), lambda i, j: (i * (l // tl) + j, 0)),
                   row(LANES)],
        out_shape=[jax.ShapeDtypeStruct((b, l, d), F32), jax.ShapeDtypeStruct((b * l * ROW_TILE_SUBLANES, LANES), F32),
                   jax.ShapeDtypeStruct((b, l, LANES), F32)],
        compiler_params=_params(("parallel", "parallel")),
    )(x, oa, os_, gt.reshape(b, 1, d), sc.reshape(b, 1, d), sh.reshape(b, 1, d), g3, wa, ws, rw, rb)


def _route_kernel(lg_ref, idx_ref, gate_ref, cnt_ref, carry_scr):
    @pl.when(pl.program_id(0) == 0)
    def _():
        carry_scr[...] = jnp.zeros(carry_scr.shape, F32)

    v = lg_ref[...]
    tr = v.shape[0]
    lane = lax.broadcasted_iota(jnp.int32, (tr, LANES), 1)
    lane_f = lane.astype(F32)
    tops, idxs = [], []
    onehot = jnp.zeros((tr, LANES), F32)
    for _ in range(TOP_K):
        m = jnp.max(v, axis=-1, keepdims=True)
        idx = jnp.min(jnp.where(v == m, lane_f, float(LANES)), axis=-1, keepdims=True)
        hit = lane_f == idx
        v = jnp.where(hit, -jnp.inf, v)
        onehot = onehot + hit.astype(F32)
        tops.append(m)
        idxs.append(idx)
    es = [jnp.exp(t - tops[0]) for t in tops]
    denom = es[0] + es[1] + es[2] + es[3]

    row = lax.broadcasted_iota(jnp.int32, (tr, tr), 0)
    col = lax.broadcasted_iota(jnp.int32, (tr, tr), 1)
    before = jnp.dot((col < row).astype(BF16), onehot.astype(BF16), preferred_element_type=F32) + carry_scr[...]

    idx_out = jnp.zeros((tr, LANES), F32)
    gate_out = jnp.zeros((tr, LANES), F32)
    for k in range(TOP_K):
        rank = jnp.sum(jnp.where(lane_f == idxs[k], before, 0.0), axis=-1, keepdims=True)
        idx_out = jnp.where(lane == k, idxs[k], idx_out)
        idx_out = jnp.where(lane == TOP_K + k, rank, idx_out)
        gate_out = jnp.where(lane == k, es[k] / denom, gate_out)
    idx_ref[...] = idx_out.astype(jnp.int32)
    gate_ref[...] = gate_out
    carry_scr[...] = carry_scr[...] + jnp.sum(onehot, axis=0, keepdims=True)
    cnt_ref[...] = carry_scr[...]


def _route(logits):
    t, n = logits.shape
    tr = min(ROUTE_TILE, t)
    assert t % tr == 0
    return pl.pallas_call(
        _route_kernel,
        grid=(t // tr,),
        in_specs=[pl.BlockSpec((tr, n), lambda i: (i, 0))],
        out_specs=[pl.BlockSpec((tr, n), lambda i: (i, 0)), pl.BlockSpec((tr, n), lambda i: (i, 0)),
                   pl.BlockSpec((1, n), lambda i: (0, 0))],
        out_shape=[jax.ShapeDtypeStruct((t, n), jnp.int32), jax.ShapeDtypeStruct((t, n), F32),
                   jax.ShapeDtypeStruct((1, n), F32)],
        scratch_shapes=[pltpu.VMEM((1, n), F32)],
        compiler_params=_params(("arbitrary",)),
    )(logits)


def _row_tile(ref, r):
    start = r * ROW_TILE_SUBLANES
    if not isinstance(r, int):
        start = pl.multiple_of(start, ROW_TILE_SUBLANES)
    return ref.at[pl.ds(start, ROW_TILE_SUBLANES), :]


def _store_row_tiles(ref, x, tile0=0):
    rows = x.shape[0]
    for s in range(x.shape[1] // LANES):
        ref[pl.ds(tile0 + s, rows, stride=ROW_TILE_SUBLANES), :] = x[:, s * LANES:(s + 1) * LANES]


def _load_row_tiles(ref, row0, rows):
    return jnp.concatenate(
        [ref[pl.ds(row0 * ROW_TILE_SUBLANES + s, rows, stride=ROW_TILE_SUBLANES), :] for s in range(ROW_TILE_SUBLANES)],
        axis=1)


def _gather_rows(idx_ref, src_hbm, dst, sem, n_rows):
    group = 8
    assert n_rows % group == 0

    def body(g, carry):
        for u in range(group):
            r = g * group + u
            pltpu.make_async_copy(_row_tile(src_hbm, idx_ref[0, 0, r]), _row_tile(dst, r), sem).start(priority=u % 2)
        return carry
    lax.fori_loop(0, n_rows // group, body, 0)


def _wait_rows(src_hbm, dst, sem, n_rows):
    pltpu.make_async_copy(src_hbm.at[pl.ds(0, n_rows * ROW_TILE_SUBLANES), :], dst, sem).wait()


EXPERT_COL_CHUNK = 256


GATHER_BUFS = 3


def _dispatch_kernel(ends_ref, pad_ref, dest_ref, h_hbm, xs_hbm, zbuf, zsem, sem, *, n_slots):
    i = pl.program_id(0)
    n = pl.num_programs(0)
    m = zbuf.shape[0] // ROW_TILE_SUBLANES
    rows = dest_ref.shape[2]
    td = rows // TOP_K
    slot = lax.rem(i, 2)

    def zero_block(first_slot):
        start = pl.multiple_of(first_slot * ROW_TILE_SUBLANES, ROW_TILE_SUBLANES)
        return pltpu.make_async_copy(zbuf, xs_hbm.at[pl.ds(start, zbuf.shape[0]), :], zsem)

    @pl.when(i == 0)
    def _():
        zbuf[...] = jnp.zeros(zbuf.shape, zbuf.dtype)
        jobs = [(pad_ref[e] > 0, ends_ref[e] - m) for e in range(N_EXPERTS)]
        jobs += [(ends_ref[N_EXPERTS - 1] + j * m < n_slots, ends_ref[N_EXPERTS - 1] + j * m)
                 for j in range(N_EXPERTS + 1)]
        for cond, first in jobs:
            @pl.when(cond)
            def _():
                zero_block(first).start()
        for cond, first in jobs:
            @pl.when(cond)
            def _():
                zero_block(first).wait()

    base = i * td
    batch = 16
    for j0 in range(0, rows, batch):
        dsts = [dest_ref[0, 0, j0 + u] for u in range(batch)]
        for u, dst in enumerate(dsts):
            j = j0 + u
            pltpu.make_async_copy(_row_tile(h_hbm, base + j // TOP_K), _row_tile(xs_hbm, dst),
                                  sem.at[slot]).start(priority=j % 2)

    def wait_step(s):
        pltpu.make_async_copy(h_hbm.at[pl.ds(0, rows * ROW_TILE_SUBLANES), :],
                              xs_hbm.at[pl.ds(0, rows * ROW_TILE_SUBLANES), :], sem.at[s]).wait()

    @pl.when(i > 0)
    def _():
        wait_step(1 - slot)

    @pl.when(i == n - 1)
    def _():
        wait_step(slot)


def _dispatch(dest, ends_p, padded, h2, m, n_slots):
    t = dest.shape[0]
    td = min(DISPATCH_TILE, t)
    assert t % td == 0
    n_steps = t // td
    grid_spec = pltpu.PrefetchScalarGridSpec(
        num_scalar_prefetch=2,
        grid=(n_steps,),
        in_specs=[pl.BlockSpec((1, 1, TOP_K * td), lambda i, e, p: (i, 0, 0), memory_space=pltpu.SMEM),
                  pl.BlockSpec(memory_space=pl.ANY)],
        out_specs=pl.BlockSpec(memory_space=pl.ANY),
        scratch_shapes=[pltpu.VMEM((m * ROW_TILE_SUBLANES, LANES), F32), pltpu.SemaphoreType.DMA(()),
                        pltpu.SemaphoreType.DMA((2,))],
    )
    return pl.pallas_call(
        functools.partial(_dispatch_kernel, n_slots=n_slots),
        grid_spec=grid_spec,
        out_shape=jax.ShapeDtypeStruct((n_slots * ROW_TILE_SUBLANES, LANES), F32),
        compiler_params=_params(("arbitrary",)),
    )(ends_p, padded, dest.reshape(n_steps, 1, TOP_K * td), h2)


def _experts_kernel(be_ref, x_ref, wgu_ref, bgu_ref, wdn_ref, bdn_ref, o_ref):
    m = x_ref.shape[0] // ROW_TILE_SUBLANES
    cw = EXPERT_COL_CHUNK
    tiles_per_piece = cw // LANES
    n_ff, n_out = D_FF // cw, ROW_TILE_SUBLANES // tiles_per_piece
    x = _load_row_tiles(x_ref, 0, m).astype(BF16)
    acts = []
    for c in range(n_ff):
        g = jnp.dot(x, wgu_ref[0, :, c * cw:(c + 1) * cw], preferred_element_type=F32) + bgu_ref[0, :, c * cw:(c + 1) * cw]
        u = (jnp.dot(x, wgu_ref[0, :, D_FF + c * cw:D_FF + (c + 1) * cw], preferred_element_type=F32)
             + bgu_ref[0, :, D_FF + c * cw:D_FF + (c + 1) * cw])
        gate = jnp.minimum(g, SWIGLU_LIMIT)
        up = jnp.clip(u, -SWIGLU_LIMIT, SWIGLU_LIMIT)
        acts.append(((up + 1.0) * gate * jax.nn.sigmoid(SWIGLU_ALPHA * gate)).astype(BF16))
    act = jnp.concatenate(acts, axis=1)
    for c in range(n_out):
        y = jnp.dot(act, wdn_ref[0, :, c * cw:(c + 1) * cw], preferred_element_type=F32) + bdn_ref[0, :, c * cw:(c + 1) * cw]
        _store_row_tiles(o_ref, y, tile0=c * tiles_per_piece)


def _experts(block_e, xs, wgu, bgu, wdn, bdn, m):
    n_blocks = block_e.shape[0]
    d = D_MODEL
    mt = m * ROW_TILE_SUBLANES
    grid_spec = pltpu.PrefetchScalarGridSpec(
        num_scalar_prefetch=1,
        grid=(n_blocks,),
        in_specs=[pl.BlockSpec((mt, LANES), lambda i, be: (i, 0)),
                  pl.BlockSpec((1, d, 2 * D_FF), lambda i, be: (be[i], 0, 0)),
                  pl.BlockSpec((1, 1, 2 * D_FF), lambda i, be: (be[i], 0, 0)),
                  pl.BlockSpec((1, D_FF, d), lambda i, be: (be[i], 0, 0)),
                  pl.BlockSpec((1, 1, d), lambda i, be: (be[i], 0, 0))],
        out_specs=pl.BlockSpec((mt, LANES), lambda i, be: (i, 0)),
    )
    return pl.pallas_call(
        _experts_kernel,
        grid_spec=grid_spec,
        out_shape=jax.ShapeDtypeStruct((n_blocks * mt, LANES), F32),
        compiler_params=_params(("arbitrary",)),
    )(block_e, xs, wgu, bgu.reshape(N_EXPERTS, 1, -1), wdn, bdn.reshape(N_EXPERTS, 1, -1))


def _combine_kernel(idx0_ref, idx1_ref, idxn_ref, y_hbm, x1_ref, gate_ref, gt_ref, g_ref, o_ref, ybuf, sem):
    i = pl.program_id(0)
    n = pl.num_programs(0)
    nb, tl, d = x1_ref.shape
    tc = nb * tl
    slot = lax.rem(i, GATHER_BUFS)
    nxt = lax.rem(i + 2, GATHER_BUFS)

    @pl.when(i == 0)
    def _():
        _gather_rows(idx0_ref, y_hbm, ybuf.at[0], sem.at[0], TOP_K * tc)
        _gather_rows(idx1_ref, y_hbm, ybuf.at[1], sem.at[1], TOP_K * tc)

    _wait_rows(y_hbm, ybuf.at[slot], sem.at[slot], TOP_K * tc)

    def prefetch(k):
        for r in range(k * tc, (k + 1) * tc):
            pltpu.make_async_copy(_row_tile(y_hbm, idxn_ref[0, 0, r]), _row_tile(ybuf.at[nxt], r),
                                  sem.at[nxt]).start(priority=r % 2)

    gates = gate_ref[...]
    y = gates[:, 0:1] * _load_row_tiles(ybuf.at[slot], 0, tc)
    prefetch(0)
    for k in range(1, TOP_K):
        y = y + gates[:, k:k + 1] * _load_row_tiles(ybuf.at[slot], k * tc, tc)
        prefetch(k)
    x2 = x1_ref[...] + gt_ref[...] * y.reshape(nb, tl, d)
    ms = jnp.mean(x2 * x2, axis=-1, keepdims=True)
    o_ref[...] = x2 * lax.rsqrt(ms + EPS) * g_ref[...]

    @pl.when(i == n - 1)
    def _():
        for ahead in (1, 2):
            s = lax.rem(i + ahead, GATHER_BUFS)
            _wait_rows(y_hbm, ybuf.at[s], sem.at[s], TOP_K * tc)


def _combine(dest, y_slots, x1, gates, gt, g_final):
    b, l, d = x1.shape
    nb, tl = _row_blocking(b, l, COMBINE_TILE)
    tc = nb * tl
    t = b * l
    n_steps = t // tc
    idx = jnp.swapaxes(dest.reshape(n_steps, tc, TOP_K), 1, 2).reshape(n_steps, 1, TOP_K * tc)
    steps_per_batch_row = l // tl
    tok = lambda i: (i // steps_per_batch_row, i % steps_per_batch_row, 0)
    smem = lambda f: pl.BlockSpec((1, 1, TOP_K * tc), f, memory_space=pltpu.SMEM)
    return pl.pallas_call(
        _combine_kernel,
        grid=(n_steps,),
        in_specs=[smem(lambda i: (0, 0, 0)),
                  smem(lambda i: (min(1, n_steps - 1), 0, 0)),
                  smem(lambda i: (jnp.minimum(i + 2, n_steps - 1), 0, 0)),
                  pl.BlockSpec(memory_space=pl.ANY),
                  pl.BlockSpec((nb, tl, d), tok),
                  pl.BlockSpec((tc, LANES), lambda i: (i, 0)),
                  pl.BlockSpec((nb, 1, d), lambda i: (i // steps_per_batch_row, 0, 0)),
                  pl.BlockSpec((1, 1, d), lambda i: (0, 0, 0))],
        out_specs=pl.BlockSpec((nb, tl, d), tok),
        out_shape=jax.ShapeDtypeStruct((b, l, d), F32),
        scratch_shapes=[pltpu.VMEM((GATHER_BUFS, TOP_K * tc * ROW_TILE_SUBLANES, LANES), F32),
                        pltpu.SemaphoreType.DMA((GATHER_BUFS,))],
        compiler_params=_params(("arbitrary",)),
    )(idx, idx, idx, y_slots, x1, gates, gt.reshape(b, 1, d), g_final.reshape(1, 1, d))


def _moe(h2, logits, x1, gt_m, g_final, wts):
    b, l, d = x1.shape
    t = b * l
    m = MOE_ROWS if t * TOP_K >= 4 * N_EXPERTS * MOE_ROWS else MOE_ROWS_SMALL
    idx, gates, counts = _route(logits.reshape(t, LANES))
    top_e = idx[:, :TOP_K]
    rank = idx[:, TOP_K:2 * TOP_K]
    counts = counts[0, :N_EXPERTS].astype(jnp.int32)
    padded = (counts + m - 1) // m * m
    ends_p = jnp.cumsum(padded)
    starts_p = ends_p - padded
    dest = starts_p[top_e] + rank
    n_blocks = (t * TOP_K + N_EXPERTS * (m - 1) + m - 1) // m
    block_start = jnp.arange(n_blocks, dtype=jnp.int32)[:, None] * m
    block_e = jnp.minimum(jnp.sum((ends_p[None, :] <= block_start).astype(jnp.int32), axis=1), N_EXPERTS - 1)
    xs = _dispatch(dest, ends_p, padded, h2, m, n_blocks * m)
    y_slots = _experts(block_e, xs, *wts, m)
    return _combine(dest, y_slots, x1, gates, gt_m, g_final)


def _layer(x, mod, k_past, v_past, logf_past, conv_past, ssm_past, p, g_final):
    b, l, d = x.shape
    sh_a, sc_a, gt_a, sh_m, sc_m, gt_m = jnp.split(mod, 6, axis=-1)
    q, kf, vf, kb, vb, z, xbc, sm = _inproj(x, p['g_mix'], sc_a, sh_a, *p['w_in'])
    dt_raw = sm[:, :, N_HEADS_A:N_HEADS_A + N_HEADS_S]

    zero_carry = jnp.zeros((b, 1, LANES), F32)
    if k_past is None:
        logf, qx, kx = _cumsum_bias(sm, p['b_f'])
        o_a = _attention(q, qx, kb, kx, vb)
        conv_past = jnp.zeros((b, CONV_W - 1, CONV_DIM), F32)
        h0t = jnp.zeros((b, D_STATE, D_INNER), F32)
    else:
        lp = k_past.shape[1]
        past = jnp.pad(logf_past, ((0, 0), (0, 0), (0, LANES - N_HEADS_A)))
        _, cum_p, cumt_p = _cumsum(past, p['b_f'], zero_carry, False)
        lpad = -(-l // LANES) * LANES
        sm_pad = jnp.pad(sm, ((0, 0), (0, lpad - l), (0, 0)))
        logf, cum_n, cumt_n = _cumsum(sm_pad, p['b_f'], cum_p[:, lp - 1:lp, :], True)
        logf = logf[:, :l]
        o_a = _attention_cached(q, kb, vb, k_past, v_past, cum_n, cumt_p, cumt_n)
        h0t = jnp.swapaxes(ssm_past.reshape(b, D_INNER, D_STATE), 1, 2)
    o_s, hout = _ssd(xbc, z, dt_raw, conv_past, h0t, p['conv_w'], p['conv_b'], p['dt_bias'], p['a_log'],
                     p['d_skip'], p['ssd_norm_w'])
    x1, h2, logits = _outproj(x, o_a, o_s, gt_a, sc_m, sh_m, p['g_ffn'], p['w_out_a'], p['w_out_s'],
                              p['router_w'], p['router_b'])
    y = _moe(h2, logits, x1, gt_m, g_final, p['experts'])

    assert l >= CONV_W - 1
    conv_new = xbc[:, l - (CONV_W - 1):]
    ssm_new = jnp.swapaxes(hout, 1, 2).reshape(b, N_HEADS_S, HEAD_DIM_S, D_STATE)
    return (y, kf.reshape(b, l, N_HEADS_A, HEAD_DIM_A), vf.reshape(b, l, N_HEADS_A, HEAD_DIM_A),
            logf[:, :, :N_HEADS_A], conv_new, ssm_new)


def kernel(x_prompt, x_sample, cache_k, cache_v, cache_logf, state_conv, state_ssm, c_prompt, c_sample, w_ada, b_ada, g_mix, w_in, b_f, conv_w, conv_b, dt_bias, a_log, d_skip, ssd_norm_w, w_out, g_ffn, router_w, router_b, w_gate_up, b_gate_up, w_down, b_down, g_final):
    assert w_ada.shape[0] == 1, "single-layer operation"
    bp = x_prompt.shape[0]
    w = w_in[0]
    q_end, k_end, v_end = D_ATTN, 2 * D_ATTN, 3 * D_ATTN
    f_end = v_end + N_HEADS_A
    z_end = f_end + D_INNER
    xbc_end = z_end + CONV_DIM
    w_small = jnp.concatenate(
        [w[:, v_end:f_end], w[:, xbc_end:], jnp.zeros((D_MODEL, LANES - N_HEADS_A - N_HEADS_S), F32)], axis=1)
    cast = lambda a: a.astype(BF16)
    p = {
        'g_mix': g_mix[0],
        'w_in': (cast(w[:, :q_end]), cast(w[:, q_end:k_end]), cast(w[:, k_end:v_end]), cast(w[:, f_end:z_end]),
                 cast(w[:, z_end:xbc_end]), cast(w_small)),
        'b_f': jnp.pad(b_f[0], (0, LANES - N_HEADS_A)).reshape(1, LANES),
        'conv_w': conv_w[0], 'conv_b': conv_b[0], 'dt_bias': dt_bias[0], 'a_log': a_log[0],
        'd_skip': d_skip[0], 'ssd_norm_w': ssd_norm_w[0],
        'w_out_a': cast(w_out[0, :D_ATTN]), 'w_out_s': cast(w_out[0, D_ATTN:]),
        'g_ffn': g_ffn[0],
        'router_w': jnp.stack(_split2(jnp.pad(router_w[0], ((0, 0), (0, LANES - N_EXPERTS))))),
        'router_b': jnp.pad(router_b[0], (0, LANES - N_EXPERTS), constant_values=-jnp.inf).reshape(1, LANES),
        'experts': (cast(w_gate_up[0]), b_gate_up[0], cast(w_down[0]), b_down[0]),
    }
    mod = _adaln(jnp.concatenate([c_prompt, c_sample], axis=0), w_ada[0], b_ada[0])
    outs_p = _layer(x_prompt, mod[:bp], None, None, None, None, None, p, g_final)
    outs_s = _layer(x_sample, mod[bp:], cache_k[0], cache_v[0], cache_logf[0], state_conv[0], state_ssm[0], p, g_final)
    stack = lambda a: a[None]
    return (outs_p[0], outs_s[0]) + tuple(stack(a) for a in outs_p[1:]) + tuple(stack(a) for a in outs_s[1:])
```

```python
import functools
import math

import jax
import jax.numpy as jnp
from jax import lax
from jax.experimental import pallas as pl
from jax.experimental.pallas import tpu as pltpu

F32 = jnp.float32
BF16 = jnp.bfloat16
HIGHEST = lax.Precision.HIGHEST

D_MODEL = 1024
N_HEADS_A = 8
HEAD_DIM_A = 128
D_ATTN = N_HEADS_A * HEAD_DIM_A
D_INNER = 1024
HEAD_DIM_S = 64
N_HEADS_S = D_INNER // HEAD_DIM_S
N_GROUPS_S = 2
D_STATE = 128
CONV_W = 4
CONV_DIM = D_INNER + 2 * N_GROUPS_S * D_STATE
N_EXPERTS = 32
TOP_K = 4
D_FF = 1024
SWIGLU_LIMIT = 7.0
SWIGLU_ALPHA = 1.702
EPS = 1e-5
LANES = 128
SUBLANES = 8
ROW_TILE_SUBLANES = D_MODEL // LANES
assert ROW_TILE_SUBLANES == SUBLANES
VMEM_LIMIT = 56 * 1024 * 1024
LOG2E = 1.4426950408889634
ATT_HEADS_PER_STEP = 4

ROW_TILE = 512
ATT_TILE = 512
DEC_KV_TILE = 512
CUM_TILE = 256
SSD_CHUNK = 128
ROUTE_TILE = 256
MOE_ROWS = 512
MOE_ROWS_SMALL = 128
COMBINE_TILE = 256
DISPATCH_TILE = 256


def _params(sem, vmem=VMEM_LIMIT):
    return pltpu.CompilerParams(dimension_semantics=sem, vmem_limit_bytes=vmem)


def _row_blocking(b, l, tile):
    if l >= tile:
        assert l % tile == 0
        return 1, tile
    nb = min(b, max(1, tile // l))
    while b % nb:
        nb -= 1
    return nb, l


def _split3(x):
    hi = x.astype(BF16)
    r1 = x - hi.astype(F32)
    mid = r1.astype(BF16)
    lo = (r1 - mid.astype(F32)).astype(BF16)
    return hi, mid, lo


def _split2(x):
    hi = x.astype(BF16)
    return hi, (x - hi.astype(F32)).astype(BF16)


def _tri_dot(tri, x):
    hi, mid, lo = _split3(x)
    d = lambda p: jnp.dot(tri, p, preferred_element_type=F32)
    return d(hi) + d(mid) + d(lo)


def _dot_tri(x, tri):
    hi, mid, lo = _split3(x)
    d = lambda p: jnp.dot(p, tri, preferred_element_type=F32)
    return d(hi) + d(mid) + d(lo)


def _silu(x):
    return x * jax.nn.sigmoid(x)


def _softplus(x):
    return jnp.maximum(x, 0.0) + jnp.log1p(jnp.exp(-jnp.abs(x)))


def _log_sigmoid(x):
    return jnp.minimum(x, 0.0) - jnp.log1p(jnp.exp(-jnp.abs(x)))


def _adaln_kernel(c_ref, w_ref, b_ref, o_ref):
    s = _silu(c_ref[...])
    o_ref[...] = jnp.dot(s, w_ref[...], precision=HIGHEST, preferred_element_type=F32) + b_ref[...]


def _adaln(c, w, b):
    m, d = c.shape
    n = w.shape[1]
    tn = 512
    return pl.pallas_call(
        _adaln_kernel,
        grid=(n // tn,),
        in_specs=[pl.BlockSpec((m, d), lambda j: (0, 0)),
                  pl.BlockSpec((d, tn), lambda j: (0, j)),
                  pl.BlockSpec((1, tn), lambda j: (0, j))],
        out_specs=pl.BlockSpec((m, tn), lambda j: (0, j)),
        out_shape=jax.ShapeDtypeStruct((m, n), F32),
        compiler_params=_params(("parallel",)),
    )(c, w, b.reshape(1, n))


def _inproj_kernel(x_ref, g_ref, sc_ref, sh_ref, wq_ref, wk_ref, wv_ref, wz_ref, wx_ref, ws_ref,
                   q_ref, kf_ref, vf_ref, kb_ref, vb_ref, z_ref, xbc_ref, sm_ref):
    x = x_ref[...]
    nb, tl, d = x.shape
    ms = jnp.mean(x * x, axis=-1, keepdims=True)
    h = x * lax.rsqrt(ms + EPS) * g_ref[...]
    h = h * (1.0 + sc_ref[...]) + sh_ref[...]
    hb = h.reshape(nb * tl, d).astype(BF16)

    def mm(w_ref):
        return jnp.dot(hb, w_ref[...], preferred_element_type=F32)

    q = mm(wq_ref) * (LOG2E / math.sqrt(HEAD_DIM_A))
    q_ref[...] = q.astype(BF16).reshape(nb, tl, -1)
    k = mm(wk_ref)
    kf_ref[...] = k.reshape(nb, tl, -1)
    kb_ref[...] = k.astype(BF16).reshape(nb, tl, -1)
    v = mm(wv_ref)
    vf_ref[...] = v.reshape(nb, tl, -1)
    vb_ref[...] = v.astype(BF16).reshape(nb, tl, -1)
    z_ref[...] = mm(wz_ref).reshape(nb, tl, -1)
    xbc_ref[...] = mm(wx_ref).reshape(nb, tl, -1)
    sm_ref[...] = mm(ws_ref).reshape(nb, tl, -1)


def _inproj(x, g, sc, sh, wq, wk, wv, wz, wx, ws):
    b, l, d = x.shape
    nb, tl = _row_blocking(b, l, ROW_TILE)
    grid = (b // nb, l // tl)
    row = lambda n: pl.BlockSpec((nb, tl, n), lambda i, j: (i, j, 0))
    mod = pl.BlockSpec((nb, 1, d), lambda i, j: (i, 0, 0))
    wspec = lambda w: pl.BlockSpec(w.shape, lambda i, j: (0, 0), pipeline_mode=pl.Buffered(1))
    outs = [(D_ATTN, BF16), (D_ATTN, F32), (D_ATTN, F32), (D_ATTN, BF16), (D_ATTN, BF16),
            (D_INNER, F32), (CONV_DIM, F32), (LANES, F32)]
    return pl.pallas_call(
        _inproj_kernel,
        grid=grid,
        in_specs=[row(d), pl.BlockSpec((1, 1, d), lambda i, j: (0, 0, 0)), mod, mod,
                  wspec(wq), wspec(wk), wspec(wv), wspec(wz), wspec(wx), wspec(ws)],
        out_specs=[row(n) for n, _ in outs],
        out_shape=[jax.ShapeDtypeStruct((b, l, n), dt) for n, dt in outs],
        compiler_params=_params(("parallel", "parallel")),
    )(x, g.reshape(1, 1, d), sc.reshape(b, 1, d), sh.reshape(b, 1, d), wq, wk, wv, wz, wx, ws)


def _running_sum(v_ref, bias_ref, carry_ref, carry_scr, apply_logsig):
    @pl.when(pl.program_id(1) == 0)
    def _():
        carry_scr[...] = carry_ref[0]

    v = v_ref[0]
    tl = v.shape[0]
    lf = _log_sigmoid(v + bias_ref[...]) if apply_logsig else v
    row = lax.broadcasted_iota(jnp.int32, (tl, tl), 0)
    col = lax.broadcasted_iota(jnp.int32, (tl, tl), 1)
    tri = (row >= col).astype(BF16)
    cs = _tri_dot(tri, lf) + carry_scr[...]
    carry_scr[...] = cs[tl - 1:tl, :]
    return lf, cs


def _cumsum_kernel(v_ref, bias_ref, carry_ref, logf_ref, cum_ref, cumt_ref, carry_scr, *, apply_logsig):
    lf, cs = _running_sum(v_ref, bias_ref, carry_ref, carry_scr, apply_logsig)
    logf_ref[0] = lf
    cum_ref[0] = cs
    cumt_ref[0] = cs.T[:N_HEADS_A, :]


def _cumsum(vals, bias, carry, apply_logsig):
    b, l, n = vals.shape
    tl = min(CUM_TILE, l)
    assert l % tl == 0 and tl % LANES == 0
    return pl.pallas_call(
        functools.partial(_cumsum_kernel, apply_logsig=apply_logsig),
        grid=(b, l // tl),
        in_specs=[pl.BlockSpec((1, tl, n), lambda i, j: (i, j, 0)),
                  pl.BlockSpec((1, n), lambda i, j: (0, 0)),
                  pl.BlockSpec((1, 1, n), lambda i, j: (i, 0, 0))],
        out_specs=[pl.BlockSpec((1, tl, n), lambda i, j: (i, j, 0)),
                   pl.BlockSpec((1, tl, n), lambda i, j: (i, j, 0)),
                   pl.BlockSpec((1, N_HEADS_A, tl), lambda i, j: (i, 0, j))],
        out_shape=[jax.ShapeDtypeStruct((b, l, n), F32),
                   jax.ShapeDtypeStruct((b, l, n), F32),
                   jax.ShapeDtypeStruct((b, N_HEADS_A, l), F32)],
        scratch_shapes=[pltpu.VMEM((1, n), F32)],
        compiler_params=_params(("parallel", "arbitrary")),
    )(vals, bias, carry)


N_BIAS_TERMS = 3


def _cumsum_bias_kernel(v_ref, bias_ref, carry_ref, logf_ref, qx_ref, kx_ref, carry_scr):
    lf, cs = _running_sum(v_ref, bias_ref, carry_ref, carry_scr, True)
    logf_ref[0] = lf
    tl = cs.shape[0]
    c2 = cs * LOG2E
    lane = lax.broadcasted_iota(jnp.int32, (tl, LANES), 1)
    ones_q = jnp.where((lane >= N_BIAS_TERMS) & (lane < 2 * N_BIAS_TERMS), 1.0, 0.0)
    ones_k = jnp.where(lane < N_BIAS_TERMS, 1.0, 0.0)
    for h in range(N_HEADS_A):
        terms = [t.astype(F32) for t in _split3(c2[:, h:h + 1])]
        qx, kx = ones_q, ones_k
        for n, t in enumerate(terms):
            qx = jnp.where(lane == n, t, qx)
            kx = jnp.where(lane == N_BIAS_TERMS + n, -t, kx)
        qx_ref[0, :, h * LANES:(h + 1) * LANES] = qx.astype(BF16)
        kx_ref[0, :, h * LANES:(h + 1) * LANES] = kx.astype(BF16)


def _cumsum_bias(vals, bias):
    b, l, n = vals.shape
    tl = min(CUM_TILE, l)
    assert l % tl == 0
    row = lambda w: pl.BlockSpec((1, tl, w), lambda i, j: (i, j, 0))
    return pl.pallas_call(
        _cumsum_bias_kernel,
        grid=(b, l // tl),
        in_specs=[row(n), pl.BlockSpec((1, n), lambda i, j: (0, 0)), pl.BlockSpec((1, 1, n), lambda i, j: (i, 0, 0))],
        out_specs=[row(n), row(D_ATTN), row(D_ATTN)],
        out_shape=[jax.ShapeDtypeStruct((b, l, n), F32), jax.ShapeDtypeStruct((b, l, D_ATTN), BF16),
                   jax.ShapeDtypeStruct((b, l, D_ATTN), BF16)],
        scratch_shapes=[pltpu.VMEM((1, n), F32)],
        compiler_params=_params(("parallel", "arbitrary")),
    )(vals, bias, jnp.zeros((b, 1, n), F32))


_NT = (((1,), (1,)), ((), ()))


def _softmax_step(s, v, m_prev, l_prev, acc_prev):
    m_new = jnp.maximum(m_prev, jnp.max(s, axis=-1, keepdims=True))
    p = jnp.exp2(s - m_new)
    alpha = jnp.exp2(m_prev - m_new)
    l_new = alpha * l_prev + jnp.sum(p, axis=-1, keepdims=True)
    acc_new = alpha * acc_prev + jnp.dot(p.astype(BF16), v, preferred_element_type=F32)
    return m_new, l_new, acc_new


def _attn_kernel(q_ref, qx_ref, k_ref, kx_ref, v_ref, o_ref, m_scr, acc_scr, *, tile, heads):
    i = pl.program_id(2)
    dh = HEAD_DIM_A
    n_chunks = tile // LANES
    ones = jnp.ones((tile, dh), BF16)
    m_scr[...] = jnp.full(m_scr.shape, -jnp.inf, F32)
    acc_scr[...] = jnp.zeros(acc_scr.shape, F32)
    q2 = [jnp.concatenate([q_ref[0, :, hh * dh:(hh + 1) * dh], qx_ref[0, :, hh * dh:(hh + 1) * dh]], axis=1)
          for hh in range(heads)]

    def update(j, mask):
        start = pl.multiple_of(j * tile, tile)
        rows = pl.ds(start, tile)
        for hh in range(heads):
            sl = slice(hh * dh, (hh + 1) * dh)
            k2 = jnp.concatenate([k_ref[0, rows, sl], kx_ref[0, rows, sl]], axis=1)
            v2 = jnp.concatenate([v_ref[0, rows, sl], ones], axis=1)
            s = lax.dot_general(q2[hh], k2, _NT, preferred_element_type=F32)
            if mask is not None:
                s = jnp.where(mask, s, -jnp.inf)
            chunks = [s[:, c * LANES:(c + 1) * LANES] for c in range(n_chunks)]
            cmax = functools.reduce(jnp.maximum, chunks)
            m_prev = m_scr[hh]
            m_new = jnp.maximum(m_prev, jnp.max(cmax, axis=-1, keepdims=True))
            alpha = jnp.exp2(m_prev - m_new)
            p = jnp.concatenate([jnp.exp2(c - m_new) for c in chunks], axis=1).astype(BF16)
            pv = jnp.dot(p, v2, preferred_element_type=F32)
            m_scr[hh] = m_new
            acc_scr[hh, :, :dh] = alpha * acc_scr[hh, :, :dh] + pv[:, :dh]
            acc_scr[hh, :, dh:] = alpha * acc_scr[hh, :, dh:] + pv[:, dh:]

    def body(j, carry):
        update(j, None)
        return carry

    lax.fori_loop(0, i, body, 0)
    row = lax.broadcasted_iota(jnp.int32, (tile, tile), 0)
    col = lax.broadcasted_iota(jnp.int32, (tile, tile), 1)
    update(i, col <= row)
    for hh in range(heads):
        o_ref[0, :, hh * dh:(hh + 1) * dh] = (acc_scr[hh, :, :dh] / acc_scr[hh, :, dh:]).astype(o_ref.dtype)


def _attention(q, qx, k, kx, v):
    b, l, _ = q.shape
    tile = ATT_TILE if l >= 2 * ATT_TILE else LANES
    assert l % tile == 0
    heads = ATT_HEADS_PER_STEP
    w = heads * HEAD_DIM_A
    qspec = pl.BlockSpec((1, tile, w), lambda bi, h, i: (bi, i, h))
    kspec = pl.BlockSpec((1, l, w), lambda bi, h, i: (bi, 0, h))
    return pl.pallas_call(
        functools.partial(_attn_kernel, tile=tile, heads=heads),
        grid=(b, N_HEADS_A // heads, l // tile),
        in_specs=[qspec, qspec, kspec, kspec, kspec],
        out_specs=qspec,
        out_shape=jax.ShapeDtypeStruct((b, l, D_ATTN), BF16),
        scratch_shapes=[pltpu.VMEM((heads, tile, LANES), F32), pltpu.VMEM((heads, tile, 2 * HEAD_DIM_A), F32)],
        compiler_params=_params(("parallel", "parallel", "arbitrary")),
    )(q, qx, k, kx, v)


def _attn_cached_kernel(q_ref, kn_ref, vn_ref, kc_ref, vc_ref, cumn_ref, cumtp_ref, cumtn_ref, o_ref,
                        m_scr, l_scr, acc_scr):
    j = pl.program_id(1)
    ld = q_ref.shape[1]
    dh = HEAD_DIM_A

    @pl.when(j == 0)
    def _():
        m_scr[...] = jnp.full(m_scr.shape, -jnp.inf, F32)
        l_scr[...] = jnp.zeros(l_scr.shape, F32)
        acc_scr[...] = jnp.zeros(acc_scr.shape, F32)

    def head_step(h, kh, vh, ck, mask):
        sl = slice(h * dh, (h + 1) * dh)
        s = lax.dot_general(q_ref[0, :, sl], kh, _NT, preferred_element_type=F32)
        s = s + (cumn_ref[0, :ld, h:h + 1] - ck) * LOG2E
        if mask is not None:
            s = jnp.where(mask, s, -jnp.inf)
        m, l, acc = _softmax_step(s, vh, m_scr[h], l_scr[h], acc_scr[:, sl])
        m_scr[h] = m
        l_scr[h] = l
        acc_scr[:, sl] = acc

    for h in range(N_HEADS_A):
        sl = slice(h * dh, (h + 1) * dh)
        rows = pl.ds(h, kc_ref.shape[1] // N_HEADS_A, stride=N_HEADS_A)
        head_step(h, kc_ref[0, rows, :].astype(BF16), vc_ref[0, rows, :].astype(BF16),
                  cumtp_ref[0, h:h + 1, :], None)

    @pl.when(j == pl.num_programs(1) - 1)
    def _():
        row = lax.broadcasted_iota(jnp.int32, (ld, ld), 0)
        col = lax.broadcasted_iota(jnp.int32, (ld, ld), 1)
        for h in range(N_HEADS_A):
            sl = slice(h * dh, (h + 1) * dh)
            head_step(h, kn_ref[0, :, sl], vn_ref[0, :, sl], cumtn_ref[0, h:h + 1, :ld], col <= row)
            o_ref[0, :, sl] = (acc_scr[:, sl] / l_scr[h]).astype(o_ref.dtype)


def _attention_cached(q, kn, vn, kc, vc, cum_new, cumt_past, cumt_new):
    b, ld, _ = q.shape
    lp = kc.shape[1]
    tk = min(DEC_KV_TILE, lp)
    assert lp % tk == 0
    lpad = cum_new.shape[1]
    new = pl.BlockSpec((1, ld, D_ATTN), lambda bi, j: (bi, 0, 0))
    past = pl.BlockSpec((1, tk * N_HEADS_A, HEAD_DIM_A), lambda bi, j: (bi, j, 0))
    kc = kc.reshape(b, lp * N_HEADS_A, HEAD_DIM_A)
    vc = vc.reshape(b, lp * N_HEADS_A, HEAD_DIM_A)
    return pl.pallas_call(
        _attn_cached_kernel,
        grid=(b, lp // tk),
        in_specs=[new, new, new, past, past,
                  pl.BlockSpec((1, lpad, LANES), lambda bi, j: (bi, 0, 0)),
                  pl.BlockSpec((1, N_HEADS_A, tk), lambda bi, j: (bi, 0, j)),
                  pl.BlockSpec((1, N_HEADS_A, lpad), lambda bi, j: (bi, 0, 0))],
        out_specs=new,
        out_shape=jax.ShapeDtypeStruct((b, ld, D_ATTN), BF16),
        scratch_shapes=[pltpu.VMEM((N_HEADS_A, ld, 1), F32), pltpu.VMEM((N_HEADS_A, ld, 1), F32),
                        pltpu.VMEM((ld, D_ATTN), F32)],
        compiler_params=_params(("parallel", "arbitrary")),
    )(q, kn, vn, kc, vc, cum_new, cumt_past, cumt_new)


def _expand_heads(a):
    r = a.shape[0]
    low = lax.broadcasted_iota(jnp.int32, (r, LANES), 1) < HEAD_DIM_S
    return jnp.concatenate(
        [jnp.where(low, a[:, 2 * j:2 * j + 1], a[:, 2 * j + 1:2 * j + 2]) for j in range(N_HEADS_S // 2)], axis=1)


def _ssd_kernel(xbc_ref, z_ref, dt_ref, dtt_ref, past_ref, h0_ref, cw_ref, cb_ref, dtb_ref, dtbt_ref,
                alog_ref, alogt_ref, dsk_ref, nw_ref, o_ref, hout_ref, xbuf, ht_scr):
    c = pl.program_id(1)
    lc = xbc_ref.shape[1]
    hist = SUBLANES

    @pl.when(c == 0)
    def _():
        xbuf[0:hist, :] = past_ref[0]
        ht_scr[...] = h0_ref[0]

    xbuf[hist:hist + lc, :] = xbc_ref[0]
    u = cb_ref[...]
    for w in range(CONV_W):
        off = hist - (CONV_W - 1) + w
        u = u + xbuf[off:off + lc, :] * cw_ref[w:w + 1, :]
    xbuf[0:hist, :] = xbuf[lc:lc + hist, :]
    u = _silu(u)
    xs = u[:, :D_INNER]
    gn = N_GROUPS_S * D_STATE
    bm = u[:, D_INNER:D_INNER + gn].astype(BF16)
    cm = u[:, D_INNER + gn:].astype(BF16)

    dt = _softplus(dt_ref[0] + dtb_ref[...])
    dtt = _softplus(dtt_ref[0] + dtbt_ref[...])
    a = dt * (-jnp.exp(alog_ref[...]))
    at = dtt * (-jnp.exp(alogt_ref[...]))
    row = lax.broadcasted_iota(jnp.int32, (lc, lc), 0)
    col = lax.broadcasted_iota(jnp.int32, (lc, lc), 1)
    causal = col <= row
    a_cs = _tri_dot(causal.astype(BF16), a)
    a_cst = _dot_tri(at, (row <= col).astype(BF16))
    total = a_cs[lc - 1:lc, :]
    dt_e = _expand_heads(dt)
    w_e = _expand_heads(dt * jnp.exp(total - a_cs))
    ea_e = _expand_heads(jnp.exp(a_cs))
    cd_e = _expand_heads(jnp.exp(total))
    xdt = xs * dt_e
    xdw = (xs * w_e).astype(BF16)

    cbs = [lax.dot_general(cm[:, g * D_STATE:(g + 1) * D_STATE], bm[:, g * D_STATE:(g + 1) * D_STATE], _NT,
                           preferred_element_type=F32) for g in range(N_GROUPS_S)]
    low = lax.broadcasted_iota(jnp.int32, (lc, LANES), 1) < HEAD_DIM_S
    heads_per_group = N_HEADS_S // N_GROUPS_S
    yd = []
    for j in range(N_HEADS_S // 2):
        ms = []
        for hh in (2 * j, 2 * j + 1):
            seg = a_cs[:, hh:hh + 1] - a_cst[hh:hh + 1, :]
            dec = jnp.where(causal, jnp.exp(jnp.where(causal, seg, 0.0)), 0.0)
            ms.append((cbs[hh // heads_per_group] * dec).astype(BF16))
        xb = xdt[:, j * LANES:(j + 1) * LANES]
        rhs = jnp.concatenate([jnp.where(low, xb, 0.0), jnp.where(low, 0.0, xb)], axis=0).astype(BF16)
        yd.append(jnp.dot(jnp.concatenate(ms, axis=1), rhs, preferred_element_type=F32))
    y = jnp.concatenate(yd, axis=1)

    half = D_INNER // N_GROUPS_S
    ht = ht_scr[...]
    htb = ht.astype(BF16)
    y_off = jnp.concatenate(
        [jnp.dot(cm[:, g * D_STATE:(g + 1) * D_STATE], htb[:, g * half:(g + 1) * half],
                 preferred_element_type=F32) for g in range(N_GROUPS_S)], axis=1)
    st = jnp.concatenate(
        [lax.dot_general(bm[:, g * D_STATE:(g + 1) * D_STATE], xdw[:, g * half:(g + 1) * half],
                         (((0,), (0,)), ((), ())), preferred_element_type=F32) for g in range(N_GROUPS_S)], axis=1)
    ht_new = cd_e * ht + st
    ht_scr[...] = ht_new
    hout_ref[0] = ht_new

    y = y + y_off * ea_e + dsk_ref[...] * xs
    gz = y * _silu(z_ref[0])
    outs = []
    for g in range(N_GROUPS_S):
        gg = gz[:, g * half:(g + 1) * half]
        outs.append(gg * lax.rsqrt(jnp.mean(gg * gg, axis=-1, keepdims=True) + EPS))
    o_ref[0] = (jnp.concatenate(outs, axis=1) * nw_ref[...]).astype(o_ref.dtype)


def _ssd(xbc, z, dt_raw, conv_past, h0t, conv_w, conv_b, dt_bias, a_log, d_skip, norm_w):
    b, l, _ = xbc.shape
    lc = min(SSD_CHUNK, l)
    assert l % lc == 0 and lc % SUBLANES == 0 and lc >= SUBLANES
    nh = N_HEADS_S
    dtt = jnp.swapaxes(dt_raw, 1, 2)
    past = jnp.pad(conv_past, ((0, 0), (SUBLANES - (CONV_W - 1), 0), (0, 0)))
    const = lambda shape: pl.BlockSpec(shape, lambda i, j: tuple(0 for _ in shape))
    o, hout = pl.pallas_call(
        _ssd_kernel,
        grid=(b, l // lc),
        in_specs=[pl.BlockSpec((1, lc, CONV_DIM), lambda i, j: (i, j, 0)),
                  pl.BlockSpec((1, lc, D_INNER), lambda i, j: (i, j, 0)),
                  pl.BlockSpec((1, lc, nh), lambda i, j: (i, j, 0)),
                  pl.BlockSpec((1, nh, lc), lambda i, j: (i, 0, j)),
                  pl.BlockSpec((1, SUBLANES, CONV_DIM), lambda i, j: (i, 0, 0)),
                  pl.BlockSpec((1, D_STATE, D_INNER), lambda i, j: (i, 0, 0)),
                  const((CONV_W, CONV_DIM)), const((1, CONV_DIM)),
                  const((1, nh)), const((nh, 1)), const((1, nh)), const((nh, 1)),
                  const((1, D_INNER)), const((1, D_INNER))],
        out_specs=[pl.BlockSpec((1, lc, D_INNER), lambda i, j: (i, j, 0)),
                   pl.BlockSpec((1, D_STATE, D_INNER), lambda i, j: (i, 0, 0))],
        out_shape=[jax.ShapeDtypeStruct((b, l, D_INNER), BF16),
                   jax.ShapeDtypeStruct((b, D_STATE, D_INNER), F32)],
        scratch_shapes=[pltpu.VMEM((lc + SUBLANES, CONV_DIM), F32), pltpu.VMEM((D_STATE, D_INNER), F32)],
        compiler_params=_params(("parallel", "arbitrary")),
    )(xbc, z, dt_raw, dtt, past, h0t, conv_w, conv_b.reshape(1, -1),
      dt_bias.reshape(1, nh), dt_bias.reshape(nh, 1), a_log.reshape(1, nh), a_log.reshape(nh, 1),
      jnp.repeat(d_skip, HEAD_DIM_S).reshape(1, -1), norm_w.reshape(1, -1))
    return o, hout


def _outproj_kernel(x_ref, oa_ref, os_ref, gt_ref, sc_ref, sh_ref, g_ref, wa_ref, ws_ref, rw_ref, rb_ref,
                    x1_ref, h2_ref, lg_ref):
    nb, tl, d = x_ref.shape
    oa = oa_ref[...].reshape(nb * tl, -1)
    os_ = os_ref[...].reshape(nb * tl, -1)
    mix = jnp.dot(oa, wa_ref[...], preferred_element_type=F32) + jnp.dot(os_, ws_ref[...], preferred_element_type=F32)
    x1 = x_ref[...] + gt_ref[...] * mix.reshape(nb, tl, d)
    x1_ref[...] = x1
    ms = jnp.mean(x1 * x1, axis=-1, keepdims=True)
    h2 = x1 * lax.rsqrt(ms + EPS) * g_ref[...]
    h2 = h2 * (1.0 + sc_ref[...]) + sh_ref[...]
    h2 = h2.reshape(nb * tl, d)
    _store_row_tiles(h2_ref, h2)
    h_hi, h_lo = _split2(h2)
    w_hi, w_lo = rw_ref[0], rw_ref[1]
    lg = (jnp.dot(h_hi, w_hi, preferred_element_type=F32) + jnp.dot(h_lo, w_hi, preferred_element_type=F32)
          + jnp.dot(h_hi, w_lo, preferred_element_type=F32)) + rb_ref[...]
    lg_ref[...] = lg.reshape(nb, tl, -1)


def _outproj(x, oa, os_, gt, sc, sh, g, wa, ws, rw, rb):
    b, l, d = x.shape
    nb, tl = _row_blocking(b, l, ROW_TILE)
    row = lambda n: pl.BlockSpec((nb, tl, n), lambda i, j: (i, j, 0))
    mod = pl.BlockSpec((nb, 1, d), lambda i, j: (i, 0, 0))
    const = lambda a: pl.BlockSpec(a.shape, lambda i, j: tuple(0 for _ in a.shape))
    g3 = g.reshape(1, 1, d)
    return pl.pallas_call(
        _outproj_kernel,
        grid=(b // nb, l // tl),
        in_specs=[row(d), row(D_ATTN), row(D_INNER), mod, mod, mod, const(g3), const(wa), const(ws),
                  const(rw), const(rb)],
        out_specs=[row(d), pl.BlockSpec((nb * tl * ROW_TILE_SUBLANES, LANES), lambda i, j: (i * (l // tl) + j, 0)),
                   row(LANES)],
        out_shape=[jax.ShapeDtypeStruct((b, l, d), F32), jax.ShapeDtypeStruct((b * l * ROW_TILE_SUBLANES, LANES), F32),
                   jax.ShapeDtypeStruct((b, l, LANES), F32)],
        compiler_params=_params(("parallel", "parallel")),
    )(x, oa, os_, gt.reshape(b, 1, d), sc.reshape(b, 1, d), sh.reshape(b, 1, d), g3, wa, ws, rw, rb)


def _route_kernel(lg_ref, idx_ref, gate_ref, cnt_ref, carry_scr):
    @pl.when(pl.program_id(0) == 0)
    def _():
        carry_scr[...] = jnp.zeros(carry_scr.shape, F32)

    v = lg_ref[...]
    tr = v.shape[0]
    lane = lax.broadcasted_iota(jnp.int32, (tr, LANES), 1)
    lane_f = lane.astype(F32)
    tops, idxs = [], []
    onehot = jnp.zeros((tr, LANES), F32)
    for _ in range(TOP_K):
        m = jnp.max(v, axis=-1, keepdims=True)
        idx = jnp.min(jnp.where(v == m, lane_f, float(LANES)), axis=-1, keepdims=True)
        hit = lane_f == idx
        v = jnp.where(hit, -jnp.inf, v)
        onehot = onehot + hit.astype(F32)
        tops.append(m)
        idxs.append(idx)
    es = [jnp.exp(t - tops[0]) for t in tops]
    denom = es[0] + es[1] + es[2] + es[3]

    row = lax.broadcasted_iota(jnp.int32, (tr, tr), 0)
    col = lax.broadcasted_iota(jnp.int32, (tr, tr), 1)
    before = jnp.dot((col < row).astype(BF16), onehot.astype(BF16), preferred_element_type=F32) + carry_scr[...]

    idx_out = jnp.zeros((tr, LANES), F32)
    gate_out = jnp.zeros((tr, LANES), F32)
    for k in range(TOP_K):
        rank = jnp.sum(jnp.where(lane_f == idxs[k], before, 0.0), axis=-1, keepdims=True)
        idx_out = jnp.where(lane == k, idxs[k], idx_out)
        idx_out = jnp.where(lane == TOP_K + k, rank, idx_out)
        gate_out = jnp.where(lane == k, es[k] / denom, gate_out)
    idx_ref[...] = idx_out.astype(jnp.int32)
    gate_ref[...] = gate_out
    carry_scr[...] = carry_scr[...] + jnp.sum(onehot, axis=0, keepdims=True)
    cnt_ref[...] = carry_scr[...]


def _route(logits):
    t, n = logits.shape
    tr = min(ROUTE_TILE, t)
    assert t % tr == 0
    return pl.pallas_call(
        _route_kernel,
        grid=(t // tr,),
        in_specs=[pl.BlockSpec((tr, n), lambda i: (i, 0))],
        out_specs=[pl.BlockSpec((tr, n), lambda i: (i, 0)), pl.BlockSpec((tr, n), lambda i: (i, 0)),
                   pl.BlockSpec((1, n), lambda i: (0, 0))],
        out_shape=[jax.ShapeDtypeStruct((t, n), jnp.int32), jax.ShapeDtypeStruct((t, n), F32),
                   jax.ShapeDtypeStruct((1, n), F32)],
        scratch_shapes=[pltpu.VMEM((1, n), F32)],
        compiler_params=_params(("arbitrary",)),
    )(logits)


def _row_tile(ref, r):
    start = r * ROW_TILE_SUBLANES
    if not isinstance(r, int):
        start = pl.multiple_of(start, ROW_TILE_SUBLANES)
    return ref.at[pl.ds(start, ROW_TILE_SUBLANES), :]


def _store_row_tiles(ref, x, tile0=0):
    rows = x.shape[0]
    for s in range(x.shape[1] // LANES):
        ref[pl.ds(tile0 + s, rows, stride=ROW_TILE_SUBLANES), :] = x[:, s * LANES:(s + 1) * LANES]


def _load_row_tiles(ref, row0, rows):
    return jnp.concatenate(
        [ref[pl.ds(row0 * ROW_TILE_SUBLANES + s, rows, stride=ROW_TILE_SUBLANES), :] for s in range(ROW_TILE_SUBLANES)],
        axis=1)


def _gather_rows(idx_ref, src_hbm, dst, sem, n_rows):
    group = 8
    assert n_rows % group == 0

    def body(g, carry):
        for u in range(group):
            r = g * group + u
            pltpu.make_async_copy(_row_tile(src_hbm, idx_ref[0, 0, r]), _row_tile(dst, r), sem).start(priority=u % 2)
        return carry
    lax.fori_loop(0, n_rows // group, body, 0)


def _wait_rows(src_hbm, dst, sem, n_rows):
    pltpu.make_async_copy(src_hbm.at[pl.ds(0, n_rows * ROW_TILE_SUBLANES), :], dst, sem).wait()


EXPERT_COL_CHUNK = 256


GATHER_BUFS = 3


def _dispatch_kernel(ends_ref, pad_ref, dest_ref, h_hbm, xs_hbm, zbuf, hbuf, zsem, fsem, sem, *, n_slots):
    i = pl.program_id(0)
    n = pl.num_programs(0)
    m = zbuf.shape[0] // ROW_TILE_SUBLANES
    rows = dest_ref.shape[2]
    td = rows // TOP_K
    slot = lax.rem(i, GATHER_BUFS)

    def fetch(step, s):
        start = pl.multiple_of(step * (td * ROW_TILE_SUBLANES), td * ROW_TILE_SUBLANES)
        return pltpu.make_async_copy(h_hbm.at[pl.ds(start, td * ROW_TILE_SUBLANES), :], hbuf.at[s], fsem.at[s])

    def wait_rows_of(s):
        for _ in range(TOP_K):
            pltpu.make_async_copy(hbuf.at[s], xs_hbm.at[pl.ds(0, td * ROW_TILE_SUBLANES), :], sem.at[s]).wait()

    def zero_block(first_slot):
        start = pl.multiple_of(first_slot * ROW_TILE_SUBLANES, ROW_TILE_SUBLANES)
        return pltpu.make_async_copy(zbuf, xs_hbm.at[pl.ds(start, zbuf.shape[0]), :], zsem)

    @pl.when(i == 0)
    def _():
        zbuf[...] = jnp.zeros(zbuf.shape, zbuf.dtype)
        jobs = [(pad_ref[e] > 0, ends_ref[e] - m) for e in range(N_EXPERTS)]
        jobs += [(ends_ref[N_EXPERTS - 1] + j * m < n_slots, ends_ref[N_EXPERTS - 1] + j * m)
                 for j in range(N_EXPERTS + 1)]
        for cond, first in jobs:
            @pl.when(cond)
            def _():
                zero_block(first).start()
        for cond, first in jobs:
            @pl.when(cond)
            def _():
                zero_block(first).wait()
        fetch(0, 0).start()

    @pl.when(i >= 2)
    def _():
        wait_rows_of(lax.rem(i + 1, GATHER_BUFS))

    @pl.when(i + 1 < n)
    def _():
        fetch(i + 1, lax.rem(i + 1, GATHER_BUFS)).start()

    fetch(i, slot).wait()
    for j in range(rows):
        pltpu.make_async_copy(_row_tile(hbuf.at[slot], j // TOP_K), _row_tile(xs_hbm, dest_ref[0, 0, j]),
                              sem.at[slot]).start(priority=j % 2)

    @pl.when(i == n - 1)
    def _():
        @pl.when(i >= 1)
        def _():
            wait_rows_of(lax.rem(i + 2, GATHER_BUFS))
        wait_rows_of(slot)


def _dispatch(dest, ends_p, padded, h2, m, n_slots):
    t = dest.shape[0]
    td = min(DISPATCH_TILE, t)
    assert t % td == 0
    n_steps = t // td
    grid_spec = pltpu.PrefetchScalarGridSpec(
        num_scalar_prefetch=2,
        grid=(n_steps,),
        in_specs=[pl.BlockSpec((1, 1, TOP_K * td), lambda i, e, p: (i, 0, 0), memory_space=pltpu.SMEM),
                  pl.BlockSpec(memory_space=pl.ANY)],
        out_specs=pl.BlockSpec(memory_space=pl.ANY),
        scratch_shapes=[pltpu.VMEM((m * ROW_TILE_SUBLANES, LANES), F32),
                        pltpu.VMEM((GATHER_BUFS, td * ROW_TILE_SUBLANES, LANES), F32),
                        pltpu.SemaphoreType.DMA(()), pltpu.SemaphoreType.DMA((GATHER_BUFS,)),
                        pltpu.SemaphoreType.DMA((GATHER_BUFS,))],
    )
    return pl.pallas_call(
        functools.partial(_dispatch_kernel, n_slots=n_slots),
        grid_spec=grid_spec,
        out_shape=jax.ShapeDtypeStruct((n_slots * ROW_TILE_SUBLANES, LANES), F32),
        compiler_params=_params(("arbitrary",)),
    )(ends_p, padded, dest.reshape(n_steps, 1, TOP_K * td), h2)


def _experts_kernel(be_ref, x_ref, wgu_ref, bgu_ref, wdn_ref, bdn_ref, o_ref):
    m = x_ref.shape[0] // ROW_TILE_SUBLANES
    cw = EXPERT_COL_CHUNK
    tiles_per_piece = cw // LANES
    n_ff, n_out = D_FF // cw, ROW_TILE_SUBLANES // tiles_per_piece
    x = _load_row_tiles(x_ref, 0, m).astype(BF16)
    acts = []
    for c in range(n_ff):
        g = jnp.dot(x, wgu_ref[0, :, c * cw:(c + 1) * cw], preferred_element_type=F32) + bgu_ref[0, :, c * cw:(c + 1) * cw]
        u = (jnp.dot(x, wgu_ref[0, :, D_FF + c * cw:D_FF + (c + 1) * cw], preferred_element_type=F32)
             + bgu_ref[0, :, D_FF + c * cw:D_FF + (c + 1) * cw])
        gate = jnp.minimum(g, SWIGLU_LIMIT)
        up = jnp.clip(u, -SWIGLU_LIMIT, SWIGLU_LIMIT)
        acts.append(((up + 1.0) * gate * jax.nn.sigmoid(SWIGLU_ALPHA * gate)).astype(BF16))
    act = jnp.concatenate(acts, axis=1)
    for c in range(n_out):
        y = jnp.dot(act, wdn_ref[0, :, c * cw:(c + 1) * cw], preferred_element_type=F32) + bdn_ref[0, :, c * cw:(c + 1) * cw]
        _store_row_tiles(o_ref, y, tile0=c * tiles_per_piece)


def _experts(block_e, xs, wgu, bgu, wdn, bdn, m):
    n_blocks = block_e.shape[0]
    d = D_MODEL
    mt = m * ROW_TILE_SUBLANES
    grid_spec = pltpu.PrefetchScalarGridSpec(
        num_scalar_prefetch=1,
        grid=(n_blocks,),
        in_specs=[pl.BlockSpec((mt, LANES), lambda i, be: (i, 0)),
                  pl.BlockSpec((1, d, 2 * D_FF), lambda i, be: (be[i], 0, 0)),
                  pl.BlockSpec((1, 1, 2 * D_FF), lambda i, be: (be[i], 0, 0)),
                  pl.BlockSpec((1, D_FF, d), lambda i, be: (be[i], 0, 0)),
                  pl.BlockSpec((1, 1, d), lambda i, be: (be[i], 0, 0))],
        out_specs=pl.BlockSpec((mt, LANES), lambda i, be: (i, 0)),
    )
    return pl.pallas_call(
        _experts_kernel,
        grid_spec=grid_spec,
        out_shape=jax.ShapeDtypeStruct((n_blocks * mt, LANES), F32),
        compiler_params=_params(("arbitrary",)),
    )(block_e, xs, wgu, bgu.reshape(N_EXPERTS, 1, -1), wdn, bdn.reshape(N_EXPERTS, 1, -1))


def _combine_kernel(idx0_ref, idx1_ref, idxn_ref, y_hbm, x1_ref, gate_ref, gt_ref, g_ref, o_ref, ybuf, sem):
    i = pl.program_id(0)
    n = pl.num_programs(0)
    nb, tl, d = x1_ref.shape
    tc = nb * tl
    slot = lax.rem(i, GATHER_BUFS)
    nxt = lax.rem(i + 2, GATHER_BUFS)

    @pl.when(i == 0)
    def _():
        _gather_rows(idx0_ref, y_hbm, ybuf.at[0], sem.at[0], TOP_K * tc)
        _gather_rows(idx1_ref, y_hbm, ybuf.at[1], sem.at[1], TOP_K * tc)

    _wait_rows(y_hbm, ybuf.at[slot], sem.at[slot], TOP_K * tc)

    def prefetch(k):
        for r in range(k * tc, (k + 1) * tc):
            pltpu.make_async_copy(_row_tile(y_hbm, idxn_ref[0, 0, r]), _row_tile(ybuf.at[nxt], r),
                                  sem.at[nxt]).start(priority=r % 2)

    gates = gate_ref[...]
    y = gates[:, 0:1] * _load_row_tiles(ybuf.at[slot], 0, tc)
    prefetch(0)
    for k in range(1, TOP_K):
        y = y + gates[:, k:k + 1] * _load_row_tiles(ybuf.at[slot], k * tc, tc)
        prefetch(k)
    x2 = x1_ref[...] + gt_ref[...] * y.reshape(nb, tl, d)
    ms = jnp.mean(x2 * x2, axis=-1, keepdims=True)
    o_ref[...] = x2 * lax.rsqrt(ms + EPS) * g_ref[...]

    @pl.when(i == n - 1)
    def _():
        for ahead in (1, 2):
            s = lax.rem(i + ahead, GATHER_BUFS)
            _wait_rows(y_hbm, ybuf.at[s], sem.at[s], TOP_K * tc)


def _combine(dest, y_slots, x1, gates, gt, g_final):
    b, l, d = x1.shape
    nb, tl = _row_blocking(b, l, COMBINE_TILE)
    tc = nb * tl
    t = b * l
    n_steps = t // tc
    idx = jnp.swapaxes(dest.reshape(n_steps, tc, TOP_K), 1, 2).reshape(n_steps, 1, TOP_K * tc)
    steps_per_batch_row = l // tl
    tok = lambda i: (i // steps_per_batch_row, i % steps_per_batch_row, 0)
    smem = lambda f: pl.BlockSpec((1, 1, TOP_K * tc), f, memory_space=pltpu.SMEM)
    return pl.pallas_call(
        _combine_kernel,
        grid=(n_steps,),
        in_specs=[smem(lambda i: (0, 0, 0)),
                  smem(lambda i: (min(1, n_steps - 1), 0, 0)),
                  smem(lambda i: (jnp.minimum(i + 2, n_steps - 1), 0, 0)),
                  pl.BlockSpec(memory_space=pl.ANY),
                  pl.BlockSpec((nb, tl, d), tok),
                  pl.BlockSpec((tc, LANES), lambda i: (i, 0)),
                  pl.BlockSpec((nb, 1, d), lambda i: (i // steps_per_batch_row, 0, 0)),
                  pl.BlockSpec((1, 1, d), lambda i: (0, 0, 0))],
        out_specs=pl.BlockSpec((nb, tl, d), tok),
        out_shape=jax.ShapeDtypeStruct((b, l, d), F32),
        scratch_shapes=[pltpu.VMEM((GATHER_BUFS, TOP_K * tc * ROW_TILE_SUBLANES, LANES), F32),
                        pltpu.SemaphoreType.DMA((GATHER_BUFS,))],
        compiler_params=_params(("arbitrary",)),
    )(idx, idx, idx, y_slots, x1, gates, gt.reshape(b, 1, d), g_final.reshape(1, 1, d))


def _moe(h2, logits, x1, gt_m, g_final, wts):
    b, l, d = x1.shape
    t = b * l
    m = MOE_ROWS if t * TOP_K >= 4 * N_EXPERTS * MOE_ROWS else MOE_ROWS_SMALL
    idx, gates, counts = _route(logits.reshape(t, LANES))
    top_e = idx[:, :TOP_K]
    rank = idx[:, TOP_K:2 * TOP_K]
    counts = counts[0, :N_EXPERTS].astype(jnp.int32)
    padded = (counts + m - 1) // m * m
    ends_p = jnp.cumsum(padded)
    starts_p = ends_p - padded
    dest = starts_p[top_e] + rank
    n_blocks = (t * TOP_K + N_EXPERTS * (m - 1) + m - 1) // m
    block_start = jnp.arange(n_blocks, dtype=jnp.int32)[:, None] * m
    block_e = jnp.minimum(jnp.sum((ends_p[None, :] <= block_start).astype(jnp.int32), axis=1), N_EXPERTS - 1)
    xs = _dispatch(dest, ends_p, padded, h2, m, n_blocks * m)
    y_slots = _experts(block_e, xs, *wts, m)
    return _combine(dest, y_slots, x1, gates, gt_m, g_final)


def _layer(x, mod, k_past, v_past, logf_past, conv_past, ssm_past, p, g_final):
    b, l, d = x.shape
    sh_a, sc_a, gt_a, sh_m, sc_m, gt_m = jnp.split(mod, 6, axis=-1)
    q, kf, vf, kb, vb, z, xbc, sm = _inproj(x, p['g_mix'], sc_a, sh_a, *p['w_in'])
    dt_raw = sm[:, :, N_HEADS_A:N_HEADS_A + N_HEADS_S]

    zero_carry = jnp.zeros((b, 1, LANES), F32)
    if k_past is None:
        logf, qx, kx = _cumsum_bias(sm, p['b_f'])
        o_a = _attention(q, qx, kb, kx, vb)
        conv_past = jnp.zeros((b, CONV_W - 1, CONV_DIM), F32)
        h0t = jnp.zeros((b, D_STATE, D_INNER), F32)
    else:
        lp = k_past.shape[1]
        past = jnp.pad(logf_past, ((0, 0), (0, 0), (0, LANES - N_HEADS_A)))
        _, cum_p, cumt_p = _cumsum(past, p['b_f'], zero_carry, False)
        lpad = -(-l // LANES) * LANES
        sm_pad = jnp.pad(sm, ((0, 0), (0, lpad - l), (0, 0)))
        logf, cum_n, cumt_n = _cumsum(sm_pad, p['b_f'], cum_p[:, lp - 1:lp, :], True)
        logf = logf[:, :l]
        o_a = _attention_cached(q, kb, vb, k_past, v_past, cum_n, cumt_p, cumt_n)
        h0t = jnp.swapaxes(ssm_past.reshape(b, D_INNER, D_STATE), 1, 2)
    o_s, hout = _ssd(xbc, z, dt_raw, conv_past, h0t, p['conv_w'], p['conv_b'], p['dt_bias'], p['a_log'],
                     p['d_skip'], p['ssd_norm_w'])
    x1, h2, logits = _outproj(x, o_a, o_s, gt_a, sc_m, sh_m, p['g_ffn'], p['w_out_a'], p['w_out_s'],
                              p['router_w'], p['router_b'])
    y = _moe(h2, logits, x1, gt_m, g_final, p['experts'])

    assert l >= CONV_W - 1
    conv_new = xbc[:, l - (CONV_W - 1):]
    ssm_new = jnp.swapaxes(hout, 1, 2).reshape(b, N_HEADS_S, HEAD_DIM_S, D_STATE)
    return (y, kf.reshape(b, l, N_HEADS_A, HEAD_DIM_A), vf.reshape(b, l, N_HEADS_A, HEAD_DIM_A),
            logf[:, :, :N_HEADS_A], conv_new, ssm_new)


def kernel(x_prompt, x_sample, cache_k, cache_v, cache_logf, state_conv, state_ssm, c_prompt, c_sample, w_ada, b_ada, g_mix, w_in, b_f, conv_w, conv_b, dt_bias, a_log, d_skip, ssd_norm_w, w_out, g_ffn, router_w, router_b, w_gate_up, b_gate_up, w_down, b_down, g_final):
    assert w_ada.shape[0] == 1, "single-layer operation"
    bp = x_prompt.shape[0]
    w = w_in[0]
    q_end, k_end, v_end = D_ATTN, 2 * D_ATTN, 3 * D_ATTN
    f_end = v_end + N_HEADS_A
    z_end = f_end + D_INNER
    xbc_end = z_end + CONV_DIM
    w_small = jnp.concatenate(
        [w[:, v_end:f_end], w[:, xbc_end:], jnp.zeros((D_MODEL, LANES - N_HEADS_A - N_HEADS_S), F32)], axis=1)
    cast = lambda a: a.astype(BF16)
    p = {
        'g_mix': g_mix[0],
        'w_in': (cast(w[:, :q_end]), cast(w[:, q_end:k_end]), cast(w[:, k_end:v_end]), cast(w[:, f_end:z_end]),
                 cast(w[:, z_end:xbc_end]), cast(w_small)),
        'b_f': jnp.pad(b_f[0], (0, LANES - N_HEADS_A)).reshape(1, LANES),
        'conv_w': conv_w[0], 'conv_b': conv_b[0], 'dt_bias': dt_bias[0], 'a_log': a_log[0],
        'd_skip': d_skip[0], 'ssd_norm_w': ssd_norm_w[0],
        'w_out_a': cast(w_out[0, :D_ATTN]), 'w_out_s': cast(w_out[0, D_ATTN:]),
        'g_ffn': g_ffn[0],
        'router_w': jnp.stack(_split2(jnp.pad(router_w[0], ((0, 0), (0, LANES - N_EXPERTS))))),
        'router_b': jnp.pad(router_b[0], (0, LANES - N_EXPERTS), constant_values=-jnp.inf).reshape(1, LANES),
        'experts': (cast(w_gate_up[0]), b_gate_up[0], cast(w_down[0]), b_down[0]),
    }
    mod = _adaln(jnp.concatenate([c_prompt, c_sample], axis=0), w_ada[0], b_ada[0])
    outs_p = _layer(x_prompt, mod[:bp], None, None, None, None, None, p, g_final)
    outs_s = _layer(x_sample, mod[bp:], cache_k[0], cache_v[0], cache_logf[0], state_conv[0], state_ssm[0], p, g_final)
    stack = lambda a: a[None]
    return (outs_p[0], outs_s[0]) + tuple(stack(a) for a in outs_p[1:]) + tuple(stack(a) for a in outs_s[1:])
```

```python
import functools
import math

import jax
import jax.numpy as jnp
from jax import lax
from jax.experimental import pallas as pl
from jax.experimental.pallas import tpu as pltpu

F32 = jnp.float32
BF16 = jnp.bfloat16
HIGHEST = lax.Precision.HIGHEST

D_MODEL = 1024
N_HEADS_A = 8
HEAD_DIM_A = 128
D_ATTN = N_HEADS_A * HEAD_DIM_A
D_INNER = 1024
HEAD_DIM_S = 64
N_HEADS_S = D_INNER // HEAD_DIM_S
N_GROUPS_S = 2
D_STATE = 128
CONV_W = 4
CONV_DIM = D_INNER + 2 * N_GROUPS_S * D_STATE
N_EXPERTS = 32
TOP_K = 4
D_FF = 1024
SWIGLU_LIMIT = 7.0
SWIGLU_ALPHA = 1.702
EPS = 1e-5
LANES = 128
SUBLANES = 8
ROW_TILE_SUBLANES = D_MODEL // LANES
assert ROW_TILE_SUBLANES == SUBLANES
VMEM_LIMIT = 56 * 1024 * 1024
LOG2E = 1.4426950408889634
ATT_HEADS_PER_STEP = 4

ROW_TILE = 512
ATT_TILE = 512
DEC_KV_TILE = 512
CUM_TILE = 512
SSD_CHUNK = 128
ROUTE_TILE = 256
MOE_ROWS = 512
MOE_ROWS_SMALL = 128
COMBINE_TILE = 256
DISPATCH_TILE = 256


def _params(sem, vmem=VMEM_LIMIT):
    return pltpu.CompilerParams(dimension_semantics=sem, vmem_limit_bytes=vmem)


def _row_blocking(b, l, tile):
    if l >= tile:
        assert l % tile == 0
        return 1, tile
    nb = min(b, max(1, tile // l))
    while b % nb:
        nb -= 1
    return nb, l


def _split3(x):
    hi = x.astype(BF16)
    r1 = x - hi.astype(F32)
    mid = r1.astype(BF16)
    lo = (r1 - mid.astype(F32)).astype(BF16)
    return hi, mid, lo


def _split2(x):
    hi = x.astype(BF16)
    return hi, (x - hi.astype(F32)).astype(BF16)


def _tri_dot(tri, x):
    hi, mid, lo = _split3(x)
    d = lambda p: jnp.dot(tri, p, preferred_element_type=F32)
    return d(hi) + d(mid) + d(lo)


def _dot_tri(x, tri):
    hi, mid, lo = _split3(x)
    d = lambda p: jnp.dot(p, tri, preferred_element_type=F32)
    return d(hi) + d(mid) + d(lo)


def _silu(x):
    return x * jax.nn.sigmoid(x)


def _softplus(x):
    return jnp.maximum(x, 0.0) + jnp.log1p(jnp.exp(-jnp.abs(x)))


def _log_sigmoid(x):
    return jnp.minimum(x, 0.0) - jnp.log1p(jnp.exp(-jnp.abs(x)))


def _adaln_kernel(c_ref, w_ref, b_ref, o_ref):
    s = _silu(c_ref[...])
    o_ref[...] = jnp.dot(s, w_ref[...], precision=HIGHEST, preferred_element_type=F32) + b_ref[...]


def _adaln(c, w, b):
    m, d = c.shape
    n = w.shape[1]
    tn = 512
    return pl.pallas_call(
        _adaln_kernel,
        grid=(n // tn,),
        in_specs=[pl.BlockSpec((m, d), lambda j: (0, 0)),
                  pl.BlockSpec((d, tn), lambda j: (0, j)),
                  pl.BlockSpec((1, tn), lambda j: (0, j))],
        out_specs=pl.BlockSpec((m, tn), lambda j: (0, j)),
        out_shape=jax.ShapeDtypeStruct((m, n), F32),
        compiler_params=_params(("parallel",)),
    )(c, w, b.reshape(1, n))


def _inproj_kernel(x_ref, g_ref, sc_ref, sh_ref, wq_ref, wk_ref, wv_ref, wz_ref, wx_ref, ws_ref,
                   q_ref, kf_ref, vf_ref, kb_ref, vb_ref, z_ref, xbc_ref, sm_ref):
    x = x_ref[...]
    nb, tl, d = x.shape
    ms = jnp.mean(x * x, axis=-1, keepdims=True)
    h = x * lax.rsqrt(ms + EPS) * g_ref[...]
    h = h * (1.0 + sc_ref[...]) + sh_ref[...]
    hb = h.reshape(nb * tl, d).astype(BF16)

    def mm(w_ref):
        return jnp.dot(hb, w_ref[...], preferred_element_type=F32)

    q = mm(wq_ref) * (LOG2E / math.sqrt(HEAD_DIM_A))
    q_ref[...] = q.astype(BF16).reshape(nb, tl, -1)
    k = mm(wk_ref)
    kf_ref[...] = k.reshape(nb, tl, -1)
    kb_ref[...] = k.astype(BF16).reshape(nb, tl, -1)
    v = mm(wv_ref)
    vf_ref[...] = v.reshape(nb, tl, -1)
    vb_ref[...] = v.astype(BF16).reshape(nb, tl, -1)
    z_ref[...] = mm(wz_ref).reshape(nb, tl, -1)
    xbc_ref[...] = mm(wx_ref).reshape(nb, tl, -1)
    sm_ref[...] = mm(ws_ref).reshape(nb, tl, -1)


def _inproj(x, g, sc, sh, wq, wk, wv, wz, wx, ws):
    b, l, d = x.shape
    nb, tl = _row_blocking(b, l, ROW_TILE)
    grid = (b // nb, l // tl)
    row = lambda n: pl.BlockSpec((nb, tl, n), lambda i, j: (i, j, 0))
    mod = pl.BlockSpec((nb, 1, d), lambda i, j: (i, 0, 0))
    wspec = lambda w: pl.BlockSpec(w.shape, lambda i, j: (0, 0), pipeline_mode=pl.Buffered(1))
    outs = [(D_ATTN, BF16), (D_ATTN, F32), (D_ATTN, F32), (D_ATTN, BF16), (D_ATTN, BF16),
            (D_INNER, F32), (CONV_DIM, F32), (LANES, F32)]
    return pl.pallas_call(
        _inproj_kernel,
        grid=grid,
        in_specs=[row(d), pl.BlockSpec((1, 1, d), lambda i, j: (0, 0, 0)), mod, mod,
                  wspec(wq), wspec(wk), wspec(wv), wspec(wz), wspec(wx), wspec(ws)],
        out_specs=[row(n) for n, _ in outs],
        out_shape=[jax.ShapeDtypeStruct((b, l, n), dt) for n, dt in outs],
        compiler_params=_params(("parallel", "parallel")),
    )(x, g.reshape(1, 1, d), sc.reshape(b, 1, d), sh.reshape(b, 1, d), wq, wk, wv, wz, wx, ws)


def _running_sum(v_ref, bias_ref, carry_ref, carry_scr):
    @pl.when(pl.program_id(1) == 0)
    def _():
        carry_scr[...] = carry_ref[0]

    v = v_ref[0]
    tl = v.shape[0]
    lf = _log_sigmoid(v + bias_ref[...])
    row = lax.broadcasted_iota(jnp.int32, (tl, tl), 0)
    col = lax.broadcasted_iota(jnp.int32, (tl, tl), 1)
    tri = (row >= col).astype(BF16)
    cs = _tri_dot(tri, lf) + carry_scr[...]
    carry_scr[...] = cs[tl - 1:tl, :]
    return lf, cs


def _cumsum_kernel(v_ref, bias_ref, carry_ref, logf_ref, cum_ref, cumt_ref, carry_scr):
    lf, cs = _running_sum(v_ref, bias_ref, carry_ref, carry_scr)
    logf_ref[0] = lf
    cum_ref[0] = cs
    cumt_ref[0] = cs.T[:N_HEADS_A, :]


def _cumsum(vals, bias, carry):
    b, l, n = vals.shape
    tl = min(CUM_TILE, l)
    assert l % tl == 0 and tl % LANES == 0
    return pl.pallas_call(
        _cumsum_kernel,
        grid=(b, l // tl),
        in_specs=[pl.BlockSpec((1, tl, n), lambda i, j: (i, j, 0)),
                  pl.BlockSpec((1, n), lambda i, j: (0, 0)),
                  pl.BlockSpec((1, 1, n), lambda i, j: (i, 0, 0))],
        out_specs=[pl.BlockSpec((1, tl, n), lambda i, j: (i, j, 0)),
                   pl.BlockSpec((1, tl, n), lambda i, j: (i, j, 0)),
                   pl.BlockSpec((1, N_HEADS_A, tl), lambda i, j: (i, 0, j))],
        out_shape=[jax.ShapeDtypeStruct((b, l, n), F32),
                   jax.ShapeDtypeStruct((b, l, n), F32),
                   jax.ShapeDtypeStruct((b, N_HEADS_A, l), F32)],
        scratch_shapes=[pltpu.VMEM((1, n), F32)],
        compiler_params=_params(("parallel", "arbitrary")),
    )(vals, bias, carry)


def _cumsum_lanes_kernel(x_ref, o_ref, carry_scr):
    @pl.when(pl.program_id(0) == 0)
    def _():
        carry_scr[...] = jnp.zeros(carry_scr.shape, F32)

    x = x_ref[...]
    tl = x.shape[1]
    row = lax.broadcasted_iota(jnp.int32, (tl, tl), 0)
    col = lax.broadcasted_iota(jnp.int32, (tl, tl), 1)
    cs = _dot_tri(x, (row <= col).astype(BF16)) + carry_scr[...]
    o_ref[...] = cs
    carry_scr[...] = cs[:, tl - 1:tl]


def _cumsum_lanes(x):
    rows, l = x.shape
    tl = min(CUM_TILE, l)
    assert l % tl == 0 and rows % SUBLANES == 0
    return pl.pallas_call(
        _cumsum_lanes_kernel,
        grid=(l // tl,),
        in_specs=[pl.BlockSpec((rows, tl), lambda j: (0, j))],
        out_specs=pl.BlockSpec((rows, tl), lambda j: (0, j)),
        out_shape=jax.ShapeDtypeStruct((rows, l), F32),
        scratch_shapes=[pltpu.VMEM((rows, 1), F32)],
        compiler_params=_params(("arbitrary",)),
    )(x)


N_BIAS_TERMS = 3


def _bias_placement():
    rows = jnp.arange(N_BIAS_TERMS * LANES)
    n, h = rows // LANES, rows % LANES
    col = jnp.arange(D_ATTN)[None, :]
    valid = (h < N_HEADS_A)[:, None]
    eq = jnp.where(valid & (col == (h * LANES + n)[:, None]), 1.0, 0.0)
    ek = jnp.where(valid & (col == (h * LANES + N_BIAS_TERMS + n)[:, None]), -1.0, 0.0)
    lane = jnp.arange(D_ATTN) % LANES
    ones_q = jnp.where((lane >= N_BIAS_TERMS) & (lane < 2 * N_BIAS_TERMS), 1.0, 0.0)
    ones_k = jnp.where(lane < N_BIAS_TERMS, 1.0, 0.0)
    return eq.astype(BF16), ek.astype(BF16), ones_q.reshape(1, -1).astype(F32), ones_k.reshape(1, -1).astype(F32)


def _cumsum_bias_kernel(v_ref, bias_ref, carry_ref, eq_ref, ek_ref, oq_ref, ok_ref, logf_ref, qx_ref, kx_ref, carry_scr):
    lf, cs = _running_sum(v_ref, bias_ref, carry_ref, carry_scr)
    logf_ref[0] = lf
    terms = jnp.concatenate(_split3(cs * LOG2E), axis=1)
    qx_ref[0] = (jnp.dot(terms, eq_ref[...], preferred_element_type=F32) + oq_ref[...]).astype(BF16)
    kx_ref[0] = (jnp.dot(terms, ek_ref[...], preferred_element_type=F32) + ok_ref[...]).astype(BF16)


def _cumsum_bias(vals, bias):
    b, l, n = vals.shape
    tl = min(CUM_TILE, l)
    assert l % tl == 0
    row = lambda w: pl.BlockSpec((1, tl, w), lambda i, j: (i, j, 0))
    const = lambda a: pl.BlockSpec(a.shape, lambda i, j: (0, 0))
    placement = _bias_placement()
    return pl.pallas_call(
        _cumsum_bias_kernel,
        grid=(b, l // tl),
        in_specs=[row(n), pl.BlockSpec((1, n), lambda i, j: (0, 0)), pl.BlockSpec((1, 1, n), lambda i, j: (i, 0, 0))]
        + [const(a) for a in placement],
        out_specs=[row(n), row(D_ATTN), row(D_ATTN)],
        out_shape=[jax.ShapeDtypeStruct((b, l, n), F32), jax.ShapeDtypeStruct((b, l, D_ATTN), BF16),
                   jax.ShapeDtypeStruct((b, l, D_ATTN), BF16)],
        scratch_shapes=[pltpu.VMEM((1, n), F32)],
        compiler_params=_params(("parallel", "arbitrary")),
    )(vals, bias, jnp.zeros((b, 1, n), F32), *placement)


_NT = (((1,), (1,)), ((), ()))


def _softmax_step(s, v, m_prev, l_prev, acc_prev):
    m_new = jnp.maximum(m_prev, jnp.max(s, axis=-1, keepdims=True))
    p = jnp.exp2(s - m_new)
    alpha = jnp.exp2(m_prev - m_new)
    l_new = alpha * l_prev + jnp.sum(p, axis=-1, keepdims=True)
    acc_new = alpha * acc_prev + jnp.dot(p.astype(BF16), v, preferred_element_type=F32)
    return m_new, l_new, acc_new


def _attn_kernel(q_ref, qx_ref, k_ref, kx_ref, v_ref, o_ref, m_scr, acc_scr, *, tile, heads):
    i = pl.program_id(2)
    dh = HEAD_DIM_A
    n_chunks = tile // LANES
    ones = jnp.ones((tile, dh), BF16)
    m_scr[...] = jnp.full(m_scr.shape, -jnp.inf, F32)
    acc_scr[...] = jnp.zeros(acc_scr.shape, F32)
    q2 = [jnp.concatenate([q_ref[0, :, hh * dh:(hh + 1) * dh], qx_ref[0, :, hh * dh:(hh + 1) * dh]], axis=1)
          for hh in range(heads)]

    def update(j, mask):
        start = pl.multiple_of(j * tile, tile)
        rows = pl.ds(start, tile)
        for hh in range(heads):
            sl = slice(hh * dh, (hh + 1) * dh)
            k2 = jnp.concatenate([k_ref[0, rows, sl], kx_ref[0, rows, sl]], axis=1)
            v2 = jnp.concatenate([v_ref[0, rows, sl], ones], axis=1)
            s = lax.dot_general(q2[hh], k2, _NT, preferred_element_type=F32)
            if mask is not None:
                s = jnp.where(mask, s, -jnp.inf)
            chunks = [s[:, c * LANES:(c + 1) * LANES] for c in range(n_chunks)]
            cmax = functools.reduce(jnp.maximum, chunks)
            m_prev = m_scr[hh]
            m_new = jnp.maximum(m_prev, jnp.max(cmax, axis=-1, keepdims=True))
            alpha = jnp.exp2(m_prev - m_new)
            p = jnp.concatenate([jnp.exp2(c - m_new) for c in chunks], axis=1).astype(BF16)
            pv = jnp.dot(p, v2, preferred_element_type=F32)
            m_scr[hh] = m_new
            acc_scr[hh, :, :dh] = alpha * acc_scr[hh, :, :dh] + pv[:, :dh]
            acc_scr[hh, :, dh:] = alpha * acc_scr[hh, :, dh:] + pv[:, dh:]

    def body(j, carry):
        update(j, None)
        return carry

    lax.fori_loop(0, i, body, 0)
    row = lax.broadcasted_iota(jnp.int32, (tile, tile), 0)
    col = lax.broadcasted_iota(jnp.int32, (tile, tile), 1)
    update(i, col <= row)
    for hh in range(heads):
        o_ref[0, :, hh * dh:(hh + 1) * dh] = (acc_scr[hh, :, :dh] / acc_scr[hh, :, dh:]).astype(o_ref.dtype)


def _attention(q, qx, k, kx, v):
    b, l, _ = q.shape
    tile = ATT_TILE if l >= 2 * ATT_TILE else LANES
    assert l % tile == 0
    heads = ATT_HEADS_PER_STEP
    w = heads * HEAD_DIM_A
    qspec = pl.BlockSpec((1, tile, w), lambda bi, h, i: (bi, i, h))
    kspec = pl.BlockSpec((1, l, w), lambda bi, h, i: (bi, 0, h))
    return pl.pallas_call(
        functools.partial(_attn_kernel, tile=tile, heads=heads),
        grid=(b, N_HEADS_A // heads, l // tile),
        in_specs=[qspec, qspec, kspec, kspec, kspec],
        out_specs=qspec,
        out_shape=jax.ShapeDtypeStruct((b, l, D_ATTN), BF16),
        scratch_shapes=[pltpu.VMEM((heads, tile, LANES), F32), pltpu.VMEM((heads, tile, 2 * HEAD_DIM_A), F32)],
        compiler_params=_params(("parallel", "parallel", "arbitrary")),
    )(q, qx, k, kx, v)


def _attn_cached_kernel(q_ref, kn_ref, vn_ref, kc_ref, vc_ref, cumn_ref, cumtp_ref, cumtn_ref, o_ref,
                        m_scr, l_scr, acc_scr):
    j = pl.program_id(1)
    ld = q_ref.shape[1]
    dh = HEAD_DIM_A

    @pl.when(j == 0)
    def _():
        m_scr[...] = jnp.full(m_scr.shape, -jnp.inf, F32)
        l_scr[...] = jnp.zeros(l_scr.shape, F32)
        acc_scr[...] = jnp.zeros(acc_scr.shape, F32)

    def head_step(h, kh, vh, ck, mask):
        sl = slice(h * dh, (h + 1) * dh)
        s = lax.dot_general(q_ref[0, :, sl], kh, _NT, preferred_element_type=F32)
        s = s + (cumn_ref[0, :ld, h:h + 1] - ck) * LOG2E
        if mask is not None:
            s = jnp.where(mask, s, -jnp.inf)
        m, l, acc = _softmax_step(s, vh, m_scr[h], l_scr[h], acc_scr[:, sl])
        m_scr[h] = m
        l_scr[h] = l
        acc_scr[:, sl] = acc

    for h in range(N_HEADS_A):
        sl = slice(h * dh, (h + 1) * dh)
        rows = pl.ds(h, kc_ref.shape[1] // N_HEADS_A, stride=N_HEADS_A)
        head_step(h, kc_ref[0, rows, :].astype(BF16), vc_ref[0, rows, :].astype(BF16),
                  cumtp_ref[0, h:h + 1, :], None)

    @pl.when(j == pl.num_programs(1) - 1)
    def _():
        row = lax.broadcasted_iota(jnp.int32, (ld, ld), 0)
        col = lax.broadcasted_iota(jnp.int32, (ld, ld), 1)
        for h in range(N_HEADS_A):
            sl = slice(h * dh, (h + 1) * dh)
            head_step(h, kn_ref[0, :, sl], vn_ref[0, :, sl], cumtn_ref[0, h:h + 1, :ld], col <= row)
            o_ref[0, :, sl] = (acc_scr[:, sl] / l_scr[h]).astype(o_ref.dtype)


def _attention_cached(q, kn, vn, kc, vc, cum_new, cumt_past, cumt_new):
    b, ld, _ = q.shape
    lp = kc.shape[1]
    tk = min(DEC_KV_TILE, lp)
    assert lp % tk == 0
    lpad = cum_new.shape[1]
    new = pl.BlockSpec((1, ld, D_ATTN), lambda bi, j: (bi, 0, 0))
    past = pl.BlockSpec((1, tk * N_HEADS_A, HEAD_DIM_A), lambda bi, j: (bi, j, 0))
    kc = kc.reshape(b, lp * N_HEADS_A, HEAD_DIM_A)
    vc = vc.reshape(b, lp * N_HEADS_A, HEAD_DIM_A)
    return pl.pallas_call(
        _attn_cached_kernel,
        grid=(b, lp // tk),
        in_specs=[new, new, new, past, past,
                  pl.BlockSpec((1, lpad, LANES), lambda bi, j: (bi, 0, 0)),
                  pl.BlockSpec((1, N_HEADS_A, tk), lambda bi, j: (bi, 0, j)),
                  pl.BlockSpec((1, N_HEADS_A, lpad), lambda bi, j: (bi, 0, 0))],
        out_specs=new,
        out_shape=jax.ShapeDtypeStruct((b, ld, D_ATTN), BF16),
        scratch_shapes=[pltpu.VMEM((N_HEADS_A, ld, 1), F32), pltpu.VMEM((N_HEADS_A, ld, 1), F32),
                        pltpu.VMEM((ld, D_ATTN), F32)],
        compiler_params=_params(("parallel", "arbitrary")),
    )(q, kn, vn, kc, vc, cum_new, cumt_past, cumt_new)


def _expand_heads(a):
    r = a.shape[0]
    low = lax.broadcasted_iota(jnp.int32, (r, LANES), 1) < HEAD_DIM_S
    return jnp.concatenate(
        [jnp.where(low, a[:, 2 * j:2 * j + 1], a[:, 2 * j + 1:2 * j + 2]) for j in range(N_HEADS_S // 2)], axis=1)


CONV_HIST = SUBLANES


def _ssd_kernel(xbc_ref, z_ref, dt_ref, dtt_ref, past_ref, h0_ref, cw_ref, cb_ref, dtb_ref, dtbt_ref,
                alog_ref, alogt_ref, dsk_ref, nw_ref, o_ref, hout_ref, xbuf, ht_scr):
    c = pl.program_id(1)
    lc = xbc_ref.shape[1]
    hist = CONV_HIST

    @pl.when(c == 0)
    def _():
        xbuf[0:hist, :] = past_ref[0]
        ht_scr[...] = h0_ref[0]

    xbuf[hist:hist + lc, :] = xbc_ref[0]
    u = cb_ref[...]
    for w in range(CONV_W):
        off = hist - (CONV_W - 1) + w
        u = u + xbuf[off:off + lc, :] * cw_ref[w:w + 1, :]
    xbuf[0:hist, :] = xbuf[lc:lc + hist, :]
    u = _silu(u)
    xs = u[:, :D_INNER]
    gn = N_GROUPS_S * D_STATE
    bm = u[:, D_INNER:D_INNER + gn].astype(BF16)
    cm = u[:, D_INNER + gn:].astype(BF16)

    dt = _softplus(dt_ref[0] + dtb_ref[...])
    dtt = _softplus(dtt_ref[0] + dtbt_ref[...])
    a = dt * (-jnp.exp(alog_ref[...]))
    at = dtt * (-jnp.exp(alogt_ref[...]))
    row = lax.broadcasted_iota(jnp.int32, (lc, lc), 0)
    col = lax.broadcasted_iota(jnp.int32, (lc, lc), 1)
    causal = col <= row
    a_cs = _tri_dot(causal.astype(BF16), a)
    a_cst = _dot_tri(at, (row <= col).astype(BF16))
    total = a_cs[lc - 1:lc, :]
    dt_e = _expand_heads(dt)
    w_e = _expand_heads(dt * jnp.exp(total - a_cs))
    ea_e = _expand_heads(jnp.exp(a_cs))
    cd_e = _expand_heads(jnp.exp(total))
    xdt = xs * dt_e
    xdw = (xs * w_e).astype(BF16)

    cbs = [lax.dot_general(cm[:, g * D_STATE:(g + 1) * D_STATE], bm[:, g * D_STATE:(g + 1) * D_STATE], _NT,
                           preferred_element_type=F32) for g in range(N_GROUPS_S)]
    low = lax.broadcasted_iota(jnp.int32, (lc, LANES), 1) < HEAD_DIM_S
    heads_per_group = N_HEADS_S // N_GROUPS_S
    yd = []
    for j in range(N_HEADS_S // 2):
        ms = []
        for hh in (2 * j, 2 * j + 1):
            seg = a_cs[:, hh:hh + 1] - a_cst[hh:hh + 1, :]
            dec = jnp.where(causal, jnp.exp(jnp.where(causal, seg, 0.0)), 0.0)
            ms.append((cbs[hh // heads_per_group] * dec).astype(BF16))
        xb = xdt[:, j * LANES:(j + 1) * LANES]
        rhs = jnp.concatenate([jnp.where(low, xb, 0.0), jnp.where(low, 0.0, xb)], axis=0).astype(BF16)
        yd.append(jnp.dot(jnp.concatenate(ms, axis=1), rhs, preferred_element_type=F32))
    y = jnp.concatenate(yd, axis=1)

    half = D_INNER // N_GROUPS_S
    ht = ht_scr[...]
    htb = ht.astype(BF16)
    y_off = jnp.concatenate(
        [jnp.dot(cm[:, g * D_STATE:(g + 1) * D_STATE], htb[:, g * half:(g + 1) * half],
                 preferred_element_type=F32) for g in range(N_GROUPS_S)], axis=1)
    st = jnp.concatenate(
        [lax.dot_general(bm[:, g * D_STATE:(g + 1) * D_STATE], xdw[:, g * half:(g + 1) * half],
                         (((0,), (0,)), ((), ())), preferred_element_type=F32) for g in range(N_GROUPS_S)], axis=1)
    ht_new = cd_e * ht + st
    ht_scr[...] = ht_new
    hout_ref[0] = ht_new

    y = y + y_off * ea_e + dsk_ref[...] * xs
    gz = y * _silu(z_ref[0])
    outs = []
    for g in range(N_GROUPS_S):
        gg = gz[:, g * half:(g + 1) * half]
        outs.append(gg * lax.rsqrt(jnp.mean(gg * gg, axis=-1, keepdims=True) + EPS))
    o_ref[0] = (jnp.concatenate(outs, axis=1) * nw_ref[...]).astype(o_ref.dtype)


def _ssd(xbc, z, dt_raw, conv_past, h0t, conv_w, conv_b, dt_bias, a_log, d_skip, norm_w):
    b, l, _ = xbc.shape
    lc = min(SSD_CHUNK, l)
    assert l % lc == 0 and lc % CONV_HIST == 0
    nh = N_HEADS_S
    dtt = jnp.swapaxes(dt_raw, 1, 2)
    past = jnp.pad(conv_past, ((0, 0), (CONV_HIST - (CONV_W - 1), 0), (0, 0)))
    const = lambda shape: pl.BlockSpec(shape, lambda i, j: tuple(0 for _ in shape))
    o, hout = pl.pallas_call(
        _ssd_kernel,
        grid=(b, l // lc),
        in_specs=[pl.BlockSpec((1, lc, CONV_DIM), lambda i, j: (i, j, 0)),
                  pl.BlockSpec((1, lc, D_INNER), lambda i, j: (i, j, 0)),
                  pl.BlockSpec((1, lc, nh), lambda i, j: (i, j, 0)),
                  pl.BlockSpec((1, nh, lc), lambda i, j: (i, 0, j)),
                  pl.BlockSpec((1, CONV_HIST, CONV_DIM), lambda i, j: (i, 0, 0)),
                  pl.BlockSpec((1, D_STATE, D_INNER), lambda i, j: (i, 0, 0)),
                  const((CONV_W, CONV_DIM)), const((1, CONV_DIM)),
                  const((1, nh)), const((nh, 1)), const((1, nh)), const((nh, 1)),
                  const((1, D_INNER)), const((1, D_INNER))],
        out_specs=[pl.BlockSpec((1, lc, D_INNER), lambda i, j: (i, j, 0)),
                   pl.BlockSpec((1, D_STATE, D_INNER), lambda i, j: (i, 0, 0))],
        out_shape=[jax.ShapeDtypeStruct((b, l, D_INNER), BF16),
                   jax.ShapeDtypeStruct((b, D_STATE, D_INNER), F32)],
        scratch_shapes=[pltpu.VMEM((lc + CONV_HIST, CONV_DIM), F32), pltpu.VMEM((D_STATE, D_INNER), F32)],
        compiler_params=_params(("parallel", "arbitrary")),
    )(xbc, z, dt_raw, dtt, past, h0t, conv_w, conv_b.reshape(1, -1),
      dt_bias.reshape(1, nh), dt_bias.reshape(nh, 1), a_log.reshape(1, nh), a_log.reshape(nh, 1),
      jnp.repeat(d_skip, HEAD_DIM_S).reshape(1, -1), norm_w.reshape(1, -1))
    return o, hout


def _outproj_kernel(x_ref, oa_ref, os_ref, gt_ref, sc_ref, sh_ref, g_ref, wa_ref, ws_ref, rw_ref, rb_ref,
                    x1_ref, h2_ref, lg_ref):
    nb, tl, d = x_ref.shape
    oa = oa_ref[...].reshape(nb * tl, -1)
    os_ = os_ref[...].reshape(nb * tl, -1)
    mix = jnp.dot(oa, wa_ref[...], preferred_element_type=F32) + jnp.dot(os_, ws_ref[...], preferred_element_type=F32)
    x1 = x_ref[...] + gt_ref[...] * mix.reshape(nb, tl, d)
    x1_ref[...] = x1
    ms = jnp.mean(x1 * x1, axis=-1, keepdims=True)
    h2 = x1 * lax.rsqrt(ms + EPS) * g_ref[...]
    h2 = h2 * (1.0 + sc_ref[...]) + sh_ref[...]
    h2 = h2.reshape(nb * tl, d)
    _store_row_tiles(h2_ref, h2)
    h_hi, h_lo = _split2(h2)
    w_hi, w_lo = rw_ref[0], rw_ref[1]
    lg = (jnp.dot(h_hi, w_hi, preferred_element_type=F32) + jnp.dot(h_lo, w_hi, preferred_element_type=F32)
          + jnp.dot(h_hi, w_lo, preferred_element_type=F32)) + rb_ref[...]
    lg_ref[...] = lg.reshape(nb, tl, -1)


def _outproj(x, oa, os_, gt, sc, sh, g, wa, ws, rw, rb):
    b, l, d = x.shape
    nb, tl = _row_blocking(b, l, ROW_TILE)
    row = lambda n: pl.BlockSpec((nb, tl, n), lambda i, j: (i, j, 0))
    mod = pl.BlockSpec((nb, 1, d), lambda i, j: (i, 0, 0))
    const = lambda a: pl.BlockSpec(a.shape, lambda i, j: tuple(0 for _ in a.shape))
    g3 = g.reshape(1, 1, d)
    return pl.pallas_call(
        _outproj_kernel,
        grid=(b // nb, l // tl),
        in_specs=[row(d), row(D_ATTN), row(D_INNER), mod, mod, mod, const(g3), const(wa), const(ws),
                  const(rw), const(rb)],
        out_specs=[row(d), pl.BlockSpec((nb * tl * ROW_TILE_SUBLANES, LANES), lambda i, j: (i * (l // tl) + j, 0)),
                   row(LANES)],
        out_shape=[jax.ShapeDtypeStruct((b, l, d), F32), jax.ShapeDtypeStruct((b * l * ROW_TILE_SUBLANES, LANES), F32),
                   jax.ShapeDtypeStruct((b, l, LANES), F32)],
        compiler_params=_params(("parallel", "parallel")),
    )(x, oa, os_, gt.reshape(b, 1, d), sc.reshape(b, 1, d), sh.reshape(b, 1, d), g3, wa, ws, rw, rb)


def _route_kernel(lg_ref, idx_ref, gate_ref, cnt_ref, carry_scr):
    @pl.when(pl.program_id(0) == 0)
    def _():
        carry_scr[...] = jnp.zeros(carry_scr.shape, F32)

    v = lg_ref[...]
    tr = v.shape[0]
    lane = lax.broadcasted_iota(jnp.int32, (tr, LANES), 1)
    lane_f = lane.astype(F32)
    tops, idxs = [], []
    onehot = jnp.zeros((tr, LANES), F32)
    for _ in range(TOP_K):
        m = jnp.max(v, axis=-1, keepdims=True)
        idx = jnp.min(jnp.where(v == m, lane_f, float(LANES)), axis=-1, keepdims=True)
        hit = lane_f == idx
        v = jnp.where(hit, -jnp.inf, v)
        onehot = onehot + hit.astype(F32)
        tops.append(m)
        idxs.append(idx)
    es = [jnp.exp(t - tops[0]) for t in tops]
    denom = es[0] + es[1] + es[2] + es[3]

    row = lax.broadcasted_iota(jnp.int32, (tr, tr), 0)
    col = lax.broadcasted_iota(jnp.int32, (tr, tr), 1)
    before = jnp.dot((col < row).astype(BF16), onehot.astype(BF16), preferred_element_type=F32) + carry_scr[...]

    idx_out = jnp.zeros((tr, LANES), F32)
    gate_out = jnp.zeros((tr, LANES), F32)
    for k in range(TOP_K):
        rank = jnp.sum(jnp.where(lane_f == idxs[k], before, 0.0), axis=-1, keepdims=True)
        idx_out = jnp.where(lane == k, idxs[k], idx_out)
        idx_out = jnp.where(lane == TOP_K + k, rank, idx_out)
        gate_out = jnp.where(lane == k, es[k] / denom, gate_out)
    idx_ref[...] = idx_out.astype(jnp.int32)
    gate_ref[...] = gate_out
    carry_scr[...] = carry_scr[...] + jnp.sum(onehot, axis=0, keepdims=True)
    cnt_ref[...] = carry_scr[...]


def _route(logits):
    t, n = logits.shape
    tr = min(ROUTE_TILE, t)
    assert t % tr == 0
    return pl.pallas_call(
        _route_kernel,
        grid=(t // tr,),
        in_specs=[pl.BlockSpec((tr, n), lambda i: (i, 0))],
        out_specs=[pl.BlockSpec((tr, n), lambda i: (i, 0)), pl.BlockSpec((tr, n), lambda i: (i, 0)),
                   pl.BlockSpec((1, n), lambda i: (0, 0))],
        out_shape=[jax.ShapeDtypeStruct((t, n), jnp.int32), jax.ShapeDtypeStruct((t, n), F32),
                   jax.ShapeDtypeStruct((1, n), F32)],
        scratch_shapes=[pltpu.VMEM((1, n), F32)],
        compiler_params=_params(("arbitrary",)),
    )(logits)


def _row_tile(ref, r):
    start = r * ROW_TILE_SUBLANES
    if not isinstance(r, int):
        start = pl.multiple_of(start, ROW_TILE_SUBLANES)
    return ref.at[pl.ds(start, ROW_TILE_SUBLANES), :]


def _store_row_tiles(ref, x, tile0=0):
    rows = x.shape[0]
    for s in range(x.shape[1] // LANES):
        ref[pl.ds(tile0 + s, rows, stride=ROW_TILE_SUBLANES), :] = x[:, s * LANES:(s + 1) * LANES]


def _load_row_tiles(ref, row0, rows):
    return jnp.concatenate(
        [ref[pl.ds(row0 * ROW_TILE_SUBLANES + s, rows, stride=ROW_TILE_SUBLANES), :] for s in range(ROW_TILE_SUBLANES)],
        axis=1)


def _gather_rows(idx_ref, src_hbm, dst, sem, n_rows):
    group = 8
    assert n_rows % group == 0

    def body(g, carry):
        for u in range(group):
            r = g * group + u
            pltpu.make_async_copy(_row_tile(src_hbm, idx_ref[0, 0, r]), _row_tile(dst, r), sem).start(priority=u % 2)
        return carry
    lax.fori_loop(0, n_rows // group, body, 0)


def _wait_rows(src_hbm, dst, sem, n_rows):
    pltpu.make_async_copy(src_hbm.at[pl.ds(0, n_rows * ROW_TILE_SUBLANES), :], dst, sem).wait()


EXPERT_COL_CHUNK = 256


GATHER_BUFS = 3


def _dispatch_kernel(ends_ref, pad_ref, dest_ref, h_hbm, xs_hbm, zbuf, hbuf, zsem, fsem, sem, *, n_slots):
    i = pl.program_id(0)
    n = pl.num_programs(0)
    m = zbuf.shape[0] // ROW_TILE_SUBLANES
    rows = dest_ref.shape[2]
    td = rows // TOP_K
    slot = lax.rem(i, GATHER_BUFS)

    def fetch(step, s):
        start = pl.multiple_of(step * (td * ROW_TILE_SUBLANES), td * ROW_TILE_SUBLANES)
        return pltpu.make_async_copy(h_hbm.at[pl.ds(start, td * ROW_TILE_SUBLANES), :], hbuf.at[s], fsem.at[s])

    def wait_rows_of(s):
        for _ in range(TOP_K):
            pltpu.make_async_copy(hbuf.at[s], xs_hbm.at[pl.ds(0, td * ROW_TILE_SUBLANES), :], sem.at[s]).wait()

    def zero_block(first_slot):
        start = pl.multiple_of(first_slot * ROW_TILE_SUBLANES, ROW_TILE_SUBLANES)
        return pltpu.make_async_copy(zbuf, xs_hbm.at[pl.ds(start, zbuf.shape[0]), :], zsem)

    @pl.when(i == 0)
    def _():
        zbuf[...] = jnp.zeros(zbuf.shape, zbuf.dtype)
        jobs = [(pad_ref[e] > 0, ends_ref[e] - m) for e in range(N_EXPERTS)]
        jobs += [(ends_ref[N_EXPERTS - 1] + j * m < n_slots, ends_ref[N_EXPERTS - 1] + j * m)
                 for j in range(N_EXPERTS + 1)]
        for cond, first in jobs:
            @pl.when(cond)
            def _():
                zero_block(first).start()
        for cond, first in jobs:
            @pl.when(cond)
            def _():
                zero_block(first).wait()
        fetch(0, 0).start()

    @pl.when(i >= 2)
    def _():
        wait_rows_of(lax.rem(i + 1, GATHER_BUFS))

    @pl.when(i + 1 < n)
    def _():
        fetch(i + 1, lax.rem(i + 1, GATHER_BUFS)).start()

    fetch(i, slot).wait()
    for j in range(rows):
        pltpu.make_async_copy(_row_tile(hbuf.at[slot], j // TOP_K), _row_tile(xs_hbm, dest_ref[0, 0, j]),
                              sem.at[slot]).start(priority=j % 2)

    @pl.when(i == n - 1)
    def _():
        @pl.when(i >= 1)
        def _():
            wait_rows_of(lax.rem(i + 2, GATHER_BUFS))
        wait_rows_of(slot)


def _dispatch(dest, ends_p, padded, h2, m, n_slots):
    t = dest.shape[0]
    td = min(DISPATCH_TILE, t)
    assert t % td == 0
    n_steps = t // td
    grid_spec = pltpu.PrefetchScalarGridSpec(
        num_scalar_prefetch=2,
        grid=(n_steps,),
        in_specs=[pl.BlockSpec((1, 1, TOP_K * td), lambda i, e, p: (i, 0, 0), memory_space=pltpu.SMEM),
                  pl.BlockSpec(memory_space=pl.ANY)],
        out_specs=pl.BlockSpec(memory_space=pl.ANY),
        scratch_shapes=[pltpu.VMEM((m * ROW_TILE_SUBLANES, LANES), F32),
                        pltpu.VMEM((GATHER_BUFS, td * ROW_TILE_SUBLANES, LANES), F32),
                        pltpu.SemaphoreType.DMA(()), pltpu.SemaphoreType.DMA((GATHER_BUFS,)),
                        pltpu.SemaphoreType.DMA((GATHER_BUFS,))],
    )
    return pl.pallas_call(
        functools.partial(_dispatch_kernel, n_slots=n_slots),
        grid_spec=grid_spec,
        out_shape=jax.ShapeDtypeStruct((n_slots * ROW_TILE_SUBLANES, LANES), F32),
        compiler_params=_params(("arbitrary",)),
    )(ends_p, padded, dest.reshape(n_steps, 1, TOP_K * td), h2)


def _experts_kernel(be_ref, x_ref, wgu_ref, bgu_ref, wdn_ref, bdn_ref, o_ref):
    m = x_ref.shape[0] // ROW_TILE_SUBLANES
    cw = EXPERT_COL_CHUNK
    tiles_per_piece = cw // LANES
    n_ff, n_out = D_FF // cw, ROW_TILE_SUBLANES // tiles_per_piece
    x = _load_row_tiles(x_ref, 0, m).astype(BF16)
    acts = []
    for c in range(n_ff):
        g = jnp.dot(x, wgu_ref[0, :, c * cw:(c + 1) * cw], preferred_element_type=F32) + bgu_ref[0, :, c * cw:(c + 1) * cw]
        u = (jnp.dot(x, wgu_ref[0, :, D_FF + c * cw:D_FF + (c + 1) * cw], preferred_element_type=F32)
             + bgu_ref[0, :, D_FF + c * cw:D_FF + (c + 1) * cw])
        gate = jnp.minimum(g, SWIGLU_LIMIT)
        up = jnp.clip(u, -SWIGLU_LIMIT, SWIGLU_LIMIT)
        acts.append(((up + 1.0) * gate * jax.nn.sigmoid(SWIGLU_ALPHA * gate)).astype(BF16))
    act = jnp.concatenate(acts, axis=1)
    for c in range(n_out):
        y = jnp.dot(act, wdn_ref[0, :, c * cw:(c + 1) * cw], preferred_element_type=F32) + bdn_ref[0, :, c * cw:(c + 1) * cw]
        _store_row_tiles(o_ref, y, tile0=c * tiles_per_piece)


def _experts(block_e, xs, wgu, bgu, wdn, bdn, m):
    n_blocks = block_e.shape[0]
    d = D_MODEL
    mt = m * ROW_TILE_SUBLANES
    grid_spec = pltpu.PrefetchScalarGridSpec(
        num_scalar_prefetch=1,
        grid=(n_blocks,),
        in_specs=[pl.BlockSpec((mt, LANES), lambda i, be: (i, 0)),
                  pl.BlockSpec((1, d, 2 * D_FF), lambda i, be: (be[i], 0, 0)),
                  pl.BlockSpec((1, 1, 2 * D_FF), lambda i, be: (be[i], 0, 0)),
                  pl.BlockSpec((1, D_FF, d), lambda i, be: (be[i], 0, 0)),
                  pl.BlockSpec((1, 1, d), lambda i, be: (be[i], 0, 0))],
        out_specs=pl.BlockSpec((mt, LANES), lambda i, be: (i, 0)),
    )
    return pl.pallas_call(
        _experts_kernel,
        grid_spec=grid_spec,
        out_shape=jax.ShapeDtypeStruct((n_blocks * mt, LANES), F32),
        compiler_params=_params(("arbitrary",)),
    )(block_e, xs, wgu, bgu.reshape(N_EXPERTS, 1, -1), wdn, bdn.reshape(N_EXPERTS, 1, -1))


def _combine_kernel(idx0_ref, idx1_ref, idxn_ref, y_hbm, x1_ref, gate_ref, gt_ref, g_ref, o_ref, ybuf, sem):
    i = pl.program_id(0)
    n = pl.num_programs(0)
    nb, tl, d = x1_ref.shape
    tc = nb * tl
    slot = lax.rem(i, GATHER_BUFS)
    nxt = lax.rem(i + 2, GATHER_BUFS)

    @pl.when(i == 0)
    def _():
        _gather_rows(idx0_ref, y_hbm, ybuf.at[0], sem.at[0], TOP_K * tc)
        _gather_rows(idx1_ref, y_hbm, ybuf.at[1], sem.at[1], TOP_K * tc)

    _wait_rows(y_hbm, ybuf.at[slot], sem.at[slot], TOP_K * tc)

    def prefetch(k):
        for r in range(k * tc, (k + 1) * tc):
            pltpu.make_async_copy(_row_tile(y_hbm, idxn_ref[0, 0, r]), _row_tile(ybuf.at[nxt], r),
                                  sem.at[nxt]).start(priority=r % 2)

    gates = gate_ref[...]
    y = gates[:, 0:1] * _load_row_tiles(ybuf.at[slot], 0, tc)
    prefetch(0)
    for k in range(1, TOP_K):
        y = y + gates[:, k:k + 1] * _load_row_tiles(ybuf.at[slot], k * tc, tc)
        prefetch(k)
    x2 = x1_ref[...] + gt_ref[...] * y.reshape(nb, tl, d)
    ms = jnp.mean(x2 * x2, axis=-1, keepdims=True)
    o_ref[...] = x2 * lax.rsqrt(ms + EPS) * g_ref[...]

    @pl.when(i == n - 1)
    def _():
        for ahead in (1, 2):
            s = lax.rem(i + ahead, GATHER_BUFS)
            _wait_rows(y_hbm, ybuf.at[s], sem.at[s], TOP_K * tc)


def _combine(dest, y_slots, x1, gates, gt, g_final):
    b, l, d = x1.shape
    nb, tl = _row_blocking(b, l, COMBINE_TILE)
    tc = nb * tl
    t = b * l
    n_steps = t // tc
    idx = jnp.swapaxes(dest.reshape(n_steps, tc, TOP_K), 1, 2).reshape(n_steps, 1, TOP_K * tc)
    steps_per_batch_row = l // tl
    tok = lambda i: (i // steps_per_batch_row, i % steps_per_batch_row, 0)
    smem = lambda f: pl.BlockSpec((1, 1, TOP_K * tc), f, memory_space=pltpu.SMEM)
    return pl.pallas_call(
        _combine_kernel,
        grid=(n_steps,),
        in_specs=[smem(lambda i: (0, 0, 0)),
                  smem(lambda i: (min(1, n_steps - 1), 0, 0)),
                  smem(lambda i: (jnp.minimum(i + 2, n_steps - 1), 0, 0)),
                  pl.BlockSpec(memory_space=pl.ANY),
                  pl.BlockSpec((nb, tl, d), tok),
                  pl.BlockSpec((tc, LANES), lambda i: (i, 0)),
                  pl.BlockSpec((nb, 1, d), lambda i: (i // steps_per_batch_row, 0, 0)),
                  pl.BlockSpec((1, 1, d), lambda i: (0, 0, 0))],
        out_specs=pl.BlockSpec((nb, tl, d), tok),
        out_shape=jax.ShapeDtypeStruct((b, l, d), F32),
        scratch_shapes=[pltpu.VMEM((GATHER_BUFS, TOP_K * tc * ROW_TILE_SUBLANES, LANES), F32),
                        pltpu.SemaphoreType.DMA((GATHER_BUFS,))],
        compiler_params=_params(("arbitrary",)),
    )(idx, idx, idx, y_slots, x1, gates, gt.reshape(b, 1, d), g_final.reshape(1, 1, d))


def _moe(h2, logits, x1, gt_m, g_final, wts):
    b, l, d = x1.shape
    t = b * l
    m = MOE_ROWS if t * TOP_K >= 4 * N_EXPERTS * MOE_ROWS else MOE_ROWS_SMALL
    idx, gates, counts = _route(logits.reshape(t, LANES))
    top_e = idx[:, :TOP_K]
    rank = idx[:, TOP_K:2 * TOP_K]
    counts = counts[0, :N_EXPERTS].astype(jnp.int32)
    padded = (counts + m - 1) // m * m
    ends_p = jnp.cumsum(padded)
    starts_p = ends_p - padded
    dest = starts_p[top_e] + rank
    n_blocks = (t * TOP_K + N_EXPERTS * (m - 1) + m - 1) // m
    block_start = jnp.arange(n_blocks, dtype=jnp.int32)[:, None] * m
    block_e = jnp.minimum(jnp.sum((ends_p[None, :] <= block_start).astype(jnp.int32), axis=1), N_EXPERTS - 1)
    xs = _dispatch(dest, ends_p, padded, h2, m, n_blocks * m)
    y_slots = _experts(block_e, xs, *wts, m)
    return _combine(dest, y_slots, x1, gates, gt_m, g_final)


def _layer(x, mod, k_past, v_past, logf_past, conv_past, ssm_past, p, g_final):
    b, l, d = x.shape
    sh_a, sc_a, gt_a, sh_m, sc_m, gt_m = jnp.split(mod, 6, axis=-1)
    q, kf, vf, kb, vb, z, xbc, sm = _inproj(x, p['g_mix'], sc_a, sh_a, *p['w_in'])
    dt_raw = sm[:, :, N_HEADS_A:N_HEADS_A + N_HEADS_S]

    if k_past is None:
        logf, qx, kx = _cumsum_bias(sm, p['b_f'])
        o_a = _attention(q, qx, kb, kx, vb)
        conv_past = jnp.zeros((b, CONV_W - 1, CONV_DIM), F32)
        h0t = jnp.zeros((b, D_STATE, D_INNER), F32)
    else:
        lp = k_past.shape[1]
        past_t = jnp.swapaxes(logf_past, 1, 2).reshape(b * N_HEADS_A, lp)
        cumt_p = _cumsum_lanes(past_t).reshape(b, N_HEADS_A, lp)
        carry = jnp.pad(cumt_p[:, :, lp - 1], ((0, 0), (0, LANES - N_HEADS_A))).reshape(b, 1, LANES)
        lpad = -(-l // LANES) * LANES
        sm_pad = jnp.pad(sm, ((0, 0), (0, lpad - l), (0, 0)))
        logf, cum_n, cumt_n = _cumsum(sm_pad, p['b_f'], carry)
        logf = logf[:, :l]
        o_a = _attention_cached(q, kb, vb, k_past, v_past, cum_n, cumt_p, cumt_n)
        h0t = jnp.swapaxes(ssm_past.reshape(b, D_INNER, D_STATE), 1, 2)
    o_s, hout = _ssd(xbc, z, dt_raw, conv_past, h0t, p['conv_w'], p['conv_b'], p['dt_bias'], p['a_log'],
                     p['d_skip'], p['ssd_norm_w'])
    x1, h2, logits = _outproj(x, o_a, o_s, gt_a, sc_m, sh_m, p['g_ffn'], p['w_out_a'], p['w_out_s'],
                              p['router_w'], p['router_b'])
    y = _moe(h2, logits, x1, gt_m, g_final, p['experts'])

    assert l >= CONV_W - 1
    conv_new = xbc[:, l - (CONV_W - 1):]
    ssm_new = jnp.swapaxes(hout, 1, 2).reshape(b, N_HEADS_S, HEAD_DIM_S, D_STATE)
    return (y, kf.reshape(b, l, N_HEADS_A, HEAD_DIM_A), vf.reshape(b, l, N_HEADS_A, HEAD_DIM_A),
            logf[:, :, :N_HEADS_A], conv_new, ssm_new)


def kernel(x_prompt, x_sample, cache_k, cache_v, cache_logf, state_conv, state_ssm, c_prompt, c_sample, w_ada, b_ada, g_mix, w_in, b_f, conv_w, conv_b, dt_bias, a_log, d_skip, ssd_norm_w, w_out, g_ffn, router_w, router_b, w_gate_up, b_gate_up, w_down, b_down, g_final):
    assert w_ada.shape[0] == 1, "single-layer operation"
    bp = x_prompt.shape[0]
    w = w_in[0]
    q_end, k_end, v_end = D_ATTN, 2 * D_ATTN, 3 * D_ATTN
    f_end = v_end + N_HEADS_A
    z_end = f_end + D_INNER
    xbc_end = z_end + CONV_DIM
    w_small = jnp.concatenate(
        [w[:, v_end:f_end], w[:, xbc_end:], jnp.zeros((D_MODEL, LANES - N_HEADS_A - N_HEADS_S), F32)], axis=1)
    cast = lambda a: a.astype(BF16)
    p = {
        'g_mix': g_mix[0],
        'w_in': (cast(w[:, :q_end]), cast(w[:, q_end:k_end]), cast(w[:, k_end:v_end]), cast(w[:, f_end:z_end]),
                 cast(w[:, z_end:xbc_end]), cast(w_small)),
        'b_f': jnp.pad(b_f[0], (0, LANES - N_HEADS_A)).reshape(1, LANES),
        'conv_w': conv_w[0], 'conv_b': conv_b[0], 'dt_bias': dt_bias[0], 'a_log': a_log[0],
        'd_skip': d_skip[0], 'ssd_norm_w': ssd_norm_w[0],
        'w_out_a': cast(w_out[0, :D_ATTN]), 'w_out_s': cast(w_out[0, D_ATTN:]),
        'g_ffn': g_ffn[0],
        'router_w': jnp.stack(_split2(jnp.pad(router_w[0], ((0, 0), (0, LANES - N_EXPERTS))))),
        'router_b': jnp.pad(router_b[0], (0, LANES - N_EXPERTS), constant_values=-jnp.inf).reshape(1, LANES),
        'experts': (cast(w_gate_up[0]), b_gate_up[0], cast(w_down[0]), b_down[0]),
    }
    mod = _adaln(jnp.concatenate([c_prompt, c_sample], axis=0), w_ada[0], b_ada[0])
    outs_p = _layer(x_prompt, mod[:bp], None, None, None, None, None, p, g_final)
    outs_s = _layer(x_sample, mod[bp:], cache_k[0], cache_v[0], cache_logf[0], state_conv[0], state_ssm[0], p, g_final)
    stack = lambda a: a[None]
    return (outs_p[0], outs_s[0]) + tuple(stack(a) for a in outs_p[1:]) + tuple(stack(a) for a in outs_s[1:])
```

```python
import functools
import math

import jax
import jax.numpy as jnp
from jax import lax
from jax.experimental import pallas as pl
from jax.experimental.pallas import tpu as pltpu

F32 = jnp.float32
BF16 = jnp.bfloat16
HIGHEST = lax.Precision.HIGHEST

D_MODEL = 1024
N_HEADS_A = 8
HEAD_DIM_A = 128
D_ATTN = N_HEADS_A * HEAD_DIM_A
D_INNER = 1024
HEAD_DIM_S = 64
N_HEADS_S = D_INNER // HEAD_DIM_S
N_GROUPS_S = 2
D_STATE = 128
CONV_W = 4
CONV_DIM = D_INNER + 2 * N_GROUPS_S * D_STATE
N_EXPERTS = 32
TOP_K = 4
D_FF = 1024
SWIGLU_LIMIT = 7.0
SWIGLU_ALPHA = 1.702
EPS = 1e-5
LANES = 128
SUBLANES = 8
ROW_TILE_SUBLANES = D_MODEL // LANES
assert ROW_TILE_SUBLANES == SUBLANES
VMEM_LIMIT = 56 * 1024 * 1024
LOG2E = 1.4426950408889634
ATT_HEADS_PER_STEP = 4

ROW_TILE = 512
ATT_TILE = 512
DEC_KV_TILE = 512
CUM_TILE = 512
SSD_CHUNK = 128
ROUTE_TILE = 256
MOE_ROWS = 512
MOE_ROWS_SMALL = 128
COMBINE_TILE = 256
DISPATCH_TILE = 256

def _params(sem, vmem=VMEM_LIMIT):
    return pltpu.CompilerParams(dimension_semantics=sem, vmem_limit_bytes=vmem)


def _row_blocking(b, l, tile):
    if l >= tile:
        assert l % tile == 0
        return 1, tile
    nb = min(b, max(1, tile // l))
    while b % nb:
        nb -= 1
    return nb, l


def _split3(x):
    hi = x.astype(BF16)
    r1 = x - hi.astype(F32)
    mid = r1.astype(BF16)
    lo = (r1 - mid.astype(F32)).astype(BF16)
    return hi, mid, lo


def _split2(x):
    hi = x.astype(BF16)
    return hi, (x - hi.astype(F32)).astype(BF16)


def _tri_dot(tri, x):
    hi, mid, lo = _split3(x)
    d = lambda p: jnp.dot(tri, p, preferred_element_type=F32)
    return d(hi) + d(mid) + d(lo)


def _dot_tri(x, tri):
    hi, mid, lo = _split3(x)
    d = lambda p: jnp.dot(p, tri, preferred_element_type=F32)
    return d(hi) + d(mid) + d(lo)


def _silu(x):
    return x * jax.nn.sigmoid(x)


def _softplus(x):
    return jnp.maximum(x, 0.0) + jnp.log1p(jnp.exp(-jnp.abs(x)))


def _log_sigmoid(x):
    return jnp.minimum(x, 0.0) - jnp.log1p(jnp.exp(-jnp.abs(x)))


def _adaln_kernel(c_ref, w_ref, b_ref, o_ref):
    s = _silu(c_ref[...])
    o_ref[...] = jnp.dot(s, w_ref[...], precision=HIGHEST, preferred_element_type=F32) + b_ref[...]


def _adaln(c, w, b):
    m, d = c.shape
    n = w.shape[1]
    tn = 512
    return pl.pallas_call(
        _adaln_kernel,
        grid=(n // tn,),
        in_specs=[pl.BlockSpec((m, d), lambda j: (0, 0)),
                  pl.BlockSpec((d, tn), lambda j: (0, j)),
                  pl.BlockSpec((1, tn), lambda j: (0, j))],
        out_specs=pl.BlockSpec((m, tn), lambda j: (0, j)),
        out_shape=jax.ShapeDtypeStruct((m, n), F32),
        compiler_params=_params(("parallel",)),
    )(c, w, b.reshape(1, n))


def _inproj_kernel(x_ref, g_ref, sc_ref, sh_ref, wq_ref, wk_ref, wv_ref, wz_ref, wx_ref, ws_ref,
                   q_ref, kf_ref, vf_ref, kb_ref, vb_ref, z_ref, xbc_ref, sm_ref):
    x = x_ref[...]
    nb, tl, d = x.shape
    ms = jnp.mean(x * x, axis=-1, keepdims=True)
    h = x * lax.rsqrt(ms + EPS) * g_ref[...]
    h = h * (1.0 + sc_ref[...]) + sh_ref[...]
    hb = h.reshape(nb * tl, d).astype(BF16)

    def mm(w_ref):
        return jnp.dot(hb, w_ref[...], preferred_element_type=F32)

    q = mm(wq_ref) * (LOG2E / math.sqrt(HEAD_DIM_A))
    q_ref[...] = q.astype(BF16).reshape(nb, tl, -1)
    k = mm(wk_ref)
    kf_ref[...] = k.reshape(nb, tl, -1)
    kb_ref[...] = k.astype(BF16).reshape(nb, tl, -1)
    v = mm(wv_ref)
    vf_ref[...] = v.reshape(nb, tl, -1)
    vb_ref[...] = v.astype(BF16).reshape(nb, tl, -1)
    z_ref[...] = mm(wz_ref).reshape(nb, tl, -1)
    xbc_ref[...] = mm(wx_ref).reshape(nb, tl, -1)
    sm_ref[...] = mm(ws_ref).reshape(nb, tl, -1)


def _inproj(x, g, sc, sh, wq, wk, wv, wz, wx, ws):
    b, l, d = x.shape
    nb, tl = _row_blocking(b, l, ROW_TILE)
    grid = (b // nb, l // tl)
    row = lambda n: pl.BlockSpec((nb, tl, n), lambda i, j: (i, j, 0))
    mod = pl.BlockSpec((nb, 1, d), lambda i, j: (i, 0, 0))
    wspec = lambda w: pl.BlockSpec(w.shape, lambda i, j: (0, 0), pipeline_mode=pl.Buffered(1))
    outs = [(D_ATTN, BF16), (D_ATTN, F32), (D_ATTN, F32), (D_ATTN, BF16), (D_ATTN, BF16),
            (D_INNER, F32), (CONV_DIM, F32), (LANES, F32)]
    return pl.pallas_call(
        _inproj_kernel,
        grid=grid,
        in_specs=[row(d), pl.BlockSpec((1, 1, d), lambda i, j: (0, 0, 0)), mod, mod,
                  wspec(wq), wspec(wk), wspec(wv), wspec(wz), wspec(wx), wspec(ws)],
        out_specs=[row(n) for n, _ in outs],
        out_shape=[jax.ShapeDtypeStruct((b, l, n), dt) for n, dt in outs],
        compiler_params=_params(("parallel", "parallel")),
    )(x, g.reshape(1, 1, d), sc.reshape(b, 1, d), sh.reshape(b, 1, d), wq, wk, wv, wz, wx, ws)


def _running_sum(v_ref, bias_ref, carry_ref, carry_scr):
    @pl.when(pl.program_id(1) == 0)
    def _():
        carry_scr[...] = carry_ref[0]

    v = v_ref[0]
    tl = v.shape[0]
    lf = _log_sigmoid(v + bias_ref[...])
    row = lax.broadcasted_iota(jnp.int32, (tl, tl), 0)
    col = lax.broadcasted_iota(jnp.int32, (tl, tl), 1)
    tri = (row >= col).astype(BF16)
    cs = _tri_dot(tri, lf) + carry_scr[...]
    carry_scr[...] = cs[tl - 1:tl, :]
    return lf, cs


def _cumsum_kernel(v_ref, bias_ref, carry_ref, logf_ref, cum_ref, cumt_ref, carry_scr):
    lf, cs = _running_sum(v_ref, bias_ref, carry_ref, carry_scr)
    logf_ref[0] = lf
    cum_ref[0] = cs
    cumt_ref[0] = cs.T[:N_HEADS_A, :]


def _cumsum(vals, bias, carry):
    b, l, n = vals.shape
    tl = min(CUM_TILE, l)
    assert l % tl == 0 and tl % LANES == 0
    return pl.pallas_call(
        _cumsum_kernel,
        grid=(b, l // tl),
        in_specs=[pl.BlockSpec((1, tl, n), lambda i, j: (i, j, 0)),
                  pl.BlockSpec((1, n), lambda i, j: (0, 0)),
                  pl.BlockSpec((1, 1, n), lambda i, j: (i, 0, 0))],
        out_specs=[pl.BlockSpec((1, tl, n), lambda i, j: (i, j, 0)),
                   pl.BlockSpec((1, tl, n), lambda i, j: (i, j, 0)),
                   pl.BlockSpec((1, N_HEADS_A, tl), lambda i, j: (i, 0, j))],
        out_shape=[jax.ShapeDtypeStruct((b, l, n), F32),
                   jax.ShapeDtypeStruct((b, l, n), F32),
                   jax.ShapeDtypeStruct((b, N_HEADS_A, l), F32)],
        scratch_shapes=[pltpu.VMEM((1, n), F32)],
        compiler_params=_params(("parallel", "arbitrary")),
    )(vals, bias, carry)


def _cumsum_lanes_kernel(x_ref, o_ref, carry_scr):
    @pl.when(pl.program_id(0) == 0)
    def _():
        carry_scr[...] = jnp.zeros(carry_scr.shape, F32)

    x = x_ref[...]
    tl = x.shape[1]
    row = lax.broadcasted_iota(jnp.int32, (tl, tl), 0)
    col = lax.broadcasted_iota(jnp.int32, (tl, tl), 1)
    cs = _dot_tri(x, (row <= col).astype(BF16)) + carry_scr[...]
    o_ref[...] = cs
    carry_scr[...] = cs[:, tl - 1:tl]


def _cumsum_lanes(x):
    rows, l = x.shape
    tl = min(CUM_TILE, l)
    assert l % tl == 0 and rows % SUBLANES == 0
    return pl.pallas_call(
        _cumsum_lanes_kernel,
        grid=(l // tl,),
        in_specs=[pl.BlockSpec((rows, tl), lambda j: (0, j))],
        out_specs=pl.BlockSpec((rows, tl), lambda j: (0, j)),
        out_shape=jax.ShapeDtypeStruct((rows, l), F32),
        scratch_shapes=[pltpu.VMEM((rows, 1), F32)],
        compiler_params=_params(("arbitrary",)),
    )(x)


N_BIAS_TERMS = 3


def _bias_placement():
    rows = jnp.arange(N_BIAS_TERMS * LANES)
    n, h = rows // LANES, rows % LANES
    col = jnp.arange(D_ATTN)[None, :]
    valid = (h < N_HEADS_A)[:, None]
    eq = jnp.where(valid & (col == (h * LANES + n)[:, None]), 1.0, 0.0)
    ek = jnp.where(valid & (col == (h * LANES + N_BIAS_TERMS + n)[:, None]), -1.0, 0.0)
    lane = jnp.arange(D_ATTN) % LANES
    ones_q = jnp.where((lane >= N_BIAS_TERMS) & (lane < 2 * N_BIAS_TERMS), 1.0, 0.0)
    ones_k = jnp.where(lane < N_BIAS_TERMS, 1.0, 0.0)
    return eq.astype(BF16), ek.astype(BF16), ones_q.reshape(1, -1).astype(F32), ones_k.reshape(1, -1).astype(F32)


def _cumsum_bias_kernel(v_ref, bias_ref, carry_ref, eq_ref, ek_ref, oq_ref, ok_ref, logf_ref, qx_ref, kx_ref, carry_scr):
    lf, cs = _running_sum(v_ref, bias_ref, carry_ref, carry_scr)
    logf_ref[0] = lf
    terms = jnp.concatenate(_split3(cs * LOG2E), axis=1)
    qx_ref[0] = (jnp.dot(terms, eq_ref[...], preferred_element_type=F32) + oq_ref[...]).astype(BF16)
    kx_ref[0] = (jnp.dot(terms, ek_ref[...], preferred_element_type=F32) + ok_ref[...]).astype(BF16)


def _cumsum_bias(vals, bias):
    b, l, n = vals.shape
    tl = min(CUM_TILE, l)
    assert l % tl == 0
    row = lambda w: pl.BlockSpec((1, tl, w), lambda i, j: (i, j, 0))
    const = lambda a: pl.BlockSpec(a.shape, lambda i, j: (0, 0))
    placement = _bias_placement()
    return pl.pallas_call(
        _cumsum_bias_kernel,
        grid=(b, l // tl),
        in_specs=[row(n), pl.BlockSpec((1, n), lambda i, j: (0, 0)), pl.BlockSpec((1, 1, n), lambda i, j: (i, 0, 0))]
        + [const(a) for a in placement],
        out_specs=[row(n), row(D_ATTN), row(D_ATTN)],
        out_shape=[jax.ShapeDtypeStruct((b, l, n), F32), jax.ShapeDtypeStruct((b, l, D_ATTN), BF16),
                   jax.ShapeDtypeStruct((b, l, D_ATTN), BF16)],
        scratch_shapes=[pltpu.VMEM((1, n), F32)],
        compiler_params=_params(("parallel", "arbitrary")),
    )(vals, bias, jnp.zeros((b, 1, n), F32), *placement)


_NT = (((1,), (1,)), ((), ()))


def _softmax_step(s, v, m_prev, l_prev, acc_prev):
    m_new = jnp.maximum(m_prev, jnp.max(s, axis=-1, keepdims=True))
    p = jnp.exp2(s - m_new)
    alpha = jnp.exp2(m_prev - m_new)
    l_new = alpha * l_prev + jnp.sum(p, axis=-1, keepdims=True)
    acc_new = alpha * acc_prev + jnp.dot(p.astype(BF16), v, preferred_element_type=F32)
    return m_new, l_new, acc_new


def _attn_kernel(q_ref, qx_ref, k_ref, kx_ref, v_ref, o_ref, m_scr, acc_scr, *, tile, heads):
    i = pl.program_id(2)
    dh = HEAD_DIM_A
    n_chunks = tile // LANES
    ones = jnp.ones((tile, dh), BF16)
    m_scr[...] = jnp.full(m_scr.shape, -jnp.inf, F32)
    acc_scr[...] = jnp.zeros(acc_scr.shape, F32)
    q2 = [jnp.concatenate([q_ref[0, :, hh * dh:(hh + 1) * dh], qx_ref[0, :, hh * dh:(hh + 1) * dh]], axis=1)
          for hh in range(heads)]

    def update(j, mask):
        start = pl.multiple_of(j * tile, tile)
        rows = pl.ds(start, tile)
        for hh in range(heads):
            sl = slice(hh * dh, (hh + 1) * dh)
            k2 = jnp.concatenate([k_ref[0, rows, sl], kx_ref[0, rows, sl]], axis=1)
            v2 = jnp.concatenate([v_ref[0, rows, sl], ones], axis=1)
            s = lax.dot_general(q2[hh], k2, _NT, preferred_element_type=F32)
            if mask is not None:
                s = jnp.where(mask, s, -jnp.inf)
            chunks = [s[:, c * LANES:(c + 1) * LANES] for c in range(n_chunks)]
            cmax = functools.reduce(jnp.maximum, chunks)
            m_prev = m_scr[hh]
            m_new = jnp.maximum(m_prev, jnp.max(cmax, axis=-1, keepdims=True))
            alpha = jnp.exp2(m_prev - m_new)
            p = jnp.concatenate([jnp.exp2(c - m_new) for c in chunks], axis=1).astype(BF16)
            pv = jnp.dot(p, v2, preferred_element_type=F32)
            m_scr[hh] = m_new
            acc_scr[hh, :, :dh] = alpha * acc_scr[hh, :, :dh] + pv[:, :dh]
            acc_scr[hh, :, dh:] = alpha * acc_scr[hh, :, dh:] + pv[:, dh:]

    def body(j, carry):
        update(j, None)
        return carry

    lax.fori_loop(0, i, body, 0)
    row = lax.broadcasted_iota(jnp.int32, (tile, tile), 0)
    col = lax.broadcasted_iota(jnp.int32, (tile, tile), 1)
    update(i, col <= row)
    for hh in range(heads):
        o_ref[0, :, hh * dh:(hh + 1) * dh] = (acc_scr[hh, :, :dh] / acc_scr[hh, :, dh:]).astype(o_ref.dtype)


def _attention(q, qx, k, kx, v):
    b, l, _ = q.shape
    tile = ATT_TILE if l >= 2 * ATT_TILE else LANES
    assert l % tile == 0
    heads = ATT_HEADS_PER_STEP
    w = heads * HEAD_DIM_A
    qspec = pl.BlockSpec((1, tile, w), lambda bi, h, i: (bi, i, h))
    kspec = pl.BlockSpec((1, l, w), lambda bi, h, i: (bi, 0, h))
    return pl.pallas_call(
        functools.partial(_attn_kernel, tile=tile, heads=heads),
        grid=(b, N_HEADS_A // heads, l // tile),
        in_specs=[qspec, qspec, kspec, kspec, kspec],
        out_specs=qspec,
        out_shape=jax.ShapeDtypeStruct((b, l, D_ATTN), BF16),
        scratch_shapes=[pltpu.VMEM((heads, tile, LANES), F32), pltpu.VMEM((heads, tile, 2 * HEAD_DIM_A), F32)],
        compiler_params=_params(("parallel", "parallel", "arbitrary")),
    )(q, qx, k, kx, v)


def _attn_cached_kernel(q_ref, kn_ref, vn_ref, kc_ref, vc_ref, cumn_ref, cumtp_ref, cumtn_ref, o_ref,
                        m_scr, l_scr, acc_scr):
    j = pl.program_id(1)
    ld = q_ref.shape[1]
    dh = HEAD_DIM_A

    @pl.when(j == 0)
    def _():
        m_scr[...] = jnp.full(m_scr.shape, -jnp.inf, F32)
        l_scr[...] = jnp.zeros(l_scr.shape, F32)
        acc_scr[...] = jnp.zeros(acc_scr.shape, F32)

    def head_step(h, kh, vh, ck, mask):
        sl = slice(h * dh, (h + 1) * dh)
        s = lax.dot_general(q_ref[0, :, sl], kh, _NT, preferred_element_type=F32)
        s = s + (cumn_ref[0, :ld, h:h + 1] - ck) * LOG2E
        if mask is not None:
            s = jnp.where(mask, s, -jnp.inf)
        m, l, acc = _softmax_step(s, vh, m_scr[h], l_scr[h], acc_scr[:, sl])
        m_scr[h] = m
        l_scr[h] = l
        acc_scr[:, sl] = acc

    for h in range(N_HEADS_A):
        sl = slice(h * dh, (h + 1) * dh)
        rows = pl.ds(h, kc_ref.shape[1] // N_HEADS_A, stride=N_HEADS_A)
        head_step(h, kc_ref[0, rows, :].astype(BF16), vc_ref[0, rows, :].astype(BF16),
                  cumtp_ref[0, h:h + 1, :], None)

    @pl.when(j == pl.num_programs(1) - 1)
    def _():
        row = lax.broadcasted_iota(jnp.int32, (ld, ld), 0)
        col = lax.broadcasted_iota(jnp.int32, (ld, ld), 1)
        for h in range(N_HEADS_A):
            sl = slice(h * dh, (h + 1) * dh)
            head_step(h, kn_ref[0, :, sl], vn_ref[0, :, sl], cumtn_ref[0, h:h + 1, :ld], col <= row)
            o_ref[0, :, sl] = (acc_scr[:, sl] / l_scr[h]).astype(o_ref.dtype)


def _attention_cached(q, kn, vn, kc, vc, cum_new, cumt_past, cumt_new):
    b, ld, _ = q.shape
    lp = kc.shape[1]
    tk = min(DEC_KV_TILE, lp)
    assert lp % tk == 0
    lpad = cum_new.shape[1]
    new = pl.BlockSpec((1, ld, D_ATTN), lambda bi, j: (bi, 0, 0))
    past = pl.BlockSpec((1, tk * N_HEADS_A, HEAD_DIM_A), lambda bi, j: (bi, j, 0))
    kc = kc.reshape(b, lp * N_HEADS_A, HEAD_DIM_A)
    vc = vc.reshape(b, lp * N_HEADS_A, HEAD_DIM_A)
    return pl.pallas_call(
        _attn_cached_kernel,
        grid=(b, lp // tk),
        in_specs=[new, new, new, past, past,
                  pl.BlockSpec((1, lpad, LANES), lambda bi, j: (bi, 0, 0)),
                  pl.BlockSpec((1, N_HEADS_A, tk), lambda bi, j: (bi, 0, j)),
                  pl.BlockSpec((1, N_HEADS_A, lpad), lambda bi, j: (bi, 0, 0))],
        out_specs=new,
        out_shape=jax.ShapeDtypeStruct((b, ld, D_ATTN), BF16),
        scratch_shapes=[pltpu.VMEM((N_HEADS_A, ld, 1), F32), pltpu.VMEM((N_HEADS_A, ld, 1), F32),
                        pltpu.VMEM((ld, D_ATTN), F32)],
        compiler_params=_params(("parallel", "arbitrary")),
    )(q, kn, vn, kc, vc, cum_new, cumt_past, cumt_new)


def _expand_heads(a):
    r = a.shape[0]
    low = lax.broadcasted_iota(jnp.int32, (r, LANES), 1) < HEAD_DIM_S
    return jnp.concatenate(
        [jnp.where(low, a[:, 2 * j:2 * j + 1], a[:, 2 * j + 1:2 * j + 2]) for j in range(N_HEADS_S // 2)], axis=1)


CONV_HIST = SUBLANES


def _ssd_kernel(xbc_ref, z_ref, dt_ref, dtt_ref, past_ref, h0_ref, cw_ref, cb_ref, dtb_ref, dtbt_ref,
                alog_ref, alogt_ref, dsk_ref, nw_ref, o_ref, hout_ref, xbuf, ht_scr):
    c = pl.program_id(1)
    lc = xbc_ref.shape[1]
    hist = CONV_HIST

    @pl.when(c == 0)
    def _():
        xbuf[0:hist, :] = past_ref[0]
        ht_scr[...] = h0_ref[0]

    xbuf[hist:hist + lc, :] = xbc_ref[0]
    u = cb_ref[...]
    for w in range(CONV_W):
        off = hist - (CONV_W - 1) + w
        u = u + xbuf[off:off + lc, :] * cw_ref[w:w + 1, :]
    xbuf[0:hist, :] = xbuf[lc:lc + hist, :]
    u = _silu(u)
    xs = u[:, :D_INNER]
    gn = N_GROUPS_S * D_STATE
    bm = u[:, D_INNER:D_INNER + gn].astype(BF16)
    cm = u[:, D_INNER + gn:].astype(BF16)

    dt = _softplus(dt_ref[0] + dtb_ref[...])
    dtt = _softplus(dtt_ref[0] + dtbt_ref[...])
    a = dt * (-jnp.exp(alog_ref[...]))
    at = dtt * (-jnp.exp(alogt_ref[...]))
    row = lax.broadcasted_iota(jnp.int32, (lc, lc), 0)
    col = lax.broadcasted_iota(jnp.int32, (lc, lc), 1)
    causal = col <= row
    a_cs = _tri_dot(causal.astype(BF16), a)
    a_cst = _dot_tri(at, (row <= col).astype(BF16))
    total = a_cs[lc - 1:lc, :]
    dt_e = _expand_heads(dt)
    w_e = _expand_heads(dt * jnp.exp(total - a_cs))
    ea_e = _expand_heads(jnp.exp(a_cs))
    cd_e = _expand_heads(jnp.exp(total))
    xdt = xs * dt_e
    xdw = (xs * w_e).astype(BF16)

    cbs = [lax.dot_general(cm[:, g * D_STATE:(g + 1) * D_STATE], bm[:, g * D_STATE:(g + 1) * D_STATE], _NT,
                           preferred_element_type=F32) for g in range(N_GROUPS_S)]
    low = lax.broadcasted_iota(jnp.int32, (lc, LANES), 1) < HEAD_DIM_S
    heads_per_group = N_HEADS_S // N_GROUPS_S
    yd = []
    for j in range(N_HEADS_S // 2):
        ms = []
        for hh in (2 * j, 2 * j + 1):
            seg = a_cs[:, hh:hh + 1] - a_cst[hh:hh + 1, :]
            dec = jnp.where(causal, jnp.exp(jnp.where(causal, seg, 0.0)), 0.0)
            ms.append((cbs[hh // heads_per_group] * dec).astype(BF16))
        xb = xdt[:, j * LANES:(j + 1) * LANES]
        rhs = jnp.concatenate([jnp.where(low, xb, 0.0), jnp.where(low, 0.0, xb)], axis=0).astype(BF16)
        yd.append(jnp.dot(jnp.concatenate(ms, axis=1), rhs, preferred_element_type=F32))
    y = jnp.concatenate(yd, axis=1)

    half = D_INNER // N_GROUPS_S
    ht = ht_scr[...]
    htb = ht.astype(BF16)
    y_off = jnp.concatenate(
        [jnp.dot(cm[:, g * D_STATE:(g + 1) * D_STATE], htb[:, g * half:(g + 1) * half],
                 preferred_element_type=F32) for g in range(N_GROUPS_S)], axis=1)
    st = jnp.concatenate(
        [lax.dot_general(bm[:, g * D_STATE:(g + 1) * D_STATE], xdw[:, g * half:(g + 1) * half],
                         (((0,), (0,)), ((), ())), preferred_element_type=F32) for g in range(N_GROUPS_S)], axis=1)
    ht_new = cd_e * ht + st
    ht_scr[...] = ht_new
    hout_ref[0] = ht_new

    y = y + y_off * ea_e + dsk_ref[...] * xs
    gz = y * _silu(z_ref[0])
    outs = []
    for g in range(N_GROUPS_S):
        gg = gz[:, g * half:(g + 1) * half]
        outs.append(gg * lax.rsqrt(jnp.mean(gg * gg, axis=-1, keepdims=True) + EPS))
    o_ref[0] = (jnp.concatenate(outs, axis=1) * nw_ref[...]).astype(o_ref.dtype)


def _ssd(xbc, z, dt_raw, conv_past, h0t, conv_w, conv_b, dt_bias, a_log, d_skip, norm_w):
    b, l, _ = xbc.shape
    lc = min(SSD_CHUNK, l)
    assert l % lc == 0 and lc % CONV_HIST == 0
    nh = N_HEADS_S
    dtt = jnp.swapaxes(dt_raw, 1, 2)
    past = jnp.pad(conv_past, ((0, 0), (CONV_HIST - (CONV_W - 1), 0), (0, 0)))
    const = lambda shape: pl.BlockSpec(shape, lambda i, j: tuple(0 for _ in shape))
    o, hout = pl.pallas_call(
        _ssd_kernel,
        grid=(b, l // lc),
        in_specs=[pl.BlockSpec((1, lc, CONV_DIM), lambda i, j: (i, j, 0)),
                  pl.BlockSpec((1, lc, D_INNER), lambda i, j: (i, j, 0)),
                  pl.BlockSpec((1, lc, nh), lambda i, j: (i, j, 0)),
                  pl.BlockSpec((1, nh, lc), lambda i, j: (i, 0, j)),
                  pl.BlockSpec((1, CONV_HIST, CONV_DIM), lambda i, j: (i, 0, 0)),
                  pl.BlockSpec((1, D_STATE, D_INNER), lambda i, j: (i, 0, 0)),
                  const((CONV_W, CONV_DIM)), const((1, CONV_DIM)),
                  const((1, nh)), const((nh, 1)), const((1, nh)), const((nh, 1)),
                  const((1, D_INNER)), const((1, D_INNER))],
        out_specs=[pl.BlockSpec((1, lc, D_INNER), lambda i, j: (i, j, 0)),
                   pl.BlockSpec((1, D_STATE, D_INNER), lambda i, j: (i, 0, 0))],
        out_shape=[jax.ShapeDtypeStruct((b, l, D_INNER), BF16),
                   jax.ShapeDtypeStruct((b, D_STATE, D_INNER), F32)],
        scratch_shapes=[pltpu.VMEM((lc + CONV_HIST, CONV_DIM), F32), pltpu.VMEM((D_STATE, D_INNER), F32)],
        compiler_params=_params(("parallel", "arbitrary")),
    )(xbc, z, dt_raw, dtt, past, h0t, conv_w, conv_b.reshape(1, -1),
      dt_bias.reshape(1, nh), dt_bias.reshape(nh, 1), a_log.reshape(1, nh), a_log.reshape(nh, 1),
      jnp.repeat(d_skip, HEAD_DIM_S).reshape(1, -1), norm_w.reshape(1, -1))
    return o, hout


def _outproj_kernel(x_ref, oa_ref, os_ref, gt_ref, sc_ref, sh_ref, g_ref, wa_ref, ws_ref, rw_ref, rb_ref,
                    x1_ref, h2_ref, lg_ref):
    nb, tl, d = x_ref.shape
    oa = oa_ref[...].reshape(nb * tl, -1)
    os_ = os_ref[...].reshape(nb * tl, -1)
    mix = jnp.dot(oa, wa_ref[...], preferred_element_type=F32) + jnp.dot(os_, ws_ref[...], preferred_element_type=F32)
    x1 = x_ref[...] + gt_ref[...] * mix.reshape(nb, tl, d)
    x1_ref[...] = x1
    ms = jnp.mean(x1 * x1, axis=-1, keepdims=True)
    h2 = x1 * lax.rsqrt(ms + EPS) * g_ref[...]
    h2 = h2 * (1.0 + sc_ref[...]) + sh_ref[...]
    h2 = h2.reshape(nb * tl, d)
    _store_row_tiles(h2_ref, h2)
    h_hi, h_lo = _split2(h2)
    w_hi, w_lo = rw_ref[0], rw_ref[1]
    lg = (jnp.dot(h_hi, w_hi, preferred_element_type=F32) + jnp.dot(h_lo, w_hi, preferred_element_type=F32)
          + jnp.dot(h_hi, w_lo, preferred_element_type=F32)) + rb_ref[...]
    lg_ref[...] = lg.reshape(nb, tl, -1)


def _outproj(x, oa, os_, gt, sc, sh, g, wa, ws, rw, rb):
    b, l, d = x.shape
    nb, tl = _row_blocking(b, l, ROW_TILE)
    row = lambda n: pl.BlockSpec((nb, tl, n), lambda i, j: (i, j, 0))
    mod = pl.BlockSpec((nb, 1, d), lambda i, j: (i, 0, 0))
    const = lambda a: pl.BlockSpec(a.shape, lambda i, j: tuple(0 for _ in a.shape))
    g3 = g.reshape(1, 1, d)
    return pl.pallas_call(
        _outproj_kernel,
        grid=(b // nb, l // tl),
        in_specs=[row(d), row(D_ATTN), row(D_INNER), mod, mod, mod, const(g3), const(wa), const(ws),
                  const(rw), const(rb)],
        out_specs=[row(d), pl.BlockSpec((nb * tl * ROW_TILE_SUBLANES, LANES), lambda i, j: (i * (l // tl) + j, 0)),
                   row(LANES)],
        out_shape=[jax.ShapeDtypeStruct((b, l, d), F32), jax.ShapeDtypeStruct((b * l * ROW_TILE_SUBLANES, LANES), F32),
                   jax.ShapeDtypeStruct((b, l, LANES), F32)],
        compiler_params=_params(("parallel", "parallel")),
    )(x, oa, os_, gt.reshape(b, 1, d), sc.reshape(b, 1, d), sh.reshape(b, 1, d), g3, wa, ws, rw, rb)


def _route_kernel(lg_ref, idx_ref, gate_ref, cnt_ref, carry_scr):
    @pl.when(pl.program_id(0) == 0)
    def _():
        carry_scr[...] = jnp.zeros(carry_scr.shape, F32)

    v = lg_ref[...]
    tr = v.shape[0]
    lane = lax.broadcasted_iota(jnp.int32, (tr, LANES), 1)
    lane_f = lane.astype(F32)
    tops, idxs = [], []
    onehot = jnp.zeros((tr, LANES), F32)
    for _ in range(TOP_K):
        m = jnp.max(v, axis=-1, keepdims=True)
        idx = jnp.min(jnp.where(v == m, lane_f, float(LANES)), axis=-1, keepdims=True)
        hit = lane_f == idx
        v = jnp.where(hit, -jnp.inf, v)
        onehot = onehot + hit.astype(F32)
        tops.append(m)
        idxs.append(idx)
    es = [jnp.exp(t - tops[0]) for t in tops]
    denom = es[0] + es[1] + es[2] + es[3]

    row = lax.broadcasted_iota(jnp.int32, (tr, tr), 0)
    col = lax.broadcasted_iota(jnp.int32, (tr, tr), 1)
    before = jnp.dot((col < row).astype(BF16), onehot.astype(BF16), preferred_element_type=F32) + carry_scr[...]

    idx_out = jnp.zeros((tr, LANES), F32)
    gate_out = jnp.zeros((tr, LANES), F32)
    for k in range(TOP_K):
        rank = jnp.sum(jnp.where(lane_f == idxs[k], before, 0.0), axis=-1, keepdims=True)
        idx_out = jnp.where(lane == k, idxs[k], idx_out)
        idx_out = jnp.where(lane == TOP_K + k, rank, idx_out)
        gate_out = jnp.where(lane == k, es[k] / denom, gate_out)
    idx_ref[...] = idx_out.astype(jnp.int32)
    gate_ref[...] = gate_out
    carry_scr[...] = carry_scr[...] + jnp.sum(onehot, axis=0, keepdims=True)
    cnt_ref[...] = carry_scr[...]


def _route(logits):
    t, n = logits.shape
    tr = min(ROUTE_TILE, t)
    assert t % tr == 0
    return pl.pallas_call(
        _route_kernel,
        grid=(t // tr,),
        in_specs=[pl.BlockSpec((tr, n), lambda i: (i, 0))],
        out_specs=[pl.BlockSpec((tr, n), lambda i: (i, 0)), pl.BlockSpec((tr, n), lambda i: (i, 0)),
                   pl.BlockSpec((1, n), lambda i: (0, 0))],
        out_shape=[jax.ShapeDtypeStruct((t, n), jnp.int32), jax.ShapeDtypeStruct((t, n), F32),
                   jax.ShapeDtypeStruct((1, n), F32)],
        scratch_shapes=[pltpu.VMEM((1, n), F32)],
        compiler_params=_params(("arbitrary",)),
    )(logits)


def _row_tile(ref, r):
    start = r * ROW_TILE_SUBLANES
    if not isinstance(r, int):
        start = pl.multiple_of(start, ROW_TILE_SUBLANES)
    return ref.at[pl.ds(start, ROW_TILE_SUBLANES), :]


def _store_row_tiles(ref, x, tile0=0):
    rows = x.shape[0]
    for s in range(x.shape[1] // LANES):
        ref[pl.ds(tile0 + s, rows, stride=ROW_TILE_SUBLANES), :] = x[:, s * LANES:(s + 1) * LANES]


def _load_row_tiles(ref, row0, rows):
    return jnp.concatenate(
        [ref[pl.ds(row0 * ROW_TILE_SUBLANES + s, rows, stride=ROW_TILE_SUBLANES), :] for s in range(ROW_TILE_SUBLANES)],
        axis=1)


def _gather_rows(idx_ref, src_hbm, dst, sem, n_rows):
    group = 8
    assert n_rows % group == 0

    def body(g, carry):
        for u in range(group):
            r = g * group + u
            pltpu.make_async_copy(_row_tile(src_hbm, idx_ref[0, 0, r]), _row_tile(dst, r), sem).start(priority=u % 2)
        return carry
    lax.fori_loop(0, n_rows // group, body, 0)


def _wait_rows(src_hbm, dst, sem, n_rows):
    pltpu.make_async_copy(src_hbm.at[pl.ds(0, n_rows * ROW_TILE_SUBLANES), :], dst, sem).wait()


EXPERT_COL_CHUNK = 256


GATHER_BUFS = 3

def _dispatch_kernel(ends_ref, pad_ref, dest_ref, h_hbm, xs_hbm, zbuf, hbuf, zsem, fsem, sem, *, n_slots):
    i = pl.program_id(0)
    n = pl.num_programs(0)
    m = zbuf.shape[0] // ROW_TILE_SUBLANES
    rows = dest_ref.shape[2]
    td = rows // TOP_K
    slot = lax.rem(i, GATHER_BUFS)

    def fetch(step, s):
        start = pl.multiple_of(step * (td * ROW_TILE_SUBLANES), td * ROW_TILE_SUBLANES)
        return pltpu.make_async_copy(h_hbm.at[pl.ds(start, td * ROW_TILE_SUBLANES), :], hbuf.at[s], fsem.at[s])

    def wait_rows_of(s):
        for _ in range(TOP_K):
            pltpu.make_async_copy(hbuf.at[s], xs_hbm.at[pl.ds(0, td * ROW_TILE_SUBLANES), :], sem.at[s]).wait()

    def zero_block(first_slot):
        start = pl.multiple_of(first_slot * ROW_TILE_SUBLANES, ROW_TILE_SUBLANES)
        return pltpu.make_async_copy(zbuf, xs_hbm.at[pl.ds(start, zbuf.shape[0]), :], zsem)

    @pl.when(i == 0)
    def _():
        zbuf[...] = jnp.zeros(zbuf.shape, zbuf.dtype)
        jobs = [(pad_ref[e] > 0, ends_ref[e] - m) for e in range(N_EXPERTS)]
        jobs += [(ends_ref[N_EXPERTS - 1] + j * m < n_slots, ends_ref[N_EXPERTS - 1] + j * m)
                 for j in range(N_EXPERTS + 1)]
        for cond, first in jobs:
            @pl.when(cond)
            def _():
                zero_block(first).start()
        for cond, first in jobs:
            @pl.when(cond)
            def _():
                zero_block(first).wait()
        fetch(0, 0).start()

    @pl.when(i >= 2)
    def _():
        wait_rows_of(lax.rem(i + 1, GATHER_BUFS))

    @pl.when(i + 1 < n)
    def _():
        fetch(i + 1, lax.rem(i + 1, GATHER_BUFS)).start()

    fetch(i, slot).wait()
    for s in range(GATHER_BUFS):
        @pl.when(slot == s)
        def _():
            for j in range(rows):
                pltpu.make_async_copy(_row_tile(hbuf.at[s], j // TOP_K), _row_tile(xs_hbm, dest_ref[0, 0, j]),
                                      sem.at[s]).start(priority=j % 2)

    @pl.when(i == n - 1)
    def _():
        @pl.when(i >= 1)
        def _():
            wait_rows_of(lax.rem(i + 2, GATHER_BUFS))
        wait_rows_of(slot)


def _dispatch(dest, ends_p, padded, h2, m, n_slots):
    t = dest.shape[0]
    td = min(DISPATCH_TILE, t)
    assert t % td == 0
    n_steps = t // td
    grid_spec = pltpu.PrefetchScalarGridSpec(
        num_scalar_prefetch=2,
        grid=(n_steps,),
        in_specs=[pl.BlockSpec((1, 1, TOP_K * td), lambda i, e, p: (i, 0, 0), memory_space=pltpu.SMEM),
                  pl.BlockSpec(memory_space=pl.ANY)],
        out_specs=pl.BlockSpec(memory_space=pl.ANY),
        scratch_shapes=[pltpu.VMEM((m * ROW_TILE_SUBLANES, LANES), F32),
                        pltpu.VMEM((GATHER_BUFS, td * ROW_TILE_SUBLANES, LANES), F32),
                        pltpu.SemaphoreType.DMA(()), pltpu.SemaphoreType.DMA((GATHER_BUFS,)),
                        pltpu.SemaphoreType.DMA((GATHER_BUFS,))],
    )
    return pl.pallas_call(
        functools.partial(_dispatch_kernel, n_slots=n_slots),
        grid_spec=grid_spec,
        out_shape=jax.ShapeDtypeStruct((n_slots * ROW_TILE_SUBLANES, LANES), F32),
        compiler_params=_params(("arbitrary",)),
    )(ends_p, padded, dest.reshape(n_steps, 1, TOP_K * td), h2)


def _experts_kernel(be_ref, used_ref, x_ref, wgu_ref, bgu_ref, wdn_ref, bdn_ref, o_ref):
    m = x_ref.shape[0] // ROW_TILE_SUBLANES
    cw = EXPERT_COL_CHUNK
    tiles_per_piece = cw // LANES
    n_ff, n_out = D_FF // cw, ROW_TILE_SUBLANES // tiles_per_piece
    in_use = pl.program_id(0) < used_ref[0]

    @pl.when(in_use)
    def _():
        x = _load_row_tiles(x_ref, 0, m).astype(BF16)
        acts = []
        for c in range(n_ff):
            g = (jnp.dot(x, wgu_ref[0, :, c * cw:(c + 1) * cw], preferred_element_type=F32)
                 + bgu_ref[0, :, c * cw:(c + 1) * cw])
            u = (jnp.dot(x, wgu_ref[0, :, D_FF + c * cw:D_FF + (c + 1) * cw], preferred_element_type=F32)
                 + bgu_ref[0, :, D_FF + c * cw:D_FF + (c + 1) * cw])
            gate = jnp.minimum(g, SWIGLU_LIMIT)
            up = jnp.clip(u, -SWIGLU_LIMIT, SWIGLU_LIMIT)
            acts.append(((up + 1.0) * gate * jax.nn.sigmoid(SWIGLU_ALPHA * gate)).astype(BF16))
        act = jnp.concatenate(acts, axis=1)
        for c in range(n_out):
            y = (jnp.dot(act, wdn_ref[0, :, c * cw:(c + 1) * cw], preferred_element_type=F32)
                 + bdn_ref[0, :, c * cw:(c + 1) * cw])
            _store_row_tiles(o_ref, y, tile0=c * tiles_per_piece)

    @pl.when(jnp.logical_not(in_use))
    def _():
        o_ref[...] = jnp.zeros(o_ref.shape, o_ref.dtype)


def _experts(block_e, n_used, xs, wgu, bgu, wdn, bdn, m):
    n_blocks = block_e.shape[0]
    d = D_MODEL
    mt = m * ROW_TILE_SUBLANES
    grid_spec = pltpu.PrefetchScalarGridSpec(
        num_scalar_prefetch=2,
        grid=(n_blocks,),
        in_specs=[pl.BlockSpec((mt, LANES), lambda i, be, nu: (jnp.minimum(i, nu[0] - 1), 0)),
                  pl.BlockSpec((1, d, 2 * D_FF), lambda i, be, nu: (be[i], 0, 0)),
                  pl.BlockSpec((1, 1, 2 * D_FF), lambda i, be, nu: (be[i], 0, 0)),
                  pl.BlockSpec((1, D_FF, d), lambda i, be, nu: (be[i], 0, 0)),
                  pl.BlockSpec((1, 1, d), lambda i, be, nu: (be[i], 0, 0))],
        out_specs=pl.BlockSpec((mt, LANES), lambda i, be, nu: (i, 0)),
    )
    return pl.pallas_call(
        _experts_kernel,
        grid_spec=grid_spec,
        out_shape=jax.ShapeDtypeStruct((n_blocks * mt, LANES), F32),
        compiler_params=_params(("arbitrary",)),
    )(block_e, n_used, xs, wgu, bgu.reshape(N_EXPERTS, 1, -1), wdn, bdn.reshape(N_EXPERTS, 1, -1))


def _combine_kernel(idx0_ref, idx1_ref, idxn_ref, y_hbm, x1_ref, gate_ref, gt_ref, g_ref, o_ref, ybuf, sem):
    i = pl.program_id(0)
    n = pl.num_programs(0)
    nb, tl, d = x1_ref.shape
    tc = nb * tl
    slot = lax.rem(i, GATHER_BUFS)
    nxt = lax.rem(i + 2, GATHER_BUFS)

    @pl.when(i == 0)
    def _():
        _gather_rows(idx0_ref, y_hbm, ybuf.at[0], sem.at[0], TOP_K * tc)
        _gather_rows(idx1_ref, y_hbm, ybuf.at[1], sem.at[1], TOP_K * tc)

    _wait_rows(y_hbm, ybuf.at[slot], sem.at[slot], TOP_K * tc)

    def prefetch(k):
        for r in range(k * tc, (k + 1) * tc):
            pltpu.make_async_copy(_row_tile(y_hbm, idxn_ref[0, 0, r]), _row_tile(ybuf.at[nxt], r),
                                  sem.at[nxt]).start(priority=r % 2)

    gates = gate_ref[...]
    y = gates[:, 0:1] * _load_row_tiles(ybuf.at[slot], 0, tc)
    prefetch(0)
    for k in range(1, TOP_K):
        y = y + gates[:, k:k + 1] * _load_row_tiles(ybuf.at[slot], k * tc, tc)
        prefetch(k)
    x2 = x1_ref[...] + gt_ref[...] * y.reshape(nb, tl, d)
    ms = jnp.mean(x2 * x2, axis=-1, keepdims=True)
    o_ref[...] = x2 * lax.rsqrt(ms + EPS) * g_ref[...]

    @pl.when(i == n - 1)
    def _():
        for ahead in (1, 2):
            s = lax.rem(i + ahead, GATHER_BUFS)
            _wait_rows(y_hbm, ybuf.at[s], sem.at[s], TOP_K * tc)


def _combine(dest, y_slots, x1, gates, gt, g_final):
    b, l, d = x1.shape
    nb, tl = _row_blocking(b, l, COMBINE_TILE)
    tc = nb * tl
    t = b * l
    n_steps = t // tc
    idx = jnp.swapaxes(dest.reshape(n_steps, tc, TOP_K), 1, 2).reshape(n_steps, 1, TOP_K * tc)
    steps_per_batch_row = l // tl
    tok = lambda i: (i // steps_per_batch_row, i % steps_per_batch_row, 0)
    smem = lambda f: pl.BlockSpec((1, 1, TOP_K * tc), f, memory_space=pltpu.SMEM)
    return pl.pallas_call(
        _combine_kernel,
        grid=(n_steps,),
        in_specs=[smem(lambda i: (0, 0, 0)),
                  smem(lambda i: (min(1, n_steps - 1), 0, 0)),
                  smem(lambda i: (jnp.minimum(i + 2, n_steps - 1), 0, 0)),
                  pl.BlockSpec(memory_space=pl.ANY),
                  pl.BlockSpec((nb, tl, d), tok),
                  pl.BlockSpec((tc, LANES), lambda i: (i, 0)),
                  pl.BlockSpec((nb, 1, d), lambda i: (i // steps_per_batch_row, 0, 0)),
                  pl.BlockSpec((1, 1, d), lambda i: (0, 0, 0))],
        out_specs=pl.BlockSpec((nb, tl, d), tok),
        out_shape=jax.ShapeDtypeStruct((b, l, d), F32),
        scratch_shapes=[pltpu.VMEM((GATHER_BUFS, TOP_K * tc * ROW_TILE_SUBLANES, LANES), F32),
                        pltpu.SemaphoreType.DMA((GATHER_BUFS,))],
        compiler_params=_params(("arbitrary",)),
    )(idx, idx, idx, y_slots, x1, gates, gt.reshape(b, 1, d), g_final.reshape(1, 1, d))


def _moe(h2, logits, x1, gt_m, g_final, wts):
    b, l, d = x1.shape
    t = b * l
    m = MOE_ROWS if t * TOP_K >= 4 * N_EXPERTS * MOE_ROWS else MOE_ROWS_SMALL
    idx, gates, counts = _route(logits.reshape(t, LANES))
    top_e = idx[:, :TOP_K]
    rank = idx[:, TOP_K:2 * TOP_K]
    counts = counts[0, :N_EXPERTS].astype(jnp.int32)
    padded = (counts + m - 1) // m * m
    ends_p = jnp.cumsum(padded)
    starts_p = ends_p - padded
    experts = jnp.arange(N_EXPERTS, dtype=jnp.int32)
    start_of = jnp.sum(jnp.where(top_e[..., None] == experts, starts_p, 0), axis=-1)
    dest = start_of + rank
    n_blocks = (t * TOP_K + N_EXPERTS * (m - 1) + m - 1) // m
    block_start = jnp.arange(n_blocks, dtype=jnp.int32)[:, None] * m
    block_e = jnp.minimum(jnp.sum((ends_p[None, :] <= block_start).astype(jnp.int32), axis=1), N_EXPERTS - 1)
    xs = _dispatch(dest, ends_p, padded, h2, m, n_blocks * m)
    n_used = (ends_p[N_EXPERTS - 1:] // m).astype(jnp.int32)
    y_slots = _experts(block_e, n_used, xs, *wts, m)
    return _combine(dest, y_slots, x1, gates, gt_m, g_final)


def _layer(x, mod, k_past, v_past, logf_past, conv_past, ssm_past, p, g_final):
    b, l, d = x.shape
    sh_a, sc_a, gt_a, sh_m, sc_m, gt_m = jnp.split(mod, 6, axis=-1)
    q, kf, vf, kb, vb, z, xbc, sm = _inproj(x, p['g_mix'], sc_a, sh_a, *p['w_in'])
    dt_raw = sm[:, :, N_HEADS_A:N_HEADS_A + N_HEADS_S]

    if k_past is None:
        logf, qx, kx = _cumsum_bias(sm, p['b_f'])
        o_a = _attention(q, qx, kb, kx, vb)
        conv_past = jnp.zeros((b, CONV_W - 1, CONV_DIM), F32)
        h0t = jnp.zeros((b, D_STATE, D_INNER), F32)
    else:
        lp = k_past.shape[1]
        past_t = jnp.swapaxes(logf_past, 1, 2).reshape(b * N_HEADS_A, lp)
        cumt_p = _cumsum_lanes(past_t).reshape(b, N_HEADS_A, lp)
        carry = jnp.pad(cumt_p[:, :, lp - 1], ((0, 0), (0, LANES - N_HEADS_A))).reshape(b, 1, LANES)
        lpad = -(-l // LANES) * LANES
        sm_pad = jnp.pad(sm, ((0, 0), (0, lpad - l), (0, 0)))
        logf, cum_n, cumt_n = _cumsum(sm_pad, p['b_f'], carry)
        logf = logf[:, :l]
        o_a = _attention_cached(q, kb, vb, k_past, v_past, cum_n, cumt_p, cumt_n)
        h0t = jnp.swapaxes(ssm_past.reshape(b, D_INNER, D_STATE), 1, 2)
    o_s, hout = _ssd(xbc, z, dt_raw, conv_past, h0t, p['conv_w'], p['conv_b'], p['dt_bias'], p['a_log'],
                     p['d_skip'], p['ssd_norm_w'])
    x1, h2, logits = _outproj(x, o_a, o_s, gt_a, sc_m, sh_m, p['g_ffn'], p['w_out_a'], p['w_out_s'],
                              p['router_w'], p['router_b'])
    y = _moe(h2, logits, x1, gt_m, g_final, p['experts'])

    assert l >= CONV_W - 1
    conv_new = xbc[:, l - (CONV_W - 1):]
    ssm_new = jnp.swapaxes(hout, 1, 2).reshape(b, N_HEADS_S, HEAD_DIM_S, D_STATE)
    return (y, kf.reshape(b, l, N_HEADS_A, HEAD_DIM_A), vf.reshape(b, l, N_HEADS_A, HEAD_DIM_A),
            logf[:, :, :N_HEADS_A], conv_new, ssm_new)


def kernel(x_prompt, x_sample, cache_k, cache_v, cache_logf, state_conv, state_ssm, c_prompt, c_sample, w_ada, b_ada, g_mix, w_in, b_f, conv_w, conv_b, dt_bias, a_log, d_skip, ssd_norm_w, w_out, g_ffn, router_w, router_b, w_gate_up, b_gate_up, w_down, b_down, g_final):
    assert w_ada.shape[0] == 1, "single-layer operation"
    bp = x_prompt.shape[0]
    w = w_in[0]
    q_end, k_end, v_end = D_ATTN, 2 * D_ATTN, 3 * D_ATTN
    f_end = v_end + N_HEADS_A
    z_end = f_end + D_INNER
    xbc_end = z_end + CONV_DIM
    w_small = jnp.concatenate(
        [w[:, v_end:f_end], w[:, xbc_end:], jnp.zeros((D_MODEL, LANES - N_HEADS_A - N_HEADS_S), F32)], axis=1)
    cast = lambda a: a.astype(BF16)
    p = {
        'g_mix': g_mix[0],
        'w_in': (cast(w[:, :q_end]), cast(w[:, q_end:k_end]), cast(w[:, k_end:v_end]), cast(w[:, f_end:z_end]),
                 cast(w[:, z_end:xbc_end]), cast(w_small)),
        'b_f': jnp.pad(b_f[0], (0, LANES - N_HEADS_A)).reshape(1, LANES),
        'conv_w': conv_w[0], 'conv_b': conv_b[0], 'dt_bias': dt_bias[0], 'a_log': a_log[0],
        'd_skip': d_skip[0], 'ssd_norm_w': ssd_norm_w[0],
        'w_out_a': cast(w_out[0, :D_ATTN]), 'w_out_s': cast(w_out[0, D_ATTN:]),
        'g_ffn': g_ffn[0],
        'router_w': jnp.stack(_split2(jnp.pad(router_w[0], ((0, 0), (0, LANES - N_EXPERTS))))),
        'router_b': jnp.pad(router_b[0], (0, LANES - N_EXPERTS), constant_values=-jnp.inf).reshape(1, LANES),
        'experts': (cast(w_gate_up[0]), b_gate_up[0], cast(w_down[0]), b_down[0]),
    }
    mod = _adaln(jnp.concatenate([c_prompt, c_sample], axis=0), w_ada[0], b_ada[0])
    outs_p = _layer(x_prompt, mod[:bp], None, None, None, None, None, p, g_final)
    outs_s = _layer(x_sample, mod[bp:], cache_k[0], cache_v[0], cache_logf[0], state_conv[0], state_ssm[0], p, g_final)
    stack = lambda a: a[None]
    return (outs_p[0], outs_s[0]) + tuple(stack(a) for a in outs_p[1:]) + tuple(stack(a) for a in outs_s[1:])
```

```python
import functools
import math

import jax
import jax.numpy as jnp
from jax import lax
from jax.experimental import pallas as pl
from jax.experimental.pallas import tpu as pltpu

F32 = jnp.float32
BF16 = jnp.bfloat16
HIGHEST = lax.Precision.HIGHEST

D_MODEL = 1024
N_HEADS_A = 8
HEAD_DIM_A = 128
D_ATTN = N_HEADS_A * HEAD_DIM_A
D_INNER = 1024
HEAD_DIM_S = 64
N_HEADS_S = D_INNER // HEAD_DIM_S
N_GROUPS_S = 2
D_STATE = 128
CONV_W = 4
CONV_DIM = D_INNER + 2 * N_GROUPS_S * D_STATE
N_EXPERTS = 32
TOP_K = 4
D_FF = 1024
SWIGLU_LIMIT = 7.0
SWIGLU_ALPHA = 1.702
EPS = 1e-5
LANES = 128
SUBLANES = 8
ROW_TILE_SUBLANES = D_MODEL // LANES
assert ROW_TILE_SUBLANES == SUBLANES
VMEM_LIMIT = 56 * 1024 * 1024
LOG2E = 1.4426950408889634
ATT_HEADS_PER_STEP = 4

ADALN_COL_TILE = 512
ROW_TILE = 512
ATT_TILE = 512
DEC_KV_TILE = 1024
CUM_TILE = 512
SSD_CHUNK = 128
ROUTE_TILE = 256
MOE_ROWS = 512
MOE_ROWS_SMALL = 128
COMBINE_TILE = 256
DISPATCH_TILE = 256

def _params(sem):
    return pltpu.CompilerParams(dimension_semantics=sem, vmem_limit_bytes=VMEM_LIMIT)


def _row_blocking(b, l, tile):
    if l >= tile:
        assert l % tile == 0
        return 1, tile
    nb = min(b, max(1, tile // l))
    while b % nb:
        nb -= 1
    return nb, l


def _split3(x):
    hi = x.astype(BF16)
    r1 = x - hi.astype(F32)
    mid = r1.astype(BF16)
    lo = (r1 - mid.astype(F32)).astype(BF16)
    return hi, mid, lo


def _split2(x):
    hi = x.astype(BF16)
    return hi, (x - hi.astype(F32)).astype(BF16)


def _tri_dot(tri, x):
    hi, mid, lo = _split3(x)
    d = lambda p: jnp.dot(tri, p, preferred_element_type=F32)
    return d(hi) + d(mid) + d(lo)


def _dot_tri(x, tri):
    hi, mid, lo = _split3(x)
    d = lambda p: jnp.dot(p, tri, preferred_element_type=F32)
    return d(hi) + d(mid) + d(lo)


def _silu(x):
    return x * jax.nn.sigmoid(x)


def _softplus(x):
    return jnp.maximum(x, 0.0) + jnp.log1p(jnp.exp(-jnp.abs(x)))


def _log_sigmoid(x):
    return jnp.minimum(x, 0.0) - jnp.log1p(jnp.exp(-jnp.abs(x)))


def _adaln_kernel(c_ref, w_ref, b_ref, o_ref):
    s = _silu(c_ref[...])
    o_ref[...] = jnp.dot(s, w_ref[...], precision=HIGHEST, preferred_element_type=F32) + b_ref[...]


def _adaln(c, w, b):
    m, d = c.shape
    n = w.shape[1]
    tn = ADALN_COL_TILE
    assert n % tn == 0
    return pl.pallas_call(
        _adaln_kernel,
        grid=(n // tn,),
        in_specs=[pl.BlockSpec((m, d), lambda j: (0, 0)),
                  pl.BlockSpec((d, tn), lambda j: (0, j)),
                  pl.BlockSpec((1, tn), lambda j: (0, j))],
        out_specs=pl.BlockSpec((m, tn), lambda j: (0, j)),
        out_shape=jax.ShapeDtypeStruct((m, n), F32),
        compiler_params=_params(("parallel",)),
    )(c, w, b.reshape(1, n))


def _inproj_kernel(x_ref, g_ref, sc_ref, sh_ref, wq_ref, wk_ref, wv_ref, wz_ref, wx_ref, ws_ref,
                   q_ref, kf_ref, vf_ref, kb_ref, vb_ref, z_ref, xbc_ref, sm_ref):
    x = x_ref[...]
    nb, tl, d = x.shape
    ms = jnp.mean(x * x, axis=-1, keepdims=True)
    h = x * lax.rsqrt(ms + EPS) * g_ref[...]
    h = h * (1.0 + sc_ref[...]) + sh_ref[...]
    hb = h.reshape(nb * tl, d).astype(BF16)

    def mm(w_ref):
        return jnp.dot(hb, w_ref[...], preferred_element_type=F32)

    q = mm(wq_ref) * (LOG2E / math.sqrt(HEAD_DIM_A))
    q_ref[...] = q.astype(BF16).reshape(nb, tl, -1)
    k = mm(wk_ref)
    kf_ref[...] = k.reshape(nb, tl, -1)
    kb_ref[...] = k.astype(BF16).reshape(nb, tl, -1)
    v = mm(wv_ref)
    vf_ref[...] = v.reshape(nb, tl, -1)
    vb_ref[...] = v.astype(BF16).reshape(nb, tl, -1)
    z_ref[...] = mm(wz_ref).reshape(nb, tl, -1)
    xbc_ref[...] = mm(wx_ref).reshape(nb, tl, -1)
    sm_ref[...] = mm(ws_ref).reshape(nb, tl, -1)


def _inproj(x, g, sc, sh, wq, wk, wv, wz, wx, ws):
    b, l, d = x.shape
    nb, tl = _row_blocking(b, l, ROW_TILE)
    grid = (b // nb, l // tl)
    row = lambda n: pl.BlockSpec((nb, tl, n), lambda i, j: (i, j, 0))
    mod = pl.BlockSpec((nb, 1, d), lambda i, j: (i, 0, 0))
    wspec = lambda w: pl.BlockSpec(w.shape, lambda i, j: (0, 0), pipeline_mode=pl.Buffered(1))
    outs = [(D_ATTN, BF16), (D_ATTN, F32), (D_ATTN, F32), (D_ATTN, BF16), (D_ATTN, BF16),
            (D_INNER, F32), (CONV_DIM, F32), (LANES, F32)]
    return pl.pallas_call(
        _inproj_kernel,
        grid=grid,
        in_specs=[row(d), pl.BlockSpec((1, 1, d), lambda i, j: (0, 0, 0)), mod, mod,
                  wspec(wq), wspec(wk), wspec(wv), wspec(wz), wspec(wx), wspec(ws)],
        out_specs=[row(n) for n, _ in outs],
        out_shape=[jax.ShapeDtypeStruct((b, l, n), dt) for n, dt in outs],
        compiler_params=_params(("parallel", "parallel")),
    )(x, g.reshape(1, 1, d), sc.reshape(b, 1, d), sh.reshape(b, 1, d), wq, wk, wv, wz, wx, ws)


def _running_sum(v_ref, bias_ref, carry_ref, carry_scr):
    @pl.when(pl.program_id(1) == 0)
    def _():
        carry_scr[...] = carry_ref[0]

    v = v_ref[0]
    tl = v.shape[0]
    lf = _log_sigmoid(v + bias_ref[...])
    row = lax.broadcasted_iota(jnp.int32, (tl, tl), 0)
    col = lax.broadcasted_iota(jnp.int32, (tl, tl), 1)
    tri = (row >= col).astype(BF16)
    cs = _tri_dot(tri, lf) + carry_scr[...]
    carry_scr[...] = cs[tl - 1:tl, :]
    return lf, cs


def _cumsum_kernel(v_ref, bias_ref, carry_ref, logf_ref, cum_ref, cumt_ref, carry_scr):
    lf, cs = _running_sum(v_ref, bias_ref, carry_ref, carry_scr)
    logf_ref[0] = lf
    cum_ref[0] = cs
    cumt_ref[0] = cs.T[:N_HEADS_A, :]


def _cumsum(vals, bias, carry):
    b, l, n = vals.shape
    tl = min(CUM_TILE, l)
    assert l % tl == 0 and tl % LANES == 0
    return pl.pallas_call(
        _cumsum_kernel,
        grid=(b, l // tl),
        in_specs=[pl.BlockSpec((1, tl, n), lambda i, j: (i, j, 0)),
                  pl.BlockSpec((1, n), lambda i, j: (0, 0)),
                  pl.BlockSpec((1, 1, n), lambda i, j: (i, 0, 0))],
        out_specs=[pl.BlockSpec((1, tl, n), lambda i, j: (i, j, 0)),
                   pl.BlockSpec((1, tl, n), lambda i, j: (i, j, 0)),
                   pl.BlockSpec((1, N_HEADS_A, tl), lambda i, j: (i, 0, j))],
        out_shape=[jax.ShapeDtypeStruct((b, l, n), F32),
                   jax.ShapeDtypeStruct((b, l, n), F32),
                   jax.ShapeDtypeStruct((b, N_HEADS_A, l), F32)],
        scratch_shapes=[pltpu.VMEM((1, n), F32)],
        compiler_params=_params(("parallel", "arbitrary")),
    )(vals, bias, carry)


def _cumsum_lanes_kernel(x_ref, o_ref, carry_scr):
    @pl.when(pl.program_id(0) == 0)
    def _():
        carry_scr[...] = jnp.zeros(carry_scr.shape, F32)

    x = x_ref[...]
    tl = x.shape[1]
    row = lax.broadcasted_iota(jnp.int32, (tl, tl), 0)
    col = lax.broadcasted_iota(jnp.int32, (tl, tl), 1)
    cs = _dot_tri(x, (row <= col).astype(BF16)) + carry_scr[...]
    o_ref[...] = cs
    carry_scr[...] = cs[:, tl - 1:tl]


def _cumsum_lanes(x):
    rows, l = x.shape
    tl = min(CUM_TILE, l)
    assert l % tl == 0 and rows % SUBLANES == 0
    return pl.pallas_call(
        _cumsum_lanes_kernel,
        grid=(l // tl,),
        in_specs=[pl.BlockSpec((rows, tl), lambda j: (0, j))],
        out_specs=pl.BlockSpec((rows, tl), lambda j: (0, j)),
        out_shape=jax.ShapeDtypeStruct((rows, l), F32),
        scratch_shapes=[pltpu.VMEM((rows, 1), F32)],
        compiler_params=_params(("arbitrary",)),
    )(x)


N_BIAS_TERMS = 3


def _bias_placement():
    rows = jnp.arange(N_BIAS_TERMS * LANES)
    n, h = rows // LANES, rows % LANES
    col = jnp.arange(D_ATTN)[None, :]
    valid = (h < N_HEADS_A)[:, None]
    eq = jnp.where(valid & (col == (h * LANES + n)[:, None]), 1.0, 0.0)
    ek = jnp.where(valid & (col == (h * LANES + N_BIAS_TERMS + n)[:, None]), -1.0, 0.0)
    lane = jnp.arange(D_ATTN) % LANES
    ones_q = jnp.where((lane >= N_BIAS_TERMS) & (lane < 2 * N_BIAS_TERMS), 1.0, 0.0)
    ones_k = jnp.where(lane < N_BIAS_TERMS, 1.0, 0.0)
    return eq.astype(BF16), ek.astype(BF16), ones_q.reshape(1, -1).astype(F32), ones_k.reshape(1, -1).astype(F32)


def _cumsum_bias_kernel(v_ref, bias_ref, carry_ref, eq_ref, ek_ref, oq_ref, ok_ref, logf_ref, qx_ref, kx_ref, carry_scr):
    lf, cs = _running_sum(v_ref, bias_ref, carry_ref, carry_scr)
    logf_ref[0] = lf
    terms = jnp.concatenate(_split3(cs * LOG2E), axis=1)
    qx_ref[0] = (jnp.dot(terms, eq_ref[...], preferred_element_type=F32) + oq_ref[...]).astype(BF16)
    kx_ref[0] = (jnp.dot(terms, ek_ref[...], preferred_element_type=F32) + ok_ref[...]).astype(BF16)


def _cumsum_bias(vals, bias):
    b, l, n = vals.shape
    tl = min(CUM_TILE, l)
    assert l % tl == 0
    row = lambda w: pl.BlockSpec((1, tl, w), lambda i, j: (i, j, 0))
    const = lambda a: pl.BlockSpec(a.shape, lambda i, j: (0, 0))
    placement = _bias_placement()
    return pl.pallas_call(
        _cumsum_bias_kernel,
        grid=(b, l // tl),
        in_specs=[row(n), pl.BlockSpec((1, n), lambda i, j: (0, 0)), pl.BlockSpec((1, 1, n), lambda i, j: (i, 0, 0))]
        + [const(a) for a in placement],
        out_specs=[row(n), row(D_ATTN), row(D_ATTN)],
        out_shape=[jax.ShapeDtypeStruct((b, l, n), F32), jax.ShapeDtypeStruct((b, l, D_ATTN), BF16),
                   jax.ShapeDtypeStruct((b, l, D_ATTN), BF16)],
        scratch_shapes=[pltpu.VMEM((1, n), F32)],
        compiler_params=_params(("parallel", "arbitrary")),
    )(vals, bias, jnp.zeros((b, 1, n), F32), *placement)


_NT = (((1,), (1,)), ((), ()))


def _softmax_step(s, v, m_prev, l_prev, acc_prev):
    m_new = jnp.maximum(m_prev, jnp.max(s, axis=-1, keepdims=True))
    p = jnp.exp2(s - m_new)
    alpha = jnp.exp2(m_prev - m_new)
    l_new = alpha * l_prev + jnp.sum(p, axis=-1, keepdims=True)
    acc_new = alpha * acc_prev + jnp.dot(p.astype(BF16), v, preferred_element_type=F32)
    return m_new, l_new, acc_new


def _attn_kernel(q_ref, qx_ref, k_ref, kx_ref, v_ref, o_ref, m_scr, acc_scr, *, tile, heads):
    i = pl.program_id(2)
    dh = HEAD_DIM_A
    n_chunks = tile // LANES
    ones = jnp.ones((tile, dh), BF16)
    m_scr[...] = jnp.full(m_scr.shape, -jnp.inf, F32)
    acc_scr[...] = jnp.zeros(acc_scr.shape, F32)
    q2 = [jnp.concatenate([q_ref[0, :, hh * dh:(hh + 1) * dh], qx_ref[0, :, hh * dh:(hh + 1) * dh]], axis=1)
          for hh in range(heads)]

    def update(j, mask):
        start = pl.multiple_of(j * tile, tile)
        rows = pl.ds(start, tile)
        for hh in range(heads):
            sl = slice(hh * dh, (hh + 1) * dh)
            k2 = jnp.concatenate([k_ref[0, rows, sl], kx_ref[0, rows, sl]], axis=1)
            v2 = jnp.concatenate([v_ref[0, rows, sl], ones], axis=1)
            s = lax.dot_general(q2[hh], k2, _NT, preferred_element_type=F32)
            if mask is not None:
                s = jnp.where(mask, s, -jnp.inf)
            chunks = [s[:, c * LANES:(c + 1) * LANES] for c in range(n_chunks)]
            cmax = functools.reduce(jnp.maximum, chunks)
            m_prev = m_scr[hh]
            m_new = jnp.maximum(m_prev, jnp.max(cmax, axis=-1, keepdims=True))
            alpha = jnp.exp2(m_prev - m_new)
            p = jnp.concatenate([jnp.exp2(c - m_new) for c in chunks], axis=1).astype(BF16)
            pv = jnp.dot(p, v2, preferred_element_type=F32)
            m_scr[hh] = m_new
            acc_scr[hh, :, :dh] = alpha * acc_scr[hh, :, :dh] + pv[:, :dh]
            acc_scr[hh, :, dh:] = alpha * acc_scr[hh, :, dh:] + pv[:, dh:]

    def body(j, carry):
        update(j, None)
        return carry

    lax.fori_loop(0, i, body, 0)
    row = lax.broadcasted_iota(jnp.int32, (tile, tile), 0)
    col = lax.broadcasted_iota(jnp.int32, (tile, tile), 1)
    update(i, col <= row)
    for hh in range(heads):
        o_ref[0, :, hh * dh:(hh + 1) * dh] = (acc_scr[hh, :, :dh] / acc_scr[hh, :, dh:]).astype(o_ref.dtype)


def _attention(q, qx, k, kx, v):
    b, l, _ = q.shape
    tile = ATT_TILE if l >= 2 * ATT_TILE else LANES
    assert l % tile == 0
    heads = ATT_HEADS_PER_STEP
    w = heads * HEAD_DIM_A
    qspec = pl.BlockSpec((1, tile, w), lambda bi, h, i: (bi, i, h))
    kspec = pl.BlockSpec((1, l, w), lambda bi, h, i: (bi, 0, h))
    return pl.pallas_call(
        functools.partial(_attn_kernel, tile=tile, heads=heads),
        grid=(b, N_HEADS_A // heads, l // tile),
        in_specs=[qspec, qspec, kspec, kspec, kspec],
        out_specs=qspec,
        out_shape=jax.ShapeDtypeStruct((b, l, D_ATTN), BF16),
        scratch_shapes=[pltpu.VMEM((heads, tile, LANES), F32), pltpu.VMEM((heads, tile, 2 * HEAD_DIM_A), F32)],
        compiler_params=_params(("parallel", "parallel", "arbitrary")),
    )(q, qx, k, kx, v)


def _attn_cached_kernel(q_ref, kn_ref, vn_ref, kc_ref, vc_ref, cumn_ref, cumtp_ref, cumtn_ref, o_ref,
                        m_scr, l_scr, acc_scr):
    j = pl.program_id(1)
    ld = q_ref.shape[1]
    dh = HEAD_DIM_A

    @pl.when(j == 0)
    def _():
        m_scr[...] = jnp.full(m_scr.shape, -jnp.inf, F32)
        l_scr[...] = jnp.zeros(l_scr.shape, F32)
        acc_scr[...] = jnp.zeros(acc_scr.shape, F32)

    def head_step(h, kh, vh, ck, mask):
        sl = slice(h * dh, (h + 1) * dh)
        s = lax.dot_general(q_ref[0, :, sl], kh, _NT, preferred_element_type=F32)
        s = s + (cumn_ref[0, :ld, h:h + 1] - ck) * LOG2E
        if mask is not None:
            s = jnp.where(mask, s, -jnp.inf)
        m, l, acc = _softmax_step(s, vh, m_scr[h], l_scr[h], acc_scr[:, sl])
        m_scr[h] = m
        l_scr[h] = l
        acc_scr[:, sl] = acc

    for h in range(N_HEADS_A):
        sl = slice(h * dh, (h + 1) * dh)
        rows = pl.ds(h, kc_ref.shape[1] // N_HEADS_A, stride=N_HEADS_A)
        head_step(h, kc_ref[0, rows, :].astype(BF16), vc_ref[0, rows, :].astype(BF16),
                  cumtp_ref[0, h:h + 1, :], None)

    @pl.when(j == pl.num_programs(1) - 1)
    def _():
        row = lax.broadcasted_iota(jnp.int32, (ld, ld), 0)
        col = lax.broadcasted_iota(jnp.int32, (ld, ld), 1)
        for h in range(N_HEADS_A):
            sl = slice(h * dh, (h + 1) * dh)
            head_step(h, kn_ref[0, :, sl], vn_ref[0, :, sl], cumtn_ref[0, h:h + 1, :ld], col <= row)
            o_ref[0, :, sl] = (acc_scr[:, sl] / l_scr[h]).astype(o_ref.dtype)


def _attention_cached(q, kn, vn, kc, vc, cum_new, cumt_past, cumt_new):
    b, ld, _ = q.shape
    lp = kc.shape[1]
    tk = min(DEC_KV_TILE, lp)
    assert lp % tk == 0
    lpad = cum_new.shape[1]
    new = pl.BlockSpec((1, ld, D_ATTN), lambda bi, j: (bi, 0, 0))
    past = pl.BlockSpec((1, tk * N_HEADS_A, HEAD_DIM_A), lambda bi, j: (bi, j, 0))
    kc = kc.reshape(b, lp * N_HEADS_A, HEAD_DIM_A)
    vc = vc.reshape(b, lp * N_HEADS_A, HEAD_DIM_A)
    return pl.pallas_call(
        _attn_cached_kernel,
        grid=(b, lp // tk),
        in_specs=[new, new, new, past, past,
                  pl.BlockSpec((1, lpad, LANES), lambda bi, j: (bi, 0, 0)),
                  pl.BlockSpec((1, N_HEADS_A, tk), lambda bi, j: (bi, 0, j)),
                  pl.BlockSpec((1, N_HEADS_A, lpad), lambda bi, j: (bi, 0, 0))],
        out_specs=new,
        out_shape=jax.ShapeDtypeStruct((b, ld, D_ATTN), BF16),
        scratch_shapes=[pltpu.VMEM((N_HEADS_A, ld, 1), F32), pltpu.VMEM((N_HEADS_A, ld, 1), F32),
                        pltpu.VMEM((ld, D_ATTN), F32)],
        compiler_params=_params(("parallel", "arbitrary")),
    )(q, kn, vn, kc, vc, cum_new, cumt_past, cumt_new)


def _expand_heads(a):
    r = a.shape[0]
    low = lax.broadcasted_iota(jnp.int32, (r, LANES), 1) < HEAD_DIM_S
    return jnp.concatenate(
        [jnp.where(low, a[:, 2 * j:2 * j + 1], a[:, 2 * j + 1:2 * j + 2]) for j in range(N_HEADS_S // 2)], axis=1)


CONV_HIST = SUBLANES


def _ssd_kernel(xbc_ref, z_ref, dt_ref, dtt_ref, past_ref, h0_ref, cw_ref, cb_ref, dtb_ref, dtbt_ref,
                alog_ref, alogt_ref, dsk_ref, nw_ref, o_ref, hout_ref, xbuf, ht_scr):
    c = pl.program_id(1)
    lc = xbc_ref.shape[1]
    hist = CONV_HIST

    @pl.when(c == 0)
    def _():
        xbuf[0:hist, :] = past_ref[0]
        ht_scr[...] = h0_ref[0]

    xbuf[hist:hist + lc, :] = xbc_ref[0]
    u = cb_ref[...]
    for w in range(CONV_W):
        off = hist - (CONV_W - 1) + w
        u = u + xbuf[off:off + lc, :] * cw_ref[w:w + 1, :]
    xbuf[0:hist, :] = xbuf[lc:lc + hist, :]
    u = _silu(u)
    xs = u[:, :D_INNER]
    gn = N_GROUPS_S * D_STATE
    bm = u[:, D_INNER:D_INNER + gn].astype(BF16)
    cm = u[:, D_INNER + gn:].astype(BF16)

    dt = _softplus(dt_ref[0] + dtb_ref[...])
    dtt = _softplus(dtt_ref[0] + dtbt_ref[...])
    a = dt * (-jnp.exp(alog_ref[...]))
    at = dtt * (-jnp.exp(alogt_ref[...]))
    row = lax.broadcasted_iota(jnp.int32, (lc, lc), 0)
    col = lax.broadcasted_iota(jnp.int32, (lc, lc), 1)
    causal = col <= row
    a_cs = _tri_dot(causal.astype(BF16), a)
    a_cst = _dot_tri(at, (row <= col).astype(BF16))
    total = a_cs[lc - 1:lc, :]
    dt_e = _expand_heads(dt)
    w_e = _expand_heads(dt * jnp.exp(total - a_cs))
    ea_e = _expand_heads(jnp.exp(a_cs))
    cd_e = _expand_heads(jnp.exp(total))
    xdt = xs * dt_e
    xdw = (xs * w_e).astype(BF16)

    cbs = [lax.dot_general(cm[:, g * D_STATE:(g + 1) * D_STATE], bm[:, g * D_STATE:(g + 1) * D_STATE], _NT,
                           preferred_element_type=F32) for g in range(N_GROUPS_S)]
    low = lax.broadcasted_iota(jnp.int32, (lc, LANES), 1) < HEAD_DIM_S
    heads_per_group = N_HEADS_S // N_GROUPS_S
    yd = []
    for j in range(N_HEADS_S // 2):
        ms = []
        for hh in (2 * j, 2 * j + 1):
            seg = a_cs[:, hh:hh + 1] - a_cst[hh:hh + 1, :]
            dec = jnp.where(causal, jnp.exp(jnp.where(causal, seg, 0.0)), 0.0)
            ms.append((cbs[hh // heads_per_group] * dec).astype(BF16))
        xb = xdt[:, j * LANES:(j + 1) * LANES]
        rhs = jnp.concatenate([jnp.where(low, xb, 0.0), jnp.where(low, 0.0, xb)], axis=0).astype(BF16)
        yd.append(jnp.dot(jnp.concatenate(ms, axis=1), rhs, preferred_element_type=F32))
    y = jnp.concatenate(yd, axis=1)

    half = D_INNER // N_GROUPS_S
    ht = ht_scr[...]
    htb = ht.astype(BF16)
    y_off = jnp.concatenate(
        [jnp.dot(cm[:, g * D_STATE:(g + 1) * D_STATE], htb[:, g * half:(g + 1) * half],
                 preferred_element_type=F32) for g in range(N_GROUPS_S)], axis=1)
    st = jnp.concatenate(
        [lax.dot_general(bm[:, g * D_STATE:(g + 1) * D_STATE], xdw[:, g * half:(g + 1) * half],
                         (((0,), (0,)), ((), ())), preferred_element_type=F32) for g in range(N_GROUPS_S)], axis=1)
    ht_new = cd_e * ht + st
    ht_scr[...] = ht_new
    hout_ref[0] = ht_new

    y = y + y_off * ea_e + dsk_ref[...] * xs
    gz = y * _silu(z_ref[0])
    outs = []
    for g in range(N_GROUPS_S):
        gg = gz[:, g * half:(g + 1) * half]
        outs.append(gg * lax.rsqrt(jnp.mean(gg * gg, axis=-1, keepdims=True) + EPS))
    o_ref[0] = (jnp.concatenate(outs, axis=1) * nw_ref[...]).astype(o_ref.dtype)


def _ssd(xbc, z, dt_raw, conv_past, h0t, conv_w, conv_b, dt_bias, a_log, d_skip, norm_w):
    b, l, _ = xbc.shape
    lc = min(SSD_CHUNK, l)
    assert l % lc == 0 and lc % CONV_HIST == 0
    nh = N_HEADS_S
    dtt = jnp.swapaxes(dt_raw, 1, 2)
    past = jnp.pad(conv_past, ((0, 0), (CONV_HIST - (CONV_W - 1), 0), (0, 0)))
    const = lambda shape: pl.BlockSpec(shape, lambda i, j: tuple(0 for _ in shape))
    o, hout = pl.pallas_call(
        _ssd_kernel,
        grid=(b, l // lc),
        in_specs=[pl.BlockSpec((1, lc, CONV_DIM), lambda i, j: (i, j, 0)),
                  pl.BlockSpec((1, lc, D_INNER), lambda i, j: (i, j, 0)),
                  pl.BlockSpec((1, lc, nh), lambda i, j: (i, j, 0)),
                  pl.BlockSpec((1, nh, lc), lambda i, j: (i, 0, j)),
                  pl.BlockSpec((1, CONV_HIST, CONV_DIM), lambda i, j: (i, 0, 0)),
                  pl.BlockSpec((1, D_STATE, D_INNER), lambda i, j: (i, 0, 0)),
                  const((CONV_W, CONV_DIM)), const((1, CONV_DIM)),
                  const((1, nh)), const((nh, 1)), const((1, nh)), const((nh, 1)),
                  const((1, D_INNER)), const((1, D_INNER))],
        out_specs=[pl.BlockSpec((1, lc, D_INNER), lambda i, j: (i, j, 0)),
                   pl.BlockSpec((1, D_STATE, D_INNER), lambda i, j: (i, 0, 0))],
        out_shape=[jax.ShapeDtypeStruct((b, l, D_INNER), BF16),
                   jax.ShapeDtypeStruct((b, D_STATE, D_INNER), F32)],
        scratch_shapes=[pltpu.VMEM((lc + CONV_HIST, CONV_DIM), F32), pltpu.VMEM((D_STATE, D_INNER), F32)],
        compiler_params=_params(("parallel", "arbitrary")),
    )(xbc, z, dt_raw, dtt, past, h0t, conv_w, conv_b.reshape(1, -1),
      dt_bias.reshape(1, nh), dt_bias.reshape(nh, 1), a_log.reshape(1, nh), a_log.reshape(nh, 1),
      jnp.repeat(d_skip, HEAD_DIM_S).reshape(1, -1), norm_w.reshape(1, -1))
    return o, hout


def _outproj_kernel(x_ref, oa_ref, os_ref, gt_ref, sc_ref, sh_ref, g_ref, wa_ref, ws_ref, rw_ref, rb_ref,
                    x1_ref, h2_ref, lg_ref):
    nb, tl, d = x_ref.shape
    oa = oa_ref[...].reshape(nb * tl, -1)
    os_ = os_ref[...].reshape(nb * tl, -1)
    mix = jnp.dot(oa, wa_ref[...], preferred_element_type=F32) + jnp.dot(os_, ws_ref[...], preferred_element_type=F32)
    x1 = x_ref[...] + gt_ref[...] * mix.reshape(nb, tl, d)
    x1_ref[...] = x1
    ms = jnp.mean(x1 * x1, axis=-1, keepdims=True)
    h2 = x1 * lax.rsqrt(ms + EPS) * g_ref[...]
    h2 = h2 * (1.0 + sc_ref[...]) + sh_ref[...]
    h2 = h2.reshape(nb * tl, d)
    _store_row_tiles(h2_ref, h2)
    h_hi, h_lo = _split2(h2)
    w_hi, w_lo = rw_ref[0], rw_ref[1]
    lg = (jnp.dot(h_hi, w_hi, preferred_element_type=F32) + jnp.dot(h_lo, w_hi, preferred_element_type=F32)
          + jnp.dot(h_hi, w_lo, preferred_element_type=F32)) + rb_ref[...]
    lg_ref[...] = lg.reshape(nb, tl, -1)


def _outproj(x, oa, os_, gt, sc, sh, g, wa, ws, rw, rb):
    b, l, d = x.shape
    nb, tl = _row_blocking(b, l, ROW_TILE)
    row = lambda n: pl.BlockSpec((nb, tl, n), lambda i, j: (i, j, 0))
    mod = pl.BlockSpec((nb, 1, d), lambda i, j: (i, 0, 0))
    const = lambda a: pl.BlockSpec(a.shape, lambda i, j: tuple(0 for _ in a.shape))
    g3 = g.reshape(1, 1, d)
    return pl.pallas_call(
        _outproj_kernel,
        grid=(b // nb, l // tl),
        in_specs=[row(d), row(D_ATTN), row(D_INNER), mod, mod, mod, const(g3), const(wa), const(ws),
                  const(rw), const(rb)],
        out_specs=[row(d), pl.BlockSpec((nb * tl * ROW_TILE_SUBLANES, LANES), lambda i, j: (i * (l // tl) + j, 0)),
                   row(LANES)],
        out_shape=[jax.ShapeDtypeStruct((b, l, d), F32), jax.ShapeDtypeStruct((b * l * ROW_TILE_SUBLANES, LANES), F32),
                   jax.ShapeDtypeStruct((b, l, LANES), F32)],
        compiler_params=_params(("parallel", "parallel")),
    )(x, oa, os_, gt.reshape(b, 1, d), sc.reshape(b, 1, d), sh.reshape(b, 1, d), g3, wa, ws, rw, rb)


def _route_kernel(lg_ref, idx_ref, gate_ref, cnt_ref, carry_scr):
    @pl.when(pl.program_id(0) == 0)
    def _():
        carry_scr[...] = jnp.zeros(carry_scr.shape, F32)

    v = lg_ref[...]
    tr = v.shape[0]
    lane = lax.broadcasted_iota(jnp.int32, (tr, LANES), 1)
    lane_f = lane.astype(F32)
    tops, idxs = [], []
    onehot = jnp.zeros((tr, LANES), F32)
    for _ in range(TOP_K):
        m = jnp.max(v, axis=-1, keepdims=True)
        idx = jnp.min(jnp.where(v == m, lane_f, float(LANES)), axis=-1, keepdims=True)
        hit = lane_f == idx
        v = jnp.where(hit, -jnp.inf, v)
        onehot = onehot + hit.astype(F32)
        tops.append(m)
        idxs.append(idx)
    es = [jnp.exp(t - tops[0]) for t in tops]
    denom = es[0] + es[1] + es[2] + es[3]

    row = lax.broadcasted_iota(jnp.int32, (tr, tr), 0)
    col = lax.broadcasted_iota(jnp.int32, (tr, tr), 1)
    before = jnp.dot((col < row).astype(BF16), onehot.astype(BF16), preferred_element_type=F32) + carry_scr[...]

    idx_out = jnp.zeros((tr, LANES), F32)
    gate_out = jnp.zeros((tr, LANES), F32)
    for k in range(TOP_K):
        rank = jnp.sum(jnp.where(lane_f == idxs[k], before, 0.0), axis=-1, keepdims=True)
        idx_out = jnp.where(lane == k, idxs[k], idx_out)
        idx_out = jnp.where(lane == TOP_K + k, rank, idx_out)
        gate_out = jnp.where(lane == k, es[k] / denom, gate_out)
    idx_ref[...] = idx_out.astype(jnp.int32)
    gate_ref[...] = gate_out
    carry_scr[...] = carry_scr[...] + jnp.sum(onehot, axis=0, keepdims=True)
    cnt_ref[...] = carry_scr[...]


def _route(logits):
    t, n = logits.shape
    tr = min(ROUTE_TILE, t)
    assert t % tr == 0
    return pl.pallas_call(
        _route_kernel,
        grid=(t // tr,),
        in_specs=[pl.BlockSpec((tr, n), lambda i: (i, 0))],
        out_specs=[pl.BlockSpec((tr, n), lambda i: (i, 0)), pl.BlockSpec((tr, n), lambda i: (i, 0)),
                   pl.BlockSpec((1, n), lambda i: (0, 0))],
        out_shape=[jax.ShapeDtypeStruct((t, n), jnp.int32), jax.ShapeDtypeStruct((t, n), F32),
                   jax.ShapeDtypeStruct((1, n), F32)],
        scratch_shapes=[pltpu.VMEM((1, n), F32)],
        compiler_params=_params(("arbitrary",)),
    )(logits)


def _row_tile(ref, r):
    start = r * ROW_TILE_SUBLANES
    if not isinstance(r, int):
        start = pl.multiple_of(start, ROW_TILE_SUBLANES)
    return ref.at[pl.ds(start, ROW_TILE_SUBLANES), :]


def _store_row_tiles(ref, x, tile0=0):
    rows = x.shape[0]
    for s in range(x.shape[1] // LANES):
        ref[pl.ds(tile0 + s, rows, stride=ROW_TILE_SUBLANES), :] = x[:, s * LANES:(s + 1) * LANES]


def _load_row_tiles(ref, row0, rows):
    return jnp.concatenate(
        [ref[pl.ds(row0 * ROW_TILE_SUBLANES + s, rows, stride=ROW_TILE_SUBLANES), :] for s in range(ROW_TILE_SUBLANES)],
        axis=1)


def _gather_rows(idx_ref, src_hbm, dst, sem, n_rows):
    group = SUBLANES
    assert n_rows % group == 0

    def body(g, carry):
        for u in range(group):
            r = g * group + u
            pltpu.make_async_copy(_row_tile(src_hbm, idx_ref[0, 0, r]), _row_tile(dst, r), sem).start(priority=u % 2)
        return carry
    lax.fori_loop(0, n_rows // group, body, 0)


def _wait_rows(src_hbm, dst, sem, n_rows):
    pltpu.make_async_copy(src_hbm.at[pl.ds(0, n_rows * ROW_TILE_SUBLANES), :], dst, sem).wait()


EXPERT_COL_CHUNK = 256


GATHER_BUFS = 3

def _dispatch_kernel(ends_ref, pad_ref, dest_ref, h_hbm, xs_hbm, zbuf, hbuf, zsem, fsem, sem, *, n_slots):
    i = pl.program_id(0)
    n = pl.num_programs(0)
    m = zbuf.shape[0] // ROW_TILE_SUBLANES
    rows = dest_ref.shape[2]
    td = rows // TOP_K
    slot = lax.rem(i, GATHER_BUFS)

    def fetch(step, s):
        start = pl.multiple_of(step * (td * ROW_TILE_SUBLANES), td * ROW_TILE_SUBLANES)
        return pltpu.make_async_copy(h_hbm.at[pl.ds(start, td * ROW_TILE_SUBLANES), :], hbuf.at[s], fsem.at[s])

    def wait_rows_of(s):
        for _ in range(TOP_K):
            pltpu.make_async_copy(hbuf.at[s], xs_hbm.at[pl.ds(0, td * ROW_TILE_SUBLANES), :], sem.at[s]).wait()

    def zero_block(first_slot):
        start = pl.multiple_of(first_slot * ROW_TILE_SUBLANES, ROW_TILE_SUBLANES)
        return pltpu.make_async_copy(zbuf, xs_hbm.at[pl.ds(start, zbuf.shape[0]), :], zsem)

    @pl.when(i == 0)
    def _():
        zbuf[...] = jnp.zeros(zbuf.shape, zbuf.dtype)
        jobs = [(pad_ref[e] > 0, ends_ref[e] - m) for e in range(N_EXPERTS)]
        jobs += [(ends_ref[N_EXPERTS - 1] + j * m < n_slots, ends_ref[N_EXPERTS - 1] + j * m)
                 for j in range(N_EXPERTS + 1)]
        for cond, first in jobs:
            @pl.when(cond)
            def _():
                zero_block(first).start()
        for cond, first in jobs:
            @pl.when(cond)
            def _():
                zero_block(first).wait()
        fetch(0, 0).start()

    @pl.when(i >= 2)
    def _():
        wait_rows_of(lax.rem(i + 1, GATHER_BUFS))

    @pl.when(i + 1 < n)
    def _():
        fetch(i + 1, lax.rem(i + 1, GATHER_BUFS)).start()

    fetch(i, slot).wait()
    for s in range(GATHER_BUFS):
        @pl.when(slot == s)
        def _():
            for j in range(rows):
                pltpu.make_async_copy(_row_tile(hbuf.at[s], j // TOP_K), _row_tile(xs_hbm, dest_ref[0, 0, j]),
                                      sem.at[s]).start(priority=j % 2)

    @pl.when(i == n - 1)
    def _():
        @pl.when(i >= 1)
        def _():
            wait_rows_of(lax.rem(i + 2, GATHER_BUFS))
        wait_rows_of(slot)


def _dispatch(dest, ends_p, padded, h2, m, n_slots):
    t = dest.shape[0]
    td = min(DISPATCH_TILE, t)
    assert t % td == 0
    n_steps = t // td
    grid_spec = pltpu.PrefetchScalarGridSpec(
        num_scalar_prefetch=2,
        grid=(n_steps,),
        in_specs=[pl.BlockSpec((1, 1, TOP_K * td), lambda i, e, p: (i, 0, 0), memory_space=pltpu.SMEM),
                  pl.BlockSpec(memory_space=pl.ANY)],
        out_specs=pl.BlockSpec(memory_space=pl.ANY),
        scratch_shapes=[pltpu.VMEM((m * ROW_TILE_SUBLANES, LANES), F32),
                        pltpu.VMEM((GATHER_BUFS, td * ROW_TILE_SUBLANES, LANES), F32),
                        pltpu.SemaphoreType.DMA(()), pltpu.SemaphoreType.DMA((GATHER_BUFS,)),
                        pltpu.SemaphoreType.DMA((GATHER_BUFS,))],
    )
    return pl.pallas_call(
        functools.partial(_dispatch_kernel, n_slots=n_slots),
        grid_spec=grid_spec,
        out_shape=jax.ShapeDtypeStruct((n_slots * ROW_TILE_SUBLANES, LANES), F32),
        compiler_params=_params(("arbitrary",)),
    )(ends_p, padded, dest.reshape(n_steps, 1, TOP_K * td), h2)


def _experts_kernel(be_ref, used_ref, x_ref, wgu_ref, bgu_ref, wdn_ref, bdn_ref, o_ref):
    m = x_ref.shape[0] // ROW_TILE_SUBLANES
    cw = EXPERT_COL_CHUNK
    tiles_per_piece = cw // LANES
    n_ff, n_out = D_FF // cw, ROW_TILE_SUBLANES // tiles_per_piece
    in_use = pl.program_id(0) < used_ref[0]

    @pl.when(in_use)
    def _():
        x = _load_row_tiles(x_ref, 0, m).astype(BF16)
        acts = []
        for c in range(n_ff):
            g = (jnp.dot(x, wgu_ref[0, :, c * cw:(c + 1) * cw], preferred_element_type=F32)
                 + bgu_ref[0, :, c * cw:(c + 1) * cw])
            u = (jnp.dot(x, wgu_ref[0, :, D_FF + c * cw:D_FF + (c + 1) * cw], preferred_element_type=F32)
                 + bgu_ref[0, :, D_FF + c * cw:D_FF + (c + 1) * cw])
            gate = jnp.minimum(g, SWIGLU_LIMIT)
            up = jnp.clip(u, -SWIGLU_LIMIT, SWIGLU_LIMIT)
            acts.append(((up + 1.0) * gate * jax.nn.sigmoid(SWIGLU_ALPHA * gate)).astype(BF16))
        act = jnp.concatenate(acts, axis=1)
        for c in range(n_out):
            y = (jnp.dot(act, wdn_ref[0, :, c * cw:(c + 1) * cw], preferred_element_type=F32)
                 + bdn_ref[0, :, c * cw:(c + 1) * cw])
            _store_row_tiles(o_ref, y, tile0=c * tiles_per_piece)

    @pl.when(jnp.logical_not(in_use))
    def _():
        o_ref[...] = jnp.zeros(o_ref.shape, o_ref.dtype)


def _experts(block_e, n_used, xs, wgu, bgu, wdn, bdn, m):
    n_blocks = block_e.shape[0]
    d = D_MODEL
    mt = m * ROW_TILE_SUBLANES
    grid_spec = pltpu.PrefetchScalarGridSpec(
        num_scalar_prefetch=2,
        grid=(n_blocks,),
        in_specs=[pl.BlockSpec((mt, LANES), lambda i, be, nu: (jnp.minimum(i, nu[0] - 1), 0)),
                  pl.BlockSpec((1, d, 2 * D_FF), lambda i, be, nu: (be[i], 0, 0)),
                  pl.BlockSpec((1, 1, 2 * D_FF), lambda i, be, nu: (be[i], 0, 0)),
                  pl.BlockSpec((1, D_FF, d), lambda i, be, nu: (be[i], 0, 0)),
                  pl.BlockSpec((1, 1, d), lambda i, be, nu: (be[i], 0, 0))],
        out_specs=pl.BlockSpec((mt, LANES), lambda i, be, nu: (i, 0)),
    )
    return pl.pallas_call(
        _experts_kernel,
        grid_spec=grid_spec,
        out_shape=jax.ShapeDtypeStruct((n_blocks * mt, LANES), F32),
        compiler_params=_params(("arbitrary",)),
    )(block_e, n_used, xs, wgu, bgu.reshape(N_EXPERTS, 1, -1), wdn, bdn.reshape(N_EXPERTS, 1, -1))


def _combine_kernel(idx0_ref, idx1_ref, idxn_ref, y_hbm, x1_ref, gate_ref, gt_ref, g_ref, o_ref, ybuf, sem):
    i = pl.program_id(0)
    n = pl.num_programs(0)
    nb, tl, d = x1_ref.shape
    tc = nb * tl
    slot = lax.rem(i, GATHER_BUFS)
    nxt = lax.rem(i + 2, GATHER_BUFS)

    @pl.when(i == 0)
    def _():
        _gather_rows(idx0_ref, y_hbm, ybuf.at[0], sem.at[0], TOP_K * tc)
        _gather_rows(idx1_ref, y_hbm, ybuf.at[1], sem.at[1], TOP_K * tc)

    _wait_rows(y_hbm, ybuf.at[slot], sem.at[slot], TOP_K * tc)

    def prefetch(k):
        for r in range(k * tc, (k + 1) * tc):
            pltpu.make_async_copy(_row_tile(y_hbm, idxn_ref[0, 0, r]), _row_tile(ybuf.at[nxt], r),
                                  sem.at[nxt]).start(priority=r % 2)

    gates = gate_ref[...]
    y = gates[:, 0:1] * _load_row_tiles(ybuf.at[slot], 0, tc)
    prefetch(0)
    for k in range(1, TOP_K):
        y = y + gates[:, k:k + 1] * _load_row_tiles(ybuf.at[slot], k * tc, tc)
        prefetch(k)
    x2 = x1_ref[...] + gt_ref[...] * y.reshape(nb, tl, d)
    ms = jnp.mean(x2 * x2, axis=-1, keepdims=True)
    o_ref[...] = x2 * lax.rsqrt(ms + EPS) * g_ref[...]

    @pl.when(i == n - 1)
    def _():
        for ahead in (1, 2):
            s = lax.rem(i + ahead, GATHER_BUFS)
            _wait_rows(y_hbm, ybuf.at[s], sem.at[s], TOP_K * tc)


def _combine(dest, y_slots, x1, gates, gt, g_final):
    b, l, d = x1.shape
    nb, tl = _row_blocking(b, l, COMBINE_TILE)
    tc = nb * tl
    t = b * l
    n_steps = t // tc
    idx = jnp.swapaxes(dest.reshape(n_steps, tc, TOP_K), 1, 2).reshape(n_steps, 1, TOP_K * tc)
    steps_per_batch_row = l // tl
    tok = lambda i: (i // steps_per_batch_row, i % steps_per_batch_row, 0)
    smem = lambda f: pl.BlockSpec((1, 1, TOP_K * tc), f, memory_space=pltpu.SMEM)
    return pl.pallas_call(
        _combine_kernel,
        grid=(n_steps,),
        in_specs=[smem(lambda i: (0, 0, 0)),
                  smem(lambda i: (min(1, n_steps - 1), 0, 0)),
                  smem(lambda i: (jnp.minimum(i + 2, n_steps - 1), 0, 0)),
                  pl.BlockSpec(memory_space=pl.ANY),
                  pl.BlockSpec((nb, tl, d), tok),
                  pl.BlockSpec((tc, LANES), lambda i: (i, 0)),
                  pl.BlockSpec((nb, 1, d), lambda i: (i // steps_per_batch_row, 0, 0)),
                  pl.BlockSpec((1, 1, d), lambda i: (0, 0, 0))],
        out_specs=pl.BlockSpec((nb, tl, d), tok),
        out_shape=jax.ShapeDtypeStruct((b, l, d), F32),
        scratch_shapes=[pltpu.VMEM((GATHER_BUFS, TOP_K * tc * ROW_TILE_SUBLANES, LANES), F32),
                        pltpu.SemaphoreType.DMA((GATHER_BUFS,))],
        compiler_params=_params(("arbitrary",)),
    )(idx, idx, idx, y_slots, x1, gates, gt.reshape(b, 1, d), g_final.reshape(1, 1, d))


def _moe(h2, logits, x1, gt_m, g_final, wts):
    b, l, d = x1.shape
    t = b * l
    m = MOE_ROWS if t * TOP_K >= 4 * N_EXPERTS * MOE_ROWS else MOE_ROWS_SMALL
    idx, gates, counts = _route(logits.reshape(t, LANES))
    top_e = idx[:, :TOP_K]
    rank = idx[:, TOP_K:2 * TOP_K]
    counts = counts[0, :N_EXPERTS].astype(jnp.int32)
    padded = (counts + m - 1) // m * m
    ends_p = jnp.cumsum(padded)
    starts_p = ends_p - padded
    experts = jnp.arange(N_EXPERTS, dtype=jnp.int32)
    start_of = jnp.sum(jnp.where(top_e[..., None] == experts, starts_p, 0), axis=-1)
    dest = start_of + rank
    n_blocks = (t * TOP_K + N_EXPERTS * (m - 1) + m - 1) // m
    block_start = jnp.arange(n_blocks, dtype=jnp.int32)[:, None] * m
    block_e = jnp.minimum(jnp.sum((ends_p[None, :] <= block_start).astype(jnp.int32), axis=1), N_EXPERTS - 1)
    xs = _dispatch(dest, ends_p, padded, h2, m, n_blocks * m)
    n_used = (ends_p[N_EXPERTS - 1:] // m).astype(jnp.int32)
    y_slots = _experts(block_e, n_used, xs, *wts, m)
    return _combine(dest, y_slots, x1, gates, gt_m, g_final)


def _layer(x, mod, k_past, v_past, logf_past, conv_past, ssm_past, p, g_final):
    b, l, d = x.shape
    sh_a, sc_a, gt_a, sh_m, sc_m, gt_m = jnp.split(mod, 6, axis=-1)
    q, kf, vf, kb, vb, z, xbc, sm = _inproj(x, p['g_mix'], sc_a, sh_a, *p['w_in'])
    dt_raw = sm[:, :, N_HEADS_A:N_HEADS_A + N_HEADS_S]

    if k_past is None:
        logf, qx, kx = _cumsum_bias(sm, p['b_f'])
        o_a = _attention(q, qx, kb, kx, vb)
        conv_past = jnp.zeros((b, CONV_W - 1, CONV_DIM), F32)
        h0t = jnp.zeros((b, D_STATE, D_INNER), F32)
    else:
        lp = k_past.shape[1]
        past_t = jnp.swapaxes(logf_past, 1, 2).reshape(b * N_HEADS_A, lp)
        cumt_p = _cumsum_lanes(past_t).reshape(b, N_HEADS_A, lp)
        carry = jnp.pad(cumt_p[:, :, lp - 1], ((0, 0), (0, LANES - N_HEADS_A))).reshape(b, 1, LANES)
        lpad = -(-l // LANES) * LANES
        sm_pad = jnp.pad(sm, ((0, 0), (0, lpad - l), (0, 0)))
        logf, cum_n, cumt_n = _cumsum(sm_pad, p['b_f'], carry)
        logf = logf[:, :l]
        o_a = _attention_cached(q, kb, vb, k_past, v_past, cum_n, cumt_p, cumt_n)
        h0t = jnp.swapaxes(ssm_past.reshape(b, D_INNER, D_STATE), 1, 2)
    o_s, hout = _ssd(xbc, z, dt_raw, conv_past, h0t, p['conv_w'], p['conv_b'], p['dt_bias'], p['a_log'],
                     p['d_skip'], p['ssd_norm_w'])
    x1, h2, logits = _outproj(x, o_a, o_s, gt_a, sc_m, sh_m, p['g_ffn'], p['w_out_a'], p['w_out_s'],
                              p['router_w'], p['router_b'])
    y = _moe(h2, logits, x1, gt_m, g_final, p['experts'])

    assert l >= CONV_W - 1
    conv_new = xbc[:, l - (CONV_W - 1):]
    ssm_new = jnp.swapaxes(hout, 1, 2).reshape(b, N_HEADS_S, HEAD_DIM_S, D_STATE)
    return (y, kf.reshape(b, l, N_HEADS_A, HEAD_DIM_A), vf.reshape(b, l, N_HEADS_A, HEAD_DIM_A),
            logf[:, :, :N_HEADS_A], conv_new, ssm_new)


def kernel(x_prompt, x_sample, cache_k, cache_v, cache_logf, state_conv, state_ssm, c_prompt, c_sample, w_ada, b_ada, g_mix, w_in, b_f, conv_w, conv_b, dt_bias, a_log, d_skip, ssd_norm_w, w_out, g_ffn, router_w, router_b, w_gate_up, b_gate_up, w_down, b_down, g_final):
    assert w_ada.shape[0] == 1, "single-layer operation"
    bp = x_prompt.shape[0]
    w = w_in[0]
    q_end, k_end, v_end = D_ATTN, 2 * D_ATTN, 3 * D_ATTN
    f_end = v_end + N_HEADS_A
    z_end = f_end + D_INNER
    xbc_end = z_end + CONV_DIM
    w_small = jnp.concatenate(
        [w[:, v_end:f_end], w[:, xbc_end:], jnp.zeros((D_MODEL, LANES - N_HEADS_A - N_HEADS_S), F32)], axis=1)
    cast = lambda a: a.astype(BF16)
    p = {
        'g_mix': g_mix[0],
        'w_in': (cast(w[:, :q_end]), cast(w[:, q_end:k_end]), cast(w[:, k_end:v_end]), cast(w[:, f_end:z_end]),
                 cast(w[:, z_end:xbc_end]), cast(w_small)),
        'b_f': jnp.pad(b_f[0], (0, LANES - N_HEADS_A)).reshape(1, LANES),
        'conv_w': conv_w[0], 'conv_b': conv_b[0], 'dt_bias': dt_bias[0], 'a_log': a_log[0],
        'd_skip': d_skip[0], 'ssd_norm_w': ssd_norm_w[0],
        'w_out_a': cast(w_out[0, :D_ATTN]), 'w_out_s': cast(w_out[0, D_ATTN:]),
        'g_ffn': g_ffn[0],
        'router_w': jnp.stack(_split2(jnp.pad(router_w[0], ((0, 0), (0, LANES - N_EXPERTS))))),
        'router_b': jnp.pad(router_b[0], (0, LANES - N_EXPERTS), constant_values=-jnp.inf).reshape(1, LANES),
        'experts': (cast(w_gate_up[0]), b_gate_up[0], cast(w_down[0]), b_down[0]),
    }
    mod = _adaln(jnp.concatenate([c_prompt, c_sample], axis=0), w_ada[0], b_ada[0])
    outs_p = _layer(x_prompt, mod[:bp], None, None, None, None, None, p, g_final)
    outs_s = _layer(x_sample, mod[bp:], cache_k[0], cache_v[0], cache_logf[0], state_conv[0], state_ssm[0], p, g_final)
    stack = lambda a: a[None]
    return (outs_p[0], outs_s[0]) + tuple(stack(a) for a in outs_p[1:]) + tuple(stack(a) for a in outs_s[1:])
```

```python
import functools
import math

import jax
import jax.numpy as jnp
from jax import lax
from jax.experimental import pallas as pl
from jax.experimental.pallas import tpu as pltpu

F32 = jnp.float32
BF16 = jnp.bfloat16
HIGHEST = lax.Precision.HIGHEST

D_MODEL = 1024
N_HEADS_A = 8
HEAD_DIM_A = 128
D_ATTN = N_HEADS_A * HEAD_DIM_A
D_INNER = 1024
HEAD_DIM_S = 64
N_HEADS_S = D_INNER // HEAD_DIM_S
N_GROUPS_S = 2
D_STATE = 128
CONV_W = 4
CONV_DIM = D_INNER + 2 * N_GROUPS_S * D_STATE
N_EXPERTS = 32
TOP_K = 4
D_FF = 1024
SWIGLU_LIMIT = 7.0
SWIGLU_ALPHA = 1.702
EPS = 1e-5
LANES = 128
SUBLANES = 8
ROW_TILE_SUBLANES = D_MODEL // LANES
assert ROW_TILE_SUBLANES == SUBLANES
VMEM_LIMIT = 56 * 1024 * 1024
LOG2E = 1.4426950408889634
ATT_HEADS_PER_STEP = 4

ADALN_COL_TILE = 512
ROW_TILE = 512
ATT_TILE = 512
DEC_KV_TILE = 2048
CUM_TILE = 512
SSD_CHUNK = 128
SSD_STEP_CHUNKS = 2
ROUTE_TILE = 256
MOE_ROWS = 512
MOE_ROWS_SMALL = 128
COMBINE_TILE = 256
DISPATCH_TILE = 256

def _params(sem):
    return pltpu.CompilerParams(dimension_semantics=sem, vmem_limit_bytes=VMEM_LIMIT)


def _row_blocking(b, l, tile):
    if l >= tile:
        assert l % tile == 0
        return 1, tile
    nb = min(b, max(1, tile // l))
    while b % nb:
        nb -= 1
    return nb, l


def _split3(x):
    hi = x.astype(BF16)
    r1 = x - hi.astype(F32)
    mid = r1.astype(BF16)
    lo = (r1 - mid.astype(F32)).astype(BF16)
    return hi, mid, lo


def _split2(x):
    hi = x.astype(BF16)
    return hi, (x - hi.astype(F32)).astype(BF16)


def _tri_dot(tri, x):
    hi, mid, lo = _split3(x)
    d = lambda p: jnp.dot(tri, p, preferred_element_type=F32)
    return d(hi) + d(mid) + d(lo)


def _dot_tri(x, tri):
    hi, mid, lo = _split3(x)
    d = lambda p: jnp.dot(p, tri, preferred_element_type=F32)
    return d(hi) + d(mid) + d(lo)


def _silu(x):
    return x * jax.nn.sigmoid(x)


def _softplus(x):
    return jnp.maximum(x, 0.0) + jnp.log1p(jnp.exp(-jnp.abs(x)))


def _log_sigmoid(x):
    return jnp.minimum(x, 0.0) - jnp.log1p(jnp.exp(-jnp.abs(x)))


def _adaln_kernel(c_ref, w_ref, b_ref, o_ref):
    s = _silu(c_ref[...])
    o_ref[...] = jnp.dot(s, w_ref[...], precision=HIGHEST, preferred_element_type=F32) + b_ref[...]


def _adaln(c, w, b):
    m, d = c.shape
    n = w.shape[1]
    tn = ADALN_COL_TILE
    assert n % tn == 0
    return pl.pallas_call(
        _adaln_kernel,
        grid=(n // tn,),
        in_specs=[pl.BlockSpec((m, d), lambda j: (0, 0)),
                  pl.BlockSpec((d, tn), lambda j: (0, j)),
                  pl.BlockSpec((1, tn), lambda j: (0, j))],
        out_specs=pl.BlockSpec((m, tn), lambda j: (0, j)),
        out_shape=jax.ShapeDtypeStruct((m, n), F32),
        compiler_params=_params(("parallel",)),
    )(c, w, b.reshape(1, n))


def _inproj_kernel(x_ref, g_ref, sc_ref, sh_ref, wq_ref, wk_ref, wv_ref, wz_ref, wx_ref, ws_ref,
                   q_ref, kf_ref, vf_ref, kb_ref, vb_ref, z_ref, xbc_ref, sm_ref):
    x = x_ref[...]
    nb, tl, d = x.shape
    ms = jnp.mean(x * x, axis=-1, keepdims=True)
    h = x * lax.rsqrt(ms + EPS) * g_ref[...]
    h = h * (1.0 + sc_ref[...]) + sh_ref[...]
    hb = h.reshape(nb * tl, d).astype(BF16)

    def mm(w_ref):
        return jnp.dot(hb, w_ref[...], preferred_element_type=F32)

    q = mm(wq_ref) * (LOG2E / math.sqrt(HEAD_DIM_A))
    q_ref[...] = q.astype(BF16).reshape(nb, tl, -1)
    k = mm(wk_ref)
    kf_ref[...] = k.reshape(nb, tl, -1)
    kb_ref[...] = k.astype(BF16).reshape(nb, tl, -1)
    v = mm(wv_ref)
    vf_ref[...] = v.reshape(nb, tl, -1)
    vb_ref[...] = v.astype(BF16).reshape(nb, tl, -1)
    z_ref[...] = mm(wz_ref).reshape(nb, tl, -1)
    xbc_ref[...] = mm(wx_ref).reshape(nb, tl, -1)
    sm_ref[...] = mm(ws_ref).reshape(nb, tl, -1)


def _inproj(x, g, sc, sh, wq, wk, wv, wz, wx, ws):
    b, l, d = x.shape
    nb, tl = _row_blocking(b, l, ROW_TILE)
    grid = (b // nb, l // tl)
    row = lambda n: pl.BlockSpec((nb, tl, n), lambda i, j: (i, j, 0))
    mod = pl.BlockSpec((nb, 1, d), lambda i, j: (i, 0, 0))
    wspec = lambda w: pl.BlockSpec(w.shape, lambda i, j: (0, 0), pipeline_mode=pl.Buffered(1))
    outs = [(D_ATTN, BF16), (D_ATTN, F32), (D_ATTN, F32), (D_ATTN, BF16), (D_ATTN, BF16),
            (D_INNER, F32), (CONV_DIM, F32), (LANES, F32)]
    return pl.pallas_call(
        _inproj_kernel,
        grid=grid,
        in_specs=[row(d), pl.BlockSpec((1, 1, d), lambda i, j: (0, 0, 0)), mod, mod,
                  wspec(wq), wspec(wk), wspec(wv), wspec(wz), wspec(wx), wspec(ws)],
        out_specs=[row(n) for n, _ in outs],
        out_shape=[jax.ShapeDtypeStruct((b, l, n), dt) for n, dt in outs],
        compiler_params=_params(("parallel", "parallel")),
    )(x, g.reshape(1, 1, d), sc.reshape(b, 1, d), sh.reshape(b, 1, d), wq, wk, wv, wz, wx, ws)


def _running_sum(v_ref, bias_ref, carry_ref, carry_scr):
    @pl.when(pl.program_id(1) == 0)
    def _():
        carry_scr[...] = carry_ref[0]

    v = v_ref[0]
    tl = v.shape[0]
    lf = _log_sigmoid(v + bias_ref[...])
    row = lax.broadcasted_iota(jnp.int32, (tl, tl), 0)
    col = lax.broadcasted_iota(jnp.int32, (tl, tl), 1)
    tri = (row >= col).astype(BF16)
    cs = _tri_dot(tri, lf) + carry_scr[...]
    carry_scr[...] = cs[tl - 1:tl, :]
    return lf, cs


def _cumsum_kernel(v_ref, bias_ref, carry_ref, logf_ref, cum_ref, cumt_ref, carry_scr):
    lf, cs = _running_sum(v_ref, bias_ref, carry_ref, carry_scr)
    logf_ref[0] = lf
    cum_ref[0] = cs
    cumt_ref[0] = cs.T[:N_HEADS_A, :]


def _cumsum(vals, bias, carry):
    b, l, n = vals.shape
    tl = min(CUM_TILE, l)
    assert l % tl == 0 and tl % LANES == 0
    return pl.pallas_call(
        _cumsum_kernel,
        grid=(b, l // tl),
        in_specs=[pl.BlockSpec((1, tl, n), lambda i, j: (i, j, 0)),
                  pl.BlockSpec((1, n), lambda i, j: (0, 0)),
                  pl.BlockSpec((1, 1, n), lambda i, j: (i, 0, 0))],
        out_specs=[pl.BlockSpec((1, tl, n), lambda i, j: (i, j, 0)),
                   pl.BlockSpec((1, tl, n), lambda i, j: (i, j, 0)),
                   pl.BlockSpec((1, N_HEADS_A, tl), lambda i, j: (i, 0, j))],
        out_shape=[jax.ShapeDtypeStruct((b, l, n), F32),
                   jax.ShapeDtypeStruct((b, l, n), F32),
                   jax.ShapeDtypeStruct((b, N_HEADS_A, l), F32)],
        scratch_shapes=[pltpu.VMEM((1, n), F32)],
        compiler_params=_params(("parallel", "arbitrary")),
    )(vals, bias, carry)


def _cumsum_lanes_kernel(x_ref, o_ref, carry_scr):
    @pl.when(pl.program_id(0) == 0)
    def _():
        carry_scr[...] = jnp.zeros(carry_scr.shape, F32)

    x = x_ref[...]
    tl = x.shape[1]
    row = lax.broadcasted_iota(jnp.int32, (tl, tl), 0)
    col = lax.broadcasted_iota(jnp.int32, (tl, tl), 1)
    cs = _dot_tri(x, (row <= col).astype(BF16)) + carry_scr[...]
    o_ref[...] = cs
    carry_scr[...] = cs[:, tl - 1:tl]


def _cumsum_lanes(x):
    rows, l = x.shape
    tl = min(CUM_TILE, l)
    assert l % tl == 0 and rows % SUBLANES == 0
    return pl.pallas_call(
        _cumsum_lanes_kernel,
        grid=(l // tl,),
        in_specs=[pl.BlockSpec((rows, tl), lambda j: (0, j))],
        out_specs=pl.BlockSpec((rows, tl), lambda j: (0, j)),
        out_shape=jax.ShapeDtypeStruct((rows, l), F32),
        scratch_shapes=[pltpu.VMEM((rows, 1), F32)],
        compiler_params=_params(("arbitrary",)),
    )(x)


N_BIAS_TERMS = 3


def _bias_placement():
    rows = jnp.arange(N_BIAS_TERMS * LANES)
    n, h = rows // LANES, rows % LANES
    col = jnp.arange(D_ATTN)[None, :]
    valid = (h < N_HEADS_A)[:, None]
    eq = jnp.where(valid & (col == (h * LANES + n)[:, None]), 1.0, 0.0)
    ek = jnp.where(valid & (col == (h * LANES + N_BIAS_TERMS + n)[:, None]), -1.0, 0.0)
    lane = jnp.arange(D_ATTN) % LANES
    ones_q = jnp.where((lane >= N_BIAS_TERMS) & (lane < 2 * N_BIAS_TERMS), 1.0, 0.0)
    ones_k = jnp.where(lane < N_BIAS_TERMS, 1.0, 0.0)
    return eq.astype(BF16), ek.astype(BF16), ones_q.reshape(1, -1).astype(F32), ones_k.reshape(1, -1).astype(F32)


def _cumsum_bias_kernel(v_ref, bias_ref, carry_ref, eq_ref, ek_ref, oq_ref, ok_ref, logf_ref, qx_ref, kx_ref, carry_scr):
    lf, cs = _running_sum(v_ref, bias_ref, carry_ref, carry_scr)
    logf_ref[0] = lf
    terms = jnp.concatenate(_split3(cs * LOG2E), axis=1)
    qx_ref[0] = (jnp.dot(terms, eq_ref[...], preferred_element_type=F32) + oq_ref[...]).astype(BF16)
    kx_ref[0] = (jnp.dot(terms, ek_ref[...], preferred_element_type=F32) + ok_ref[...]).astype(BF16)


def _cumsum_bias(vals, bias):
    b, l, n = vals.shape
    tl = min(CUM_TILE, l)
    assert l % tl == 0
    row = lambda w: pl.BlockSpec((1, tl, w), lambda i, j: (i, j, 0))
    const = lambda a: pl.BlockSpec(a.shape, lambda i, j: (0, 0))
    placement = _bias_placement()
    return pl.pallas_call(
        _cumsum_bias_kernel,
        grid=(b, l // tl),
        in_specs=[row(n), pl.BlockSpec((1, n), lambda i, j: (0, 0)), pl.BlockSpec((1, 1, n), lambda i, j: (i, 0, 0))]
        + [const(a) for a in placement],
        out_specs=[row(n), row(D_ATTN), row(D_ATTN)],
        out_shape=[jax.ShapeDtypeStruct((b, l, n), F32), jax.ShapeDtypeStruct((b, l, D_ATTN), BF16),
                   jax.ShapeDtypeStruct((b, l, D_ATTN), BF16)],
        scratch_shapes=[pltpu.VMEM((1, n), F32)],
        compiler_params=_params(("parallel", "arbitrary")),
    )(vals, bias, jnp.zeros((b, 1, n), F32), *placement)


_NT = (((1,), (1,)), ((), ()))


def _softmax_step(s, v, m_prev, l_prev, acc_prev):
    m_new = jnp.maximum(m_prev, jnp.max(s, axis=-1, keepdims=True))
    p = jnp.exp2(s - m_new)
    alpha = jnp.exp2(m_prev - m_new)
    l_new = alpha * l_prev + jnp.sum(p, axis=-1, keepdims=True)
    acc_new = alpha * acc_prev + jnp.dot(p.astype(BF16), v, preferred_element_type=F32)
    return m_new, l_new, acc_new


def _attn_kernel(q_ref, qx_ref, k_ref, kx_ref, v_ref, o_ref, m_scr, acc_scr, *, tile, heads):
    i = pl.program_id(2)
    dh = HEAD_DIM_A
    n_chunks = tile // LANES
    ones = jnp.ones((tile, dh), BF16)
    m_scr[...] = jnp.full(m_scr.shape, -jnp.inf, F32)
    acc_scr[...] = jnp.zeros(acc_scr.shape, F32)
    q2 = [jnp.concatenate([q_ref[0, :, hh * dh:(hh + 1) * dh], qx_ref[0, :, hh * dh:(hh + 1) * dh]], axis=1)
          for hh in range(heads)]

    def update(j, mask):
        start = pl.multiple_of(j * tile, tile)
        rows = pl.ds(start, tile)
        for hh in range(heads):
            sl = slice(hh * dh, (hh + 1) * dh)
            k2 = jnp.concatenate([k_ref[0, rows, sl], kx_ref[0, rows, sl]], axis=1)
            v2 = jnp.concatenate([v_ref[0, rows, sl], ones], axis=1)
            s = lax.dot_general(q2[hh], k2, _NT, preferred_element_type=F32)
            if mask is not None:
                s = jnp.where(mask, s, -jnp.inf)
            chunks = [s[:, c * LANES:(c + 1) * LANES] for c in range(n_chunks)]
            cmax = functools.reduce(jnp.maximum, chunks)
            m_prev = m_scr[hh]
            m_new = jnp.maximum(m_prev, jnp.max(cmax, axis=-1, keepdims=True))
            alpha = jnp.exp2(m_prev - m_new)
            p = jnp.concatenate([jnp.exp2(c - m_new) for c in chunks], axis=1).astype(BF16)
            pv = jnp.dot(p, v2, preferred_element_type=F32)
            m_scr[hh] = m_new
            acc_scr[hh, :, :dh] = alpha * acc_scr[hh, :, :dh] + pv[:, :dh]
            acc_scr[hh, :, dh:] = alpha * acc_scr[hh, :, dh:] + pv[:, dh:]

    def body(j, carry):
        update(j, None)
        return carry

    lax.fori_loop(0, i, body, 0)
    row = lax.broadcasted_iota(jnp.int32, (tile, tile), 0)
    col = lax.broadcasted_iota(jnp.int32, (tile, tile), 1)
    update(i, col <= row)
    for hh in range(heads):
        o_ref[0, :, hh * dh:(hh + 1) * dh] = (acc_scr[hh, :, :dh] / acc_scr[hh, :, dh:]).astype(o_ref.dtype)


def _attention(q, qx, k, kx, v):
    b, l, _ = q.shape
    tile = ATT_TILE if l >= 2 * ATT_TILE else LANES
    assert l % tile == 0
    heads = ATT_HEADS_PER_STEP
    w = heads * HEAD_DIM_A
    qspec = pl.BlockSpec((1, tile, w), lambda bi, h, i: (bi, i, h))
    kspec = pl.BlockSpec((1, l, w), lambda bi, h, i: (bi, 0, h))
    return pl.pallas_call(
        functools.partial(_attn_kernel, tile=tile, heads=heads),
        grid=(b, N_HEADS_A // heads, l // tile),
        in_specs=[qspec, qspec, kspec, kspec, kspec],
        out_specs=qspec,
        out_shape=jax.ShapeDtypeStruct((b, l, D_ATTN), BF16),
        scratch_shapes=[pltpu.VMEM((heads, tile, LANES), F32), pltpu.VMEM((heads, tile, 2 * HEAD_DIM_A), F32)],
        compiler_params=_params(("parallel", "parallel", "arbitrary")),
    )(q, qx, k, kx, v)


def _attn_cached_kernel(q_ref, kn_ref, vn_ref, kc_ref, vc_ref, cumn_ref, cumtp_ref, cumtn_ref, o_ref,
                        m_scr, l_scr, acc_scr):
    j = pl.program_id(1)
    ld = q_ref.shape[1]
    dh = HEAD_DIM_A

    @pl.when(j == 0)
    def _():
        m_scr[...] = jnp.full(m_scr.shape, -jnp.inf, F32)
        l_scr[...] = jnp.zeros(l_scr.shape, F32)
        acc_scr[...] = jnp.zeros(acc_scr.shape, F32)

    def head_step(h, kh, vh, ck, mask):
        sl = slice(h * dh, (h + 1) * dh)
        s = lax.dot_general(q_ref[0, :, sl], kh, _NT, preferred_element_type=F32)
        s = s + (cumn_ref[0, :ld, h:h + 1] - ck) * LOG2E
        if mask is not None:
            s = jnp.where(mask, s, -jnp.inf)
        m, l, acc = _softmax_step(s, vh, m_scr[h], l_scr[h], acc_scr[:, sl])
        m_scr[h] = m
        l_scr[h] = l
        acc_scr[:, sl] = acc

    for h in range(N_HEADS_A):
        sl = slice(h * dh, (h + 1) * dh)
        rows = pl.ds(h, kc_ref.shape[1] // N_HEADS_A, stride=N_HEADS_A)
        head_step(h, kc_ref[0, rows, :].astype(BF16), vc_ref[0, rows, :].astype(BF16),
                  cumtp_ref[0, h:h + 1, :], None)

    @pl.when(j == pl.num_programs(1) - 1)
    def _():
        row = lax.broadcasted_iota(jnp.int32, (ld, ld), 0)
        col = lax.broadcasted_iota(jnp.int32, (ld, ld), 1)
        for h in range(N_HEADS_A):
            sl = slice(h * dh, (h + 1) * dh)
            head_step(h, kn_ref[0, :, sl], vn_ref[0, :, sl], cumtn_ref[0, h:h + 1, :ld], col <= row)
            o_ref[0, :, sl] = (acc_scr[:, sl] / l_scr[h]).astype(o_ref.dtype)


def _attention_cached(q, kn, vn, kc, vc, cum_new, cumt_past, cumt_new):
    b, ld, _ = q.shape
    lp = kc.shape[1]
    tk = min(DEC_KV_TILE, lp)
    assert lp % tk == 0
    lpad = cum_new.shape[1]
    new = pl.BlockSpec((1, ld, D_ATTN), lambda bi, j: (bi, 0, 0))
    past = pl.BlockSpec((1, tk * N_HEADS_A, HEAD_DIM_A), lambda bi, j: (bi, j, 0))
    kc = kc.reshape(b, lp * N_HEADS_A, HEAD_DIM_A)
    vc = vc.reshape(b, lp * N_HEADS_A, HEAD_DIM_A)
    return pl.pallas_call(
        _attn_cached_kernel,
        grid=(b, lp // tk),
        in_specs=[new, new, new, past, past,
                  pl.BlockSpec((1, lpad, LANES), lambda bi, j: (bi, 0, 0)),
                  pl.BlockSpec((1, N_HEADS_A, tk), lambda bi, j: (bi, 0, j)),
                  pl.BlockSpec((1, N_HEADS_A, lpad), lambda bi, j: (bi, 0, 0))],
        out_specs=new,
        out_shape=jax.ShapeDtypeStruct((b, ld, D_ATTN), BF16),
        scratch_shapes=[pltpu.VMEM((N_HEADS_A, ld, 1), F32), pltpu.VMEM((N_HEADS_A, ld, 1), F32),
                        pltpu.VMEM((ld, D_ATTN), F32)],
        compiler_params=_params(("parallel", "arbitrary")),
    )(q, kn, vn, kc, vc, cum_new, cumt_past, cumt_new)


def _expand_heads(a):
    r = a.shape[0]
    low = lax.broadcasted_iota(jnp.int32, (r, LANES), 1) < HEAD_DIM_S
    return jnp.concatenate(
        [jnp.where(low, a[:, 2 * j:2 * j + 1], a[:, 2 * j + 1:2 * j + 2]) for j in range(N_HEADS_S // 2)], axis=1)


CONV_HIST = SUBLANES


def _ssd_kernel(xbc_ref, z_ref, dt_ref, dtt_ref, past_ref, h0_ref, cw_ref, cb_ref, dtb_ref, dtbt_ref,
                alog_ref, alogt_ref, dsk_ref, nw_ref, o_ref, hout_ref, xbuf, ht_scr, *, lc):
    c = pl.program_id(1)
    lb = xbc_ref.shape[1]
    hist = CONV_HIST

    @pl.when(c == 0)
    def _():
        xbuf[0:hist, :] = past_ref[0]
        ht_scr[...] = h0_ref[0]

    xbuf[hist:hist + lb, :] = xbc_ref[0]
    u_all = cb_ref[...]
    for w in range(CONV_W):
        off = hist - (CONV_W - 1) + w
        u_all = u_all + xbuf[off:off + lb, :] * cw_ref[w:w + 1, :]
    xbuf[0:hist, :] = xbuf[lb:lb + hist, :]
    u_all = _silu(u_all)

    gn = N_GROUPS_S * D_STATE
    half = D_INNER // N_GROUPS_S
    heads_per_group = N_HEADS_S // N_GROUPS_S
    row = lax.broadcasted_iota(jnp.int32, (lc, lc), 0)
    col = lax.broadcasted_iota(jnp.int32, (lc, lc), 1)
    causal = col <= row
    tri_lower = causal.astype(BF16)
    tri_upper = (row <= col).astype(BF16)
    low = lax.broadcasted_iota(jnp.int32, (lc, LANES), 1) < HEAD_DIM_S
    a_head = -jnp.exp(alog_ref[...])
    a_head_t = -jnp.exp(alogt_ref[...])
    ht = ht_scr[...]

    for cc in range(lb // lc):
        rows = slice(cc * lc, (cc + 1) * lc)
        u = u_all[rows]
        xs = u[:, :D_INNER]
        bm = u[:, D_INNER:D_INNER + gn].astype(BF16)
        cm = u[:, D_INNER + gn:].astype(BF16)

        dt = _softplus(dt_ref[0, rows, :] + dtb_ref[...])
        dtt = _softplus(dtt_ref[0, :, rows] + dtbt_ref[...])
        a_cs = _tri_dot(tri_lower, dt * a_head)
        a_cst = _dot_tri(dtt * a_head_t, tri_upper)
        total = a_cs[lc - 1:lc, :]
        dt_e = _expand_heads(dt)
        w_e = _expand_heads(dt * jnp.exp(total - a_cs))
        ea_e = _expand_heads(jnp.exp(a_cs))
        cd_e = _expand_heads(jnp.exp(total))
        xdt = xs * dt_e
        xdw = (xs * w_e).astype(BF16)

        cbs = [lax.dot_general(cm[:, g * D_STATE:(g + 1) * D_STATE], bm[:, g * D_STATE:(g + 1) * D_STATE], _NT,
                               preferred_element_type=F32) for g in range(N_GROUPS_S)]
        yd = []
        for j in range(N_HEADS_S // 2):
            ms = []
            for hh in (2 * j, 2 * j + 1):
                seg = a_cs[:, hh:hh + 1] - a_cst[hh:hh + 1, :]
                dec = jnp.where(causal, jnp.exp(jnp.where(causal, seg, 0.0)), 0.0)
                ms.append((cbs[hh // heads_per_group] * dec).astype(BF16))
            xb = xdt[:, j * LANES:(j + 1) * LANES]
            rhs = jnp.concatenate([jnp.where(low, xb, 0.0), jnp.where(low, 0.0, xb)], axis=0).astype(BF16)
            yd.append(jnp.dot(jnp.concatenate(ms, axis=1), rhs, preferred_element_type=F32))
        y = jnp.concatenate(yd, axis=1)

        htb = ht.astype(BF16)
        y_off = jnp.concatenate(
            [jnp.dot(cm[:, g * D_STATE:(g + 1) * D_STATE], htb[:, g * half:(g + 1) * half],
                     preferred_element_type=F32) for g in range(N_GROUPS_S)], axis=1)
        st = jnp.concatenate(
            [lax.dot_general(bm[:, g * D_STATE:(g + 1) * D_STATE], xdw[:, g * half:(g + 1) * half],
                             (((0,), (0,)), ((), ())), preferred_element_type=F32) for g in range(N_GROUPS_S)], axis=1)
        ht = cd_e * ht + st

        y = y + y_off * ea_e + dsk_ref[...] * xs
        gz = y * _silu(z_ref[0, rows, :])
        outs = []
        for g in range(N_GROUPS_S):
            gg = gz[:, g * half:(g + 1) * half]
            outs.append(gg * lax.rsqrt(jnp.mean(gg * gg, axis=-1, keepdims=True) + EPS))
        o_ref[0, rows, :] = (jnp.concatenate(outs, axis=1) * nw_ref[...]).astype(o_ref.dtype)

    ht_scr[...] = ht
    hout_ref[0] = ht


def _ssd(xbc, z, dt_raw, conv_past, h0t, conv_w, conv_b, dt_bias, a_log, d_skip, norm_w):
    b, l, _ = xbc.shape
    lc = min(SSD_CHUNK, l)
    lb = lc * SSD_STEP_CHUNKS if l % (lc * SSD_STEP_CHUNKS) == 0 else lc
    assert l % lb == 0 and lc % CONV_HIST == 0
    nh = N_HEADS_S
    dtt = jnp.swapaxes(dt_raw, 1, 2)
    past = jnp.pad(conv_past, ((0, 0), (CONV_HIST - (CONV_W - 1), 0), (0, 0)))
    const = lambda shape: pl.BlockSpec(shape, lambda i, j: tuple(0 for _ in shape))
    o, hout = pl.pallas_call(
        functools.partial(_ssd_kernel, lc=lc),
        grid=(b, l // lb),
        in_specs=[pl.BlockSpec((1, lb, CONV_DIM), lambda i, j: (i, j, 0)),
                  pl.BlockSpec((1, lb, D_INNER), lambda i, j: (i, j, 0)),
                  pl.BlockSpec((1, lb, nh), lambda i, j: (i, j, 0)),
                  pl.BlockSpec((1, nh, lb), lambda i, j: (i, 0, j)),
                  pl.BlockSpec((1, CONV_HIST, CONV_DIM), lambda i, j: (i, 0, 0)),
                  pl.BlockSpec((1, D_STATE, D_INNER), lambda i, j: (i, 0, 0)),
                  const((CONV_W, CONV_DIM)), const((1, CONV_DIM)),
                  const((1, nh)), const((nh, 1)), const((1, nh)), const((nh, 1)),
                  const((1, D_INNER)), const((1, D_INNER))],
        out_specs=[pl.BlockSpec((1, lb, D_INNER), lambda i, j: (i, j, 0)),
                   pl.BlockSpec((1, D_STATE, D_INNER), lambda i, j: (i, 0, 0))],
        out_shape=[jax.ShapeDtypeStruct((b, l, D_INNER), BF16),
                   jax.ShapeDtypeStruct((b, D_STATE, D_INNER), F32)],
        scratch_shapes=[pltpu.VMEM((lb + CONV_HIST, CONV_DIM), F32), pltpu.VMEM((D_STATE, D_INNER), F32)],
        compiler_params=_params(("parallel", "arbitrary")),
    )(xbc, z, dt_raw, dtt, past, h0t, conv_w, conv_b.reshape(1, -1),
      dt_bias.reshape(1, nh), dt_bias.reshape(nh, 1), a_log.reshape(1, nh), a_log.reshape(nh, 1),
      jnp.repeat(d_skip, HEAD_DIM_S).reshape(1, -1), norm_w.reshape(1, -1))
    return o, hout


def _outproj_kernel(x_ref, oa_ref, os_ref, gt_ref, sc_ref, sh_ref, g_ref, wa_ref, ws_ref, rw_ref, rb_ref,
                    x1_ref, h2_ref, lg_ref):
    nb, tl, d = x_ref.shape
    oa = oa_ref[...].reshape(nb * tl, -1)
    os_ = os_ref[...].reshape(nb * tl, -1)
    mix = jnp.dot(oa, wa_ref[...], preferred_element_type=F32) + jnp.dot(os_, ws_ref[...], preferred_element_type=F32)
    x1 = x_ref[...] + gt_ref[...] * mix.reshape(nb, tl, d)
    x1_ref[...] = x1
    ms = jnp.mean(x1 * x1, axis=-1, keepdims=True)
    h2 = x1 * lax.rsqrt(ms + EPS) * g_ref[...]
    h2 = h2 * (1.0 + sc_ref[...]) + sh_ref[...]
    h2 = h2.reshape(nb * tl, d)
    _store_row_tiles(h2_ref, h2)
    h_hi, h_lo = _split2(h2)
    w_hi, w_lo = rw_ref[0], rw_ref[1]
    lg = (jnp.dot(h_hi, w_hi, preferred_element_type=F32) + jnp.dot(h_lo, w_hi, preferred_element_type=F32)
          + jnp.dot(h_hi, w_lo, preferred_element_type=F32)) + rb_ref[...]
    lg_ref[...] = lg.reshape(nb, tl, -1)


def _outproj(x, oa, os_, gt, sc, sh, g, wa, ws, rw, rb):
    b, l, d = x.shape
    nb, tl = _row_blocking(b, l, ROW_TILE)
    row = lambda n: pl.BlockSpec((nb, tl, n), lambda i, j: (i, j, 0))
    mod = pl.BlockSpec((nb, 1, d), lambda i, j: (i, 0, 0))
    const = lambda a: pl.BlockSpec(a.shape, lambda i, j: tuple(0 for _ in a.shape))
    g3 = g.reshape(1, 1, d)
    return pl.pallas_call(
        _outproj_kernel,
        grid=(b // nb, l // tl),
        in_specs=[row(d), row(D_ATTN), row(D_INNER), mod, mod, mod, const(g3), const(wa), const(ws),
                  const(rw), const(rb)],
        out_specs=[row(d), pl.BlockSpec((nb * tl * ROW_TILE_SUBLANES, LANES), lambda i, j: (i * (l // tl) + j, 0)),
                   row(LANES)],
        out_shape=[jax.ShapeDtypeStruct((b, l, d), F32), jax.ShapeDtypeStruct((b * l * ROW_TILE_SUBLANES, LANES), F32),
                   jax.ShapeDtypeStruct((b, l, LANES), F32)],
        compiler_params=_params(("parallel", "parallel")),
    )(x, oa, os_, gt.reshape(b, 1, d), sc.reshape(b, 1, d), sh.reshape(b, 1, d), g3, wa, ws, rw, rb)


def _route_kernel(lg_ref, idx_ref, gate_ref, cnt_ref, carry_scr):
    @pl.when(pl.program_id(0) == 0)
    def _():
        carry_scr[...] = jnp.zeros(carry_scr.shape, F32)

    v = lg_ref[...]
    tr = v.shape[0]
    lane = lax.broadcasted_iota(jnp.int32, (tr, LANES), 1)
    lane_f = lane.astype(F32)
    tops, idxs = [], []
    onehot = jnp.zeros((tr, LANES), F32)
    for _ in range(TOP_K):
        m = jnp.max(v, axis=-1, keepdims=True)
        idx = jnp.min(jnp.where(v == m, lane_f, float(LANES)), axis=-1, keepdims=True)
        hit = lane_f == idx
        v = jnp.where(hit, -jnp.inf, v)
        onehot = onehot + hit.astype(F32)
        tops.append(m)
        idxs.append(idx)
    es = [jnp.exp(t - tops[0]) for t in tops]
    denom = es[0] + es[1] + es[2] + es[3]

    row = lax.broadcasted_iota(jnp.int32, (tr, tr), 0)
    col = lax.broadcasted_iota(jnp.int32, (tr, tr), 1)
    before = jnp.dot((col < row).astype(BF16), onehot.astype(BF16), preferred_element_type=F32) + carry_scr[...]

    idx_out = jnp.zeros((tr, LANES), F32)
    gate_out = jnp.zeros((tr, LANES), F32)
    for k in range(TOP_K):
        rank = jnp.sum(jnp.where(lane_f == idxs[k], before, 0.0), axis=-1, keepdims=True)
        idx_out = jnp.where(lane == k, idxs[k], idx_out)
        idx_out = jnp.where(lane == TOP_K + k, rank, idx_out)
        gate_out = jnp.where(lane == k, es[k] / denom, gate_out)
    idx_ref[...] = idx_out.astype(jnp.int32)
    gate_ref[...] = gate_out
    carry_scr[...] = carry_scr[...] + jnp.sum(onehot, axis=0, keepdims=True)
    cnt_ref[...] = carry_scr[...]


def _route(logits):
    t, n = logits.shape
    tr = min(ROUTE_TILE, t)
    assert t % tr == 0
    return pl.pallas_call(
        _route_kernel,
        grid=(t // tr,),
        in_specs=[pl.BlockSpec((tr, n), lambda i: (i, 0))],
        out_specs=[pl.BlockSpec((tr, n), lambda i: (i, 0)), pl.BlockSpec((tr, n), lambda i: (i, 0)),
                   pl.BlockSpec((1, n), lambda i: (0, 0))],
        out_shape=[jax.ShapeDtypeStruct((t, n), jnp.int32), jax.ShapeDtypeStruct((t, n), F32),
                   jax.ShapeDtypeStruct((1, n), F32)],
        scratch_shapes=[pltpu.VMEM((1, n), F32)],
        compiler_params=_params(("arbitrary",)),
    )(logits)


def _row_tile(ref, r):
    start = r * ROW_TILE_SUBLANES
    if not isinstance(r, int):
        start = pl.multiple_of(start, ROW_TILE_SUBLANES)
    return ref.at[pl.ds(start, ROW_TILE_SUBLANES), :]


def _store_row_tiles(ref, x, tile0=0):
    rows = x.shape[0]
    for s in range(x.shape[1] // LANES):
        ref[pl.ds(tile0 + s, rows, stride=ROW_TILE_SUBLANES), :] = x[:, s * LANES:(s + 1) * LANES]


def _load_row_tiles(ref, row0, rows):
    return jnp.concatenate(
        [ref[pl.ds(row0 * ROW_TILE_SUBLANES + s, rows, stride=ROW_TILE_SUBLANES), :] for s in range(ROW_TILE_SUBLANES)],
        axis=1)


def _gather_rows(idx_ref, src_hbm, dst, sem, n_rows):
    group = SUBLANES
    assert n_rows % group == 0

    def body(g, carry):
        for u in range(group):
            r = g * group + u
            pltpu.make_async_copy(_row_tile(src_hbm, idx_ref[0, 0, r]), _row_tile(dst, r), sem).start(priority=u % 2)
        return carry
    lax.fori_loop(0, n_rows // group, body, 0)


def _wait_rows(src_hbm, dst, sem, n_rows):
    pltpu.make_async_copy(src_hbm.at[pl.ds(0, n_rows * ROW_TILE_SUBLANES), :], dst, sem).wait()


EXPERT_COL_CHUNK = 256


GATHER_BUFS = 3

def _dispatch_kernel(ends_ref, pad_ref, dest_ref, h_hbm, xs_hbm, zbuf, hbuf, zsem, fsem, sem, *, n_slots):
    i = pl.program_id(0)
    n = pl.num_programs(0)
    m = zbuf.shape[0] // ROW_TILE_SUBLANES
    rows = dest_ref.shape[2]
    td = rows // TOP_K
    slot = lax.rem(i, GATHER_BUFS)

    def fetch(step, s):
        start = pl.multiple_of(step * (td * ROW_TILE_SUBLANES), td * ROW_TILE_SUBLANES)
        return pltpu.make_async_copy(h_hbm.at[pl.ds(start, td * ROW_TILE_SUBLANES), :], hbuf.at[s], fsem.at[s])

    def wait_rows_of(s):
        for _ in range(TOP_K):
            pltpu.make_async_copy(hbuf.at[s], xs_hbm.at[pl.ds(0, td * ROW_TILE_SUBLANES), :], sem.at[s]).wait()

    def zero_block(first_slot):
        start = pl.multiple_of(first_slot * ROW_TILE_SUBLANES, ROW_TILE_SUBLANES)
        return pltpu.make_async_copy(zbuf, xs_hbm.at[pl.ds(start, zbuf.shape[0]), :], zsem)

    @pl.when(i == 0)
    def _():
        zbuf[...] = jnp.zeros(zbuf.shape, zbuf.dtype)
        jobs = [(pad_ref[e] > 0, ends_ref[e] - m) for e in range(N_EXPERTS)]
        jobs += [(ends_ref[N_EXPERTS - 1] + j * m < n_slots, ends_ref[N_EXPERTS - 1] + j * m)
                 for j in range(N_EXPERTS + 1)]
        for cond, first in jobs:
            @pl.when(cond)
            def _():
                zero_block(first).start()
        for cond, first in jobs:
            @pl.when(cond)
            def _():
                zero_block(first).wait()
        fetch(0, 0).start()

    @pl.when(i >= 2)
    def _():
        wait_rows_of(lax.rem(i + 1, GATHER_BUFS))

    @pl.when(i + 1 < n)
    def _():
        fetch(i + 1, lax.rem(i + 1, GATHER_BUFS)).start()

    fetch(i, slot).wait()
    for s in range(GATHER_BUFS):
        @pl.when(slot == s)
        def _():
            for j in range(rows):
                pltpu.make_async_copy(_row_tile(hbuf.at[s], j // TOP_K), _row_tile(xs_hbm, dest_ref[0, 0, j]),
                                      sem.at[s]).start(priority=j % 2)

    @pl.when(i == n - 1)
    def _():
        @pl.when(i >= 1)
        def _():
            wait_rows_of(lax.rem(i + 2, GATHER_BUFS))
        wait_rows_of(slot)


def _dispatch(dest, ends_p, padded, h2, m, n_slots):
    t = dest.shape[0]
    td = min(DISPATCH_TILE, t)
    assert t % td == 0
    n_steps = t // td
    grid_spec = pltpu.PrefetchScalarGridSpec(
        num_scalar_prefetch=2,
        grid=(n_steps,),
        in_specs=[pl.BlockSpec((1, 1, TOP_K * td), lambda i, e, p: (i, 0, 0), memory_space=pltpu.SMEM),
                  pl.BlockSpec(memory_space=pl.ANY)],
        out_specs=pl.BlockSpec(memory_space=pl.ANY),
        scratch_shapes=[pltpu.VMEM((m * ROW_TILE_SUBLANES, LANES), F32),
                        pltpu.VMEM((GATHER_BUFS, td * ROW_TILE_SUBLANES, LANES), F32),
                        pltpu.SemaphoreType.DMA(()), pltpu.SemaphoreType.DMA((GATHER_BUFS,)),
                        pltpu.SemaphoreType.DMA((GATHER_BUFS,))],
    )
    return pl.pallas_call(
        functools.partial(_dispatch_kernel, n_slots=n_slots),
        grid_spec=grid_spec,
        out_shape=jax.ShapeDtypeStruct((n_slots * ROW_TILE_SUBLANES, LANES), F32),
        compiler_params=_params(("arbitrary",)),
    )(ends_p, padded, dest.reshape(n_steps, 1, TOP_K * td), h2)


def _experts_kernel(be_ref, used_ref, x_ref, wgu_ref, bgu_ref, wdn_ref, bdn_ref, o_ref):
    m = x_ref.shape[0] // ROW_TILE_SUBLANES
    cw = EXPERT_COL_CHUNK
    tiles_per_piece = cw // LANES
    n_ff, n_out = D_FF // cw, ROW_TILE_SUBLANES // tiles_per_piece
    in_use = pl.program_id(0) < used_ref[0]

    @pl.when(in_use)
    def _():
        x = _load_row_tiles(x_ref, 0, m).astype(BF16)
        acts = []
        for c in range(n_ff):
            g = (jnp.dot(x, wgu_ref[0, :, c * cw:(c + 1) * cw], preferred_element_type=F32)
                 + bgu_ref[0, :, c * cw:(c + 1) * cw])
            u = (jnp.dot(x, wgu_ref[0, :, D_FF + c * cw:D_FF + (c + 1) * cw], preferred_element_type=F32)
                 + bgu_ref[0, :, D_FF + c * cw:D_FF + (c + 1) * cw])
            gate = jnp.minimum(g, SWIGLU_LIMIT)
            up = jnp.clip(u, -SWIGLU_LIMIT, SWIGLU_LIMIT)
            acts.append(((up + 1.0) * gate * jax.nn.sigmoid(SWIGLU_ALPHA * gate)).astype(BF16))
        act = jnp.concatenate(acts, axis=1)
        for c in range(n_out):
            y = (jnp.dot(act, wdn_ref[0, :, c * cw:(c + 1) * cw], preferred_element_type=F32)
                 + bdn_ref[0, :, c * cw:(c + 1) * cw])
            _store_row_tiles(o_ref, y, tile0=c * tiles_per_piece)

    @pl.when(jnp.logical_not(in_use))
    def _():
        o_ref[...] = jnp.zeros(o_ref.shape, o_ref.dtype)


def _experts(block_e, n_used, xs, wgu, bgu, wdn, bdn, m):
    n_blocks = block_e.shape[0]
    d = D_MODEL
    mt = m * ROW_TILE_SUBLANES
    grid_spec = pltpu.PrefetchScalarGridSpec(
        num_scalar_prefetch=2,
        grid=(n_blocks,),
        in_specs=[pl.BlockSpec((mt, LANES), lambda i, be, nu: (jnp.minimum(i, nu[0] - 1), 0)),
                  pl.BlockSpec((1, d, 2 * D_FF), lambda i, be, nu: (be[i], 0, 0)),
                  pl.BlockSpec((1, 1, 2 * D_FF), lambda i, be, nu: (be[i], 0, 0)),
                  pl.BlockSpec((1, D_FF, d), lambda i, be, nu: (be[i], 0, 0)),
                  pl.BlockSpec((1, 1, d), lambda i, be, nu: (be[i], 0, 0))],
        out_specs=pl.BlockSpec((mt, LANES), lambda i, be, nu: (i, 0)),
    )
    return pl.pallas_call(
        _experts_kernel,
        grid_spec=grid_spec,
        out_shape=jax.ShapeDtypeStruct((n_blocks * mt, LANES), F32),
        compiler_params=_params(("arbitrary",)),
    )(block_e, n_used, xs, wgu, bgu.reshape(N_EXPERTS, 1, -1), wdn, bdn.reshape(N_EXPERTS, 1, -1))


def _combine_kernel(idx0_ref, idx1_ref, idxn_ref, y_hbm, x1_ref, gate_ref, gt_ref, g_ref, o_ref, ybuf, sem):
    i = pl.program_id(0)
    n = pl.num_programs(0)
    nb, tl, d = x1_ref.shape
    tc = nb * tl
    slot = lax.rem(i, GATHER_BUFS)
    nxt = lax.rem(i + 2, GATHER_BUFS)

    @pl.when(i == 0)
    def _():
        _gather_rows(idx0_ref, y_hbm, ybuf.at[0], sem.at[0], TOP_K * tc)
        _gather_rows(idx1_ref, y_hbm, ybuf.at[1], sem.at[1], TOP_K * tc)

    _wait_rows(y_hbm, ybuf.at[slot], sem.at[slot], TOP_K * tc)

    def prefetch(k):
        for r in range(k * tc, (k + 1) * tc):
            pltpu.make_async_copy(_row_tile(y_hbm, idxn_ref[0, 0, r]), _row_tile(ybuf.at[nxt], r),
                                  sem.at[nxt]).start(priority=r % 2)

    gates = gate_ref[...]
    y = gates[:, 0:1] * _load_row_tiles(ybuf.at[slot], 0, tc)
    prefetch(0)
    for k in range(1, TOP_K):
        y = y + gates[:, k:k + 1] * _load_row_tiles(ybuf.at[slot], k * tc, tc)
        prefetch(k)
    x2 = x1_ref[...] + gt_ref[...] * y.reshape(nb, tl, d)
    ms = jnp.mean(x2 * x2, axis=-1, keepdims=True)
    o_ref[...] = x2 * lax.rsqrt(ms + EPS) * g_ref[...]

    @pl.when(i == n - 1)
    def _():
        for ahead in (1, 2):
            s = lax.rem(i + ahead, GATHER_BUFS)
            _wait_rows(y_hbm, ybuf.at[s], sem.at[s], TOP_K * tc)


def _combine(dest, y_slots, x1, gates, gt, g_final):
    b, l, d = x1.shape
    nb, tl = _row_blocking(b, l, COMBINE_TILE)
    tc = nb * tl
    t = b * l
    n_steps = t // tc
    idx = jnp.swapaxes(dest.reshape(n_steps, tc, TOP_K), 1, 2).reshape(n_steps, 1, TOP_K * tc)
    steps_per_batch_row = l // tl
    tok = lambda i: (i // steps_per_batch_row, i % steps_per_batch_row, 0)
    smem = lambda f: pl.BlockSpec((1, 1, TOP_K * tc), f, memory_space=pltpu.SMEM)
    return pl.pallas_call(
        _combine_kernel,
        grid=(n_steps,),
        in_specs=[smem(lambda i: (0, 0, 0)),
                  smem(lambda i: (min(1, n_steps - 1), 0, 0)),
                  smem(lambda i: (jnp.minimum(i + 2, n_steps - 1), 0, 0)),
                  pl.BlockSpec(memory_space=pl.ANY),
                  pl.BlockSpec((nb, tl, d), tok),
                  pl.BlockSpec((tc, LANES), lambda i: (i, 0)),
                  pl.BlockSpec((nb, 1, d), lambda i: (i // steps_per_batch_row, 0, 0)),
                  pl.BlockSpec((1, 1, d), lambda i: (0, 0, 0))],
        out_specs=pl.BlockSpec((nb, tl, d), tok),
        out_shape=jax.ShapeDtypeStruct((b, l, d), F32),
        scratch_shapes=[pltpu.VMEM((GATHER_BUFS, TOP_K * tc * ROW_TILE_SUBLANES, LANES), F32),
                        pltpu.SemaphoreType.DMA((GATHER_BUFS,))],
        compiler_params=_params(("arbitrary",)),
    )(idx, idx, idx, y_slots, x1, gates, gt.reshape(b, 1, d), g_final.reshape(1, 1, d))


def _moe(h2, logits, x1, gt_m, g_final, wts):
    b, l, d = x1.shape
    t = b * l
    m = MOE_ROWS if t * TOP_K >= 4 * N_EXPERTS * MOE_ROWS else MOE_ROWS_SMALL
    idx, gates, counts = _route(logits.reshape(t, LANES))
    top_e = idx[:, :TOP_K]
    rank = idx[:, TOP_K:2 * TOP_K]
    counts = counts[0, :N_EXPERTS].astype(jnp.int32)
    padded = (counts + m - 1) // m * m
    ends_p = jnp.cumsum(padded)
    starts_p = ends_p - padded
    experts = jnp.arange(N_EXPERTS, dtype=jnp.int32)
    start_of = jnp.sum(jnp.where(top_e[..., None] == experts, starts_p, 0), axis=-1)
    dest = start_of + rank
    n_blocks = (t * TOP_K + N_EXPERTS * (m - 1) + m - 1) // m
    block_start = jnp.arange(n_blocks, dtype=jnp.int32)[:, None] * m
    block_e = jnp.minimum(jnp.sum((ends_p[None, :] <= block_start).astype(jnp.int32), axis=1), N_EXPERTS - 1)
    xs = _dispatch(dest, ends_p, padded, h2, m, n_blocks * m)
    n_used = (ends_p[N_EXPERTS - 1:] // m).astype(jnp.int32)
    y_slots = _experts(block_e, n_used, xs, *wts, m)
    return _combine(dest, y_slots, x1, gates, gt_m, g_final)


def _layer(x, mod, k_past, v_past, logf_past, conv_past, ssm_past, p, g_final):
    b, l, d = x.shape
    sh_a, sc_a, gt_a, sh_m, sc_m, gt_m = jnp.split(mod, 6, axis=-1)
    q, kf, vf, kb, vb, z, xbc, sm = _inproj(x, p['g_mix'], sc_a, sh_a, *p['w_in'])
    dt_raw = sm[:, :, N_HEADS_A:N_HEADS_A + N_HEADS_S]

    if k_past is None:
        logf, qx, kx = _cumsum_bias(sm, p['b_f'])
        o_a = _attention(q, qx, kb, kx, vb)
        conv_past = jnp.zeros((b, CONV_W - 1, CONV_DIM), F32)
        h0t = jnp.zeros((b, D_STATE, D_INNER), F32)
    else:
        lp = k_past.shape[1]
        past_t = jnp.swapaxes(logf_past, 1, 2).reshape(b * N_HEADS_A, lp)
        cumt_p = _cumsum_lanes(past_t).reshape(b, N_HEADS_A, lp)
        carry = jnp.pad(cumt_p[:, :, lp - 1], ((0, 0), (0, LANES - N_HEADS_A))).reshape(b, 1, LANES)
        lpad = -(-l // LANES) * LANES
        sm_pad = jnp.pad(sm, ((0, 0), (0, lpad - l), (0, 0)))
        logf, cum_n, cumt_n = _cumsum(sm_pad, p['b_f'], carry)
        logf = logf[:, :l]
        o_a = _attention_cached(q, kb, vb, k_past, v_past, cum_n, cumt_p, cumt_n)
        h0t = jnp.swapaxes(ssm_past.reshape(b, D_INNER, D_STATE), 1, 2)
    o_s, hout = _ssd(xbc, z, dt_raw, conv_past, h0t, p['conv_w'], p['conv_b'], p['dt_bias'], p['a_log'],
                     p['d_skip'], p['ssd_norm_w'])
    x1, h2, logits = _outproj(x, o_a, o_s, gt_a, sc_m, sh_m, p['g_ffn'], p['w_out_a'], p['w_out_s'],
                              p['router_w'], p['router_b'])
    y = _moe(h2, logits, x1, gt_m, g_final, p['experts'])

    assert l >= CONV_W - 1
    conv_new = xbc[:, l - (CONV_W - 1):]
    ssm_new = jnp.swapaxes(hout, 1, 2).reshape(b, N_HEADS_S, HEAD_DIM_S, D_STATE)
    return (y, kf.reshape(b, l, N_HEADS_A, HEAD_DIM_A), vf.reshape(b, l, N_HEADS_A, HEAD_DIM_A),
            logf[:, :, :N_HEADS_A], conv_new, ssm_new)


def kernel(x_prompt, x_sample, cache_k, cache_v, cache_logf, state_conv, state_ssm, c_prompt, c_sample, w_ada, b_ada, g_mix, w_in, b_f, conv_w, conv_b, dt_bias, a_log, d_skip, ssd_norm_w, w_out, g_ffn, router_w, router_b, w_gate_up, b_gate_up, w_down, b_down, g_final):
    assert w_ada.shape[0] == 1, "single-layer operation"
    bp = x_prompt.shape[0]
    w = w_in[0]
    q_end, k_end, v_end = D_ATTN, 2 * D_ATTN, 3 * D_ATTN
    f_end = v_end + N_HEADS_A
    z_end = f_end + D_INNER
    xbc_end = z_end + CONV_DIM
    w_small = jnp.concatenate(
        [w[:, v_end:f_end], w[:, xbc_end:], jnp.zeros((D_MODEL, LANES - N_HEADS_A - N_HEADS_S), F32)], axis=1)
    cast = lambda a: a.astype(BF16)
    p = {
        'g_mix': g_mix[0],
        'w_in': (cast(w[:, :q_end]), cast(w[:, q_end:k_end]), cast(w[:, k_end:v_end]), cast(w[:, f_end:z_end]),
                 cast(w[:, z_end:xbc_end]), cast(w_small)),
        'b_f': jnp.pad(b_f[0], (0, LANES - N_HEADS_A)).reshape(1, LANES),
        'conv_w': conv_w[0], 'conv_b': conv_b[0], 'dt_bias': dt_bias[0], 'a_log': a_log[0],
        'd_skip': d_skip[0], 'ssd_norm_w': ssd_norm_w[0],
        'w_out_a': cast(w_out[0, :D_ATTN]), 'w_out_s': cast(w_out[0, D_ATTN:]),
        'g_ffn': g_ffn[0],
        'router_w': jnp.stack(_split2(jnp.pad(router_w[0], ((0, 0), (0, LANES - N_EXPERTS))))),
        'router_b': jnp.pad(router_b[0], (0, LANES - N_EXPERTS), constant_values=-jnp.inf).reshape(1, LANES),
        'experts': (cast(w_gate_up[0]), b_gate_up[0], cast(w_down[0]), b_down[0]),
    }
    mod = _adaln(jnp.concatenate([c_prompt, c_sample], axis=0), w_ada[0], b_ada[0])
    outs_p = _layer(x_prompt, mod[:bp], None, None, None, None, None, p, g_final)
    outs_s = _layer(x_sample, mod[bp:], cache_k[0], cache_v[0], cache_logf[0], state_conv[0], state_ssm[0], p, g_final)
    stack = lambda a: a[None]
    return (outs_p[0], outs_s[0]) + tuple(stack(a) for a in outs_p[1:]) + tuple(stack(a) for a in outs_s[1:])
```

```python
import functools
import math

import jax
import jax.numpy as jnp
from jax import lax
from jax.experimental import pallas as pl
from jax.experimental.pallas import tpu as pltpu

F32 = jnp.float32
BF16 = jnp.bfloat16
HIGHEST = lax.Precision.HIGHEST

D_MODEL = 1024
N_HEADS_A = 8
HEAD_DIM_A = 128
D_ATTN = N_HEADS_A * HEAD_DIM_A
D_INNER = 1024
HEAD_DIM_S = 64
N_HEADS_S = D_INNER // HEAD_DIM_S
N_GROUPS_S = 2
D_STATE = 128
CONV_W = 4
CONV_DIM = D_INNER + 2 * N_GROUPS_S * D_STATE
N_EXPERTS = 32
TOP_K = 4
D_FF = 1024
SWIGLU_LIMIT = 7.0
SWIGLU_ALPHA = 1.702
EPS = 1e-5
LANES = 128
SUBLANES = 8
ROW_TILE_SUBLANES = D_MODEL // LANES
assert ROW_TILE_SUBLANES == SUBLANES
VMEM_LIMIT = 56 * 1024 * 1024
LOG2E = 1.4426950408889634
ATT_HEADS_PER_STEP = 4

ADALN_COL_TILE = 512
ROW_TILE = 512
ATT_TILE = 512
DEC_KV_TILE = 2048
CUM_TILE = 512
SSD_CHUNK = 128
SSD_STEP_CHUNKS = 2
ROUTE_TILE = 1024
MOE_ROWS = 512
MOE_ROWS_SMALL = 128
COMBINE_TILE = 256
DISPATCH_TILE = 256

def _params(sem):
    return pltpu.CompilerParams(dimension_semantics=sem, vmem_limit_bytes=VMEM_LIMIT)


def _row_blocking(b, l, tile):
    if l >= tile:
        assert l % tile == 0
        return 1, tile
    nb = min(b, max(1, tile // l))
    while b % nb:
        nb -= 1
    return nb, l


def _split3(x):
    hi = x.astype(BF16)
    r1 = x - hi.astype(F32)
    mid = r1.astype(BF16)
    lo = (r1 - mid.astype(F32)).astype(BF16)
    return hi, mid, lo


def _split2(x):
    hi = x.astype(BF16)
    return hi, (x - hi.astype(F32)).astype(BF16)


def _tri_dot(tri, x):
    hi, mid, lo = _split3(x)
    d = lambda p: jnp.dot(tri, p, preferred_element_type=F32)
    return d(hi) + d(mid) + d(lo)


def _dot_tri(x, tri):
    hi, mid, lo = _split3(x)
    d = lambda p: jnp.dot(p, tri, preferred_element_type=F32)
    return d(hi) + d(mid) + d(lo)


def _silu(x):
    return x * jax.nn.sigmoid(x)


def _softplus(x):
    return jnp.maximum(x, 0.0) + jnp.log1p(jnp.exp(-jnp.abs(x)))


def _log_sigmoid(x):
    return jnp.minimum(x, 0.0) - jnp.log1p(jnp.exp(-jnp.abs(x)))


def _adaln_kernel(c_ref, w_ref, b_ref, o_ref):
    s = _silu(c_ref[...])
    o_ref[...] = jnp.dot(s, w_ref[...], precision=HIGHEST, preferred_element_type=F32) + b_ref[...]


def _adaln(c, w, b):
    m, d = c.shape
    n = w.shape[1]
    tn = ADALN_COL_TILE
    assert n % tn == 0
    return pl.pallas_call(
        _adaln_kernel,
        grid=(n // tn,),
        in_specs=[pl.BlockSpec((m, d), lambda j: (0, 0)),
                  pl.BlockSpec((d, tn), lambda j: (0, j)),
                  pl.BlockSpec((1, tn), lambda j: (0, j))],
        out_specs=pl.BlockSpec((m, tn), lambda j: (0, j)),
        out_shape=jax.ShapeDtypeStruct((m, n), F32),
        compiler_params=_params(("parallel",)),
    )(c, w, b.reshape(1, n))


def _inproj_kernel(x_ref, g_ref, sc_ref, sh_ref, wq_ref, wk_ref, wv_ref, wz_ref, wx_ref, ws_ref,
                   q_ref, kf_ref, vf_ref, kb_ref, vb_ref, z_ref, xbc_ref, sm_ref):
    x = x_ref[...]
    nb, tl, d = x.shape
    ms = jnp.mean(x * x, axis=-1, keepdims=True)
    h = x * lax.rsqrt(ms + EPS) * g_ref[...]
    h = h * (1.0 + sc_ref[...]) + sh_ref[...]
    hb = h.reshape(nb * tl, d).astype(BF16)

    def mm(w_ref):
        return jnp.dot(hb, w_ref[...], preferred_element_type=F32)

    q = mm(wq_ref) * (LOG2E / math.sqrt(HEAD_DIM_A))
    q_ref[...] = q.astype(BF16).reshape(nb, tl, -1)
    k = mm(wk_ref)
    kf_ref[...] = k.reshape(nb, tl, -1)
    kb_ref[...] = k.astype(BF16).reshape(nb, tl, -1)
    v = mm(wv_ref)
    vf_ref[...] = v.reshape(nb, tl, -1)
    vb_ref[...] = v.astype(BF16).reshape(nb, tl, -1)
    z_ref[...] = mm(wz_ref).reshape(nb, tl, -1)
    xbc_ref[...] = mm(wx_ref).reshape(nb, tl, -1)
    sm_ref[...] = mm(ws_ref).reshape(nb, tl, -1)


def _inproj(x, g, sc, sh, wq, wk, wv, wz, wx, ws):
    b, l, d = x.shape
    nb, tl = _row_blocking(b, l, ROW_TILE)
    grid = (b // nb, l // tl)
    row = lambda n: pl.BlockSpec((nb, tl, n), lambda i, j: (i, j, 0))
    mod = pl.BlockSpec((nb, 1, d), lambda i, j: (i, 0, 0))
    wspec = lambda w: pl.BlockSpec(w.shape, lambda i, j: (0, 0), pipeline_mode=pl.Buffered(1))
    outs = [(D_ATTN, BF16), (D_ATTN, F32), (D_ATTN, F32), (D_ATTN, BF16), (D_ATTN, BF16),
            (D_INNER, F32), (CONV_DIM, F32), (LANES, F32)]
    return pl.pallas_call(
        _inproj_kernel,
        grid=grid,
        in_specs=[row(d), pl.BlockSpec((1, 1, d), lambda i, j: (0, 0, 0)), mod, mod,
                  wspec(wq), wspec(wk), wspec(wv), wspec(wz), wspec(wx), wspec(ws)],
        out_specs=[row(n) for n, _ in outs],
        out_shape=[jax.ShapeDtypeStruct((b, l, n), dt) for n, dt in outs],
        compiler_params=_params(("parallel", "parallel")),
    )(x, g.reshape(1, 1, d), sc.reshape(b, 1, d), sh.reshape(b, 1, d), wq, wk, wv, wz, wx, ws)


def _running_sum(v_ref, bias_ref, carry_ref, carry_scr):
    @pl.when(pl.program_id(1) == 0)
    def _():
        carry_scr[...] = carry_ref[0]

    v = v_ref[0]
    tl = v.shape[0]
    lf = _log_sigmoid(v + bias_ref[...])
    row = lax.broadcasted_iota(jnp.int32, (tl, tl), 0)
    col = lax.broadcasted_iota(jnp.int32, (tl, tl), 1)
    tri = (row >= col).astype(BF16)
    cs = _tri_dot(tri, lf) + carry_scr[...]
    carry_scr[...] = cs[tl - 1:tl, :]
    return lf, cs


def _cumsum_kernel(v_ref, bias_ref, carry_ref, logf_ref, cum_ref, cumt_ref, carry_scr):
    lf, cs = _running_sum(v_ref, bias_ref, carry_ref, carry_scr)
    logf_ref[0] = lf
    cum_ref[0] = cs
    cumt_ref[0] = cs.T[:N_HEADS_A, :]


def _cumsum(vals, bias, carry):
    b, l, n = vals.shape
    tl = min(CUM_TILE, l)
    assert l % tl == 0 and tl % LANES == 0
    return pl.pallas_call(
        _cumsum_kernel,
        grid=(b, l // tl),
        in_specs=[pl.BlockSpec((1, tl, n), lambda i, j: (i, j, 0)),
                  pl.BlockSpec((1, n), lambda i, j: (0, 0)),
                  pl.BlockSpec((1, 1, n), lambda i, j: (i, 0, 0))],
        out_specs=[pl.BlockSpec((1, tl, n), lambda i, j: (i, j, 0)),
                   pl.BlockSpec((1, tl, n), lambda i, j: (i, j, 0)),
                   pl.BlockSpec((1, N_HEADS_A, tl), lambda i, j: (i, 0, j))],
        out_shape=[jax.ShapeDtypeStruct((b, l, n), F32),
                   jax.ShapeDtypeStruct((b, l, n), F32),
                   jax.ShapeDtypeStruct((b, N_HEADS_A, l), F32)],
        scratch_shapes=[pltpu.VMEM((1, n), F32)],
        compiler_params=_params(("parallel", "arbitrary")),
    )(vals, bias, carry)


def _cumsum_lanes_kernel(x_ref, o_ref, carry_scr):
    @pl.when(pl.program_id(0) == 0)
    def _():
        carry_scr[...] = jnp.zeros(carry_scr.shape, F32)

    x = x_ref[...]
    tl = x.shape[1]
    row = lax.broadcasted_iota(jnp.int32, (tl, tl), 0)
    col = lax.broadcasted_iota(jnp.int32, (tl, tl), 1)
    cs = _dot_tri(x, (row <= col).astype(BF16)) + carry_scr[...]
    o_ref[...] = cs
    carry_scr[...] = cs[:, tl - 1:tl]


def _cumsum_lanes(x):
    rows, l = x.shape
    tl = min(CUM_TILE, l)
    assert l % tl == 0 and rows % SUBLANES == 0
    return pl.pallas_call(
        _cumsum_lanes_kernel,
        grid=(l // tl,),
        in_specs=[pl.BlockSpec((rows, tl), lambda j: (0, j))],
        out_specs=pl.BlockSpec((rows, tl), lambda j: (0, j)),
        out_shape=jax.ShapeDtypeStruct((rows, l), F32),
        scratch_shapes=[pltpu.VMEM((rows, 1), F32)],
        compiler_params=_params(("arbitrary",)),
    )(x)


N_BIAS_TERMS = 3


def _bias_placement():
    rows = jnp.arange(N_BIAS_TERMS * LANES)
    n, h = rows // LANES, rows % LANES
    col = jnp.arange(D_ATTN)[None, :]
    valid = (h < N_HEADS_A)[:, None]
    eq = jnp.where(valid & (col == (h * LANES + n)[:, None]), 1.0, 0.0)
    ek = jnp.where(valid & (col == (h * LANES + N_BIAS_TERMS + n)[:, None]), -1.0, 0.0)
    lane = jnp.arange(D_ATTN) % LANES
    ones_q = jnp.where((lane >= N_BIAS_TERMS) & (lane < 2 * N_BIAS_TERMS), 1.0, 0.0)
    ones_k = jnp.where(lane < N_BIAS_TERMS, 1.0, 0.0)
    return eq.astype(BF16), ek.astype(BF16), ones_q.reshape(1, -1).astype(F32), ones_k.reshape(1, -1).astype(F32)


def _cumsum_bias_kernel(v_ref, bias_ref, carry_ref, eq_ref, ek_ref, oq_ref, ok_ref, logf_ref, qx_ref, kx_ref, carry_scr):
    lf, cs = _running_sum(v_ref, bias_ref, carry_ref, carry_scr)
    logf_ref[0] = lf
    terms = jnp.concatenate(_split3(cs * LOG2E), axis=1)
    qx_ref[0] = (jnp.dot(terms, eq_ref[...], preferred_element_type=F32) + oq_ref[...]).astype(BF16)
    kx_ref[0] = (jnp.dot(terms, ek_ref[...], preferred_element_type=F32) + ok_ref[...]).astype(BF16)


def _cumsum_bias(vals, bias):
    b, l, n = vals.shape
    tl = min(CUM_TILE, l)
    assert l % tl == 0
    row = lambda w: pl.BlockSpec((1, tl, w), lambda i, j: (i, j, 0))
    const = lambda a: pl.BlockSpec(a.shape, lambda i, j: (0, 0))
    placement = _bias_placement()
    return pl.pallas_call(
        _cumsum_bias_kernel,
        grid=(b, l // tl),
        in_specs=[row(n), pl.BlockSpec((1, n), lambda i, j: (0, 0)), pl.BlockSpec((1, 1, n), lambda i, j: (i, 0, 0))]
        + [const(a) for a in placement],
        out_specs=[row(n), row(D_ATTN), row(D_ATTN)],
        out_shape=[jax.ShapeDtypeStruct((b, l, n), F32), jax.ShapeDtypeStruct((b, l, D_ATTN), BF16),
                   jax.ShapeDtypeStruct((b, l, D_ATTN), BF16)],
        scratch_shapes=[pltpu.VMEM((1, n), F32)],
        compiler_params=_params(("parallel", "arbitrary")),
    )(vals, bias, jnp.zeros((b, 1, n), F32), *placement)


_NT = (((1,), (1,)), ((), ()))


def _softmax_step(s, v, m_prev, l_prev, acc_prev):
    m_new = jnp.maximum(m_prev, jnp.max(s, axis=-1, keepdims=True))
    p = jnp.exp2(s - m_new)
    alpha = jnp.exp2(m_prev - m_new)
    l_new = alpha * l_prev + jnp.sum(p, axis=-1, keepdims=True)
    acc_new = alpha * acc_prev + jnp.dot(p.astype(BF16), v, preferred_element_type=F32)
    return m_new, l_new, acc_new


def _attn_kernel(q_ref, qx_ref, k_ref, kx_ref, v_ref, o_ref, m_scr, acc_scr, *, tile, heads):
    i = pl.program_id(2)
    dh = HEAD_DIM_A
    n_chunks = tile // LANES
    ones = jnp.ones((tile, dh), BF16)
    m_scr[...] = jnp.full(m_scr.shape, -jnp.inf, F32)
    acc_scr[...] = jnp.zeros(acc_scr.shape, F32)
    q2 = [jnp.concatenate([q_ref[0, :, hh * dh:(hh + 1) * dh], qx_ref[0, :, hh * dh:(hh + 1) * dh]], axis=1)
          for hh in range(heads)]

    def update(j, mask):
        start = pl.multiple_of(j * tile, tile)
        rows = pl.ds(start, tile)
        for hh in range(heads):
            sl = slice(hh * dh, (hh + 1) * dh)
            k2 = jnp.concatenate([k_ref[0, rows, sl], kx_ref[0, rows, sl]], axis=1)
            v2 = jnp.concatenate([v_ref[0, rows, sl], ones], axis=1)
            s = lax.dot_general(q2[hh], k2, _NT, preferred_element_type=F32)
            if mask is not None:
                s = jnp.where(mask, s, -jnp.inf)
            chunks = [s[:, c * LANES:(c + 1) * LANES] for c in range(n_chunks)]
            cmax = functools.reduce(jnp.maximum, chunks)
            m_prev = m_scr[hh]
            m_new = jnp.maximum(m_prev, jnp.max(cmax, axis=-1, keepdims=True))
            alpha = jnp.exp2(m_prev - m_new)
            p = jnp.concatenate([jnp.exp2(c - m_new) for c in chunks], axis=1).astype(BF16)
            pv = jnp.dot(p, v2, preferred_element_type=F32)
            m_scr[hh] = m_new
            acc_scr[hh, :, :dh] = alpha * acc_scr[hh, :, :dh] + pv[:, :dh]
            acc_scr[hh, :, dh:] = alpha * acc_scr[hh, :, dh:] + pv[:, dh:]

    def body(j, carry):
        update(j, None)
        return carry

    lax.fori_loop(0, i, body, 0)
    row = lax.broadcasted_iota(jnp.int32, (tile, tile), 0)
    col = lax.broadcasted_iota(jnp.int32, (tile, tile), 1)
    update(i, col <= row)
    for hh in range(heads):
        o_ref[0, :, hh * dh:(hh + 1) * dh] = (acc_scr[hh, :, :dh] / acc_scr[hh, :, dh:]).astype(o_ref.dtype)


def _attention(q, qx, k, kx, v):
    b, l, _ = q.shape
    tile = ATT_TILE if l >= 2 * ATT_TILE else LANES
    assert l % tile == 0
    heads = ATT_HEADS_PER_STEP
    w = heads * HEAD_DIM_A
    qspec = pl.BlockSpec((1, tile, w), lambda bi, h, i: (bi, i, h))
    kspec = pl.BlockSpec((1, l, w), lambda bi, h, i: (bi, 0, h))
    return pl.pallas_call(
        functools.partial(_attn_kernel, tile=tile, heads=heads),
        grid=(b, N_HEADS_A // heads, l // tile),
        in_specs=[qspec, qspec, kspec, kspec, kspec],
        out_specs=qspec,
        out_shape=jax.ShapeDtypeStruct((b, l, D_ATTN), BF16),
        scratch_shapes=[pltpu.VMEM((heads, tile, LANES), F32), pltpu.VMEM((heads, tile, 2 * HEAD_DIM_A), F32)],
        compiler_params=_params(("parallel", "parallel", "arbitrary")),
    )(q, qx, k, kx, v)


def _attn_cached_kernel(q_ref, kn_ref, vn_ref, kc_ref, vc_ref, cumn_ref, cumtp_ref, cumtn_ref, o_ref,
                        m_scr, l_scr, acc_scr):
    j = pl.program_id(1)
    ld = q_ref.shape[1]
    dh = HEAD_DIM_A

    @pl.when(j == 0)
    def _():
        m_scr[...] = jnp.full(m_scr.shape, -jnp.inf, F32)
        l_scr[...] = jnp.zeros(l_scr.shape, F32)
        acc_scr[...] = jnp.zeros(acc_scr.shape, F32)

    def head_step(h, kh, vh, ck, mask):
        sl = slice(h * dh, (h + 1) * dh)
        s = lax.dot_general(q_ref[0, :, sl], kh, _NT, preferred_element_type=F32)
        s = s + (cumn_ref[0, :ld, h:h + 1] - ck) * LOG2E
        if mask is not None:
            s = jnp.where(mask, s, -jnp.inf)
        m, l, acc = _softmax_step(s, vh, m_scr[h], l_scr[h], acc_scr[:, sl])
        m_scr[h] = m
        l_scr[h] = l
        acc_scr[:, sl] = acc

    for h in range(N_HEADS_A):
        sl = slice(h * dh, (h + 1) * dh)
        rows = pl.ds(h, kc_ref.shape[1] // N_HEADS_A, stride=N_HEADS_A)
        head_step(h, kc_ref[0, rows, :].astype(BF16), vc_ref[0, rows, :].astype(BF16),
                  cumtp_ref[0, h:h + 1, :], None)

    @pl.when(j == pl.num_programs(1) - 1)
    def _():
        row = lax.broadcasted_iota(jnp.int32, (ld, ld), 0)
        col = lax.broadcasted_iota(jnp.int32, (ld, ld), 1)
        for h in range(N_HEADS_A):
            sl = slice(h * dh, (h + 1) * dh)
            head_step(h, kn_ref[0, :, sl], vn_ref[0, :, sl], cumtn_ref[0, h:h + 1, :ld], col <= row)
            o_ref[0, :, sl] = (acc_scr[:, sl] / l_scr[h]).astype(o_ref.dtype)


def _attention_cached(q, kn, vn, kc, vc, cum_new, cumt_past, cumt_new):
    b, ld, _ = q.shape
    lp = kc.shape[1]
    tk = min(DEC_KV_TILE, lp)
    assert lp % tk == 0
    lpad = cum_new.shape[1]
    new = pl.BlockSpec((1, ld, D_ATTN), lambda bi, j: (bi, 0, 0))
    past = pl.BlockSpec((1, tk * N_HEADS_A, HEAD_DIM_A), lambda bi, j: (bi, j, 0))
    kc = kc.reshape(b, lp * N_HEADS_A, HEAD_DIM_A)
    vc = vc.reshape(b, lp * N_HEADS_A, HEAD_DIM_A)
    return pl.pallas_call(
        _attn_cached_kernel,
        grid=(b, lp // tk),
        in_specs=[new, new, new, past, past,
                  pl.BlockSpec((1, lpad, LANES), lambda bi, j: (bi, 0, 0)),
                  pl.BlockSpec((1, N_HEADS_A, tk), lambda bi, j: (bi, 0, j)),
                  pl.BlockSpec((1, N_HEADS_A, lpad), lambda bi, j: (bi, 0, 0))],
        out_specs=new,
        out_shape=jax.ShapeDtypeStruct((b, ld, D_ATTN), BF16),
        scratch_shapes=[pltpu.VMEM((N_HEADS_A, ld, 1), F32), pltpu.VMEM((N_HEADS_A, ld, 1), F32),
                        pltpu.VMEM((ld, D_ATTN), F32)],
        compiler_params=_params(("parallel", "arbitrary")),
    )(q, kn, vn, kc, vc, cum_new, cumt_past, cumt_new)


def _expand_heads(a):
    r = a.shape[0]
    low = lax.broadcasted_iota(jnp.int32, (r, LANES), 1) < HEAD_DIM_S
    return jnp.concatenate(
        [jnp.where(low, a[:, 2 * j:2 * j + 1], a[:, 2 * j + 1:2 * j + 2]) for j in range(N_HEADS_S // 2)], axis=1)


CONV_HIST = SUBLANES


def _ssd_kernel(xbc_ref, z_ref, dt_ref, dtt_ref, past_ref, h0_ref, cw_ref, cb_ref, dtb_ref, dtbt_ref,
                alog_ref, alogt_ref, dsk_ref, nw_ref, o_ref, hout_ref, xbuf, ht_scr, *, lc):
    c = pl.program_id(1)
    lb = xbc_ref.shape[1]
    hist = CONV_HIST

    @pl.when(c == 0)
    def _():
        xbuf[0:hist, :] = past_ref[0]
        ht_scr[...] = h0_ref[0]

    xbuf[hist:hist + lb, :] = xbc_ref[0]
    u_all = cb_ref[...]
    for w in range(CONV_W):
        off = hist - (CONV_W - 1) + w
        u_all = u_all + xbuf[off:off + lb, :] * cw_ref[w:w + 1, :]
    xbuf[0:hist, :] = xbuf[lb:lb + hist, :]
    u_all = _silu(u_all)

    gn = N_GROUPS_S * D_STATE
    half = D_INNER // N_GROUPS_S
    heads_per_group = N_HEADS_S // N_GROUPS_S
    row = lax.broadcasted_iota(jnp.int32, (lc, lc), 0)
    col = lax.broadcasted_iota(jnp.int32, (lc, lc), 1)
    causal = col <= row
    tri_lower = causal.astype(BF16)
    tri_upper = (row <= col).astype(BF16)
    low = lax.broadcasted_iota(jnp.int32, (lc, LANES), 1) < HEAD_DIM_S
    a_head = -jnp.exp(alog_ref[...])
    a_head_t = -jnp.exp(alogt_ref[...])
    ht = ht_scr[...]

    for cc in range(lb // lc):
        rows = slice(cc * lc, (cc + 1) * lc)
        u = u_all[rows]
        xs = u[:, :D_INNER]
        bm = u[:, D_INNER:D_INNER + gn].astype(BF16)
        cm = u[:, D_INNER + gn:].astype(BF16)

        dt = _softplus(dt_ref[0, rows, :] + dtb_ref[...])
        dtt = _softplus(dtt_ref[0, :, rows] + dtbt_ref[...])
        a_cs = _tri_dot(tri_lower, dt * a_head)
        a_cst = _dot_tri(dtt * a_head_t, tri_upper)
        total = a_cs[lc - 1:lc, :]
        dt_e = _expand_heads(dt)
        w_e = _expand_heads(dt * jnp.exp(total - a_cs))
        ea_e = _expand_heads(jnp.exp(a_cs))
        cd_e = _expand_heads(jnp.exp(total))
        xdt = xs * dt_e
        xdw = (xs * w_e).astype(BF16)

        cbs = [lax.dot_general(cm[:, g * D_STATE:(g + 1) * D_STATE], bm[:, g * D_STATE:(g + 1) * D_STATE], _NT,
                               preferred_element_type=F32) for g in range(N_GROUPS_S)]
        yd = []
        for j in range(N_HEADS_S // 2):
            ms = []
            for hh in (2 * j, 2 * j + 1):
                seg = a_cs[:, hh:hh + 1] - a_cst[hh:hh + 1, :]
                dec = jnp.where(causal, jnp.exp(jnp.where(causal, seg, 0.0)), 0.0)
                ms.append((cbs[hh // heads_per_group] * dec).astype(BF16))
            xb = xdt[:, j * LANES:(j + 1) * LANES]
            rhs = jnp.concatenate([jnp.where(low, xb, 0.0), jnp.where(low, 0.0, xb)], axis=0).astype(BF16)
            yd.append(jnp.dot(jnp.concatenate(ms, axis=1), rhs, preferred_element_type=F32))
        y = jnp.concatenate(yd, axis=1)

        htb = ht.astype(BF16)
        y_off = jnp.concatenate(
            [jnp.dot(cm[:, g * D_STATE:(g + 1) * D_STATE], htb[:, g * half:(g + 1) * half],
                     preferred_element_type=F32) for g in range(N_GROUPS_S)], axis=1)
        st = jnp.concatenate(
            [lax.dot_general(bm[:, g * D_STATE:(g + 1) * D_STATE], xdw[:, g * half:(g + 1) * half],
                             (((0,), (0,)), ((), ())), preferred_element_type=F32) for g in range(N_GROUPS_S)], axis=1)
        ht = cd_e * ht + st

        y = y + y_off * ea_e + dsk_ref[...] * xs
        gz = y * _silu(z_ref[0, rows, :])
        outs = []
        for g in range(N_GROUPS_S):
            gg = gz[:, g * half:(g + 1) * half]
            outs.append(gg * lax.rsqrt(jnp.mean(gg * gg, axis=-1, keepdims=True) + EPS))
        o_ref[0, rows, :] = (jnp.concatenate(outs, axis=1) * nw_ref[...]).astype(o_ref.dtype)

    ht_scr[...] = ht
    hout_ref[0] = ht


def _ssd(xbc, z, dt_raw, conv_past, h0t, conv_w, conv_b, dt_bias, a_log, d_skip, norm_w):
    b, l, _ = xbc.shape
    lc = min(SSD_CHUNK, l)
    lb = lc * SSD_STEP_CHUNKS if l % (lc * SSD_STEP_CHUNKS) == 0 else lc
    assert l % lb == 0 and lc % CONV_HIST == 0
    nh = N_HEADS_S
    dtt = jnp.swapaxes(dt_raw, 1, 2)
    past = jnp.pad(conv_past, ((0, 0), (CONV_HIST - (CONV_W - 1), 0), (0, 0)))
    const = lambda shape: pl.BlockSpec(shape, lambda i, j: tuple(0 for _ in shape))
    o, hout = pl.pallas_call(
        functools.partial(_ssd_kernel, lc=lc),
        grid=(b, l // lb),
        in_specs=[pl.BlockSpec((1, lb, CONV_DIM), lambda i, j: (i, j, 0)),
                  pl.BlockSpec((1, lb, D_INNER), lambda i, j: (i, j, 0)),
                  pl.BlockSpec((1, lb, nh), lambda i, j: (i, j, 0)),
                  pl.BlockSpec((1, nh, lb), lambda i, j: (i, 0, j)),
                  pl.BlockSpec((1, CONV_HIST, CONV_DIM), lambda i, j: (i, 0, 0)),
                  pl.BlockSpec((1, D_STATE, D_INNER), lambda i, j: (i, 0, 0)),
                  const((CONV_W, CONV_DIM)), const((1, CONV_DIM)),
                  const((1, nh)), const((nh, 1)), const((1, nh)), const((nh, 1)),
                  const((1, D_INNER)), const((1, D_INNER))],
        out_specs=[pl.BlockSpec((1, lb, D_INNER), lambda i, j: (i, j, 0)),
                   pl.BlockSpec((1, D_STATE, D_INNER), lambda i, j: (i, 0, 0))],
        out_shape=[jax.ShapeDtypeStruct((b, l, D_INNER), BF16),
                   jax.ShapeDtypeStruct((b, D_STATE, D_INNER), F32)],
        scratch_shapes=[pltpu.VMEM((lb + CONV_HIST, CONV_DIM), F32), pltpu.VMEM((D_STATE, D_INNER), F32)],
        compiler_params=_params(("parallel", "arbitrary")),
    )(xbc, z, dt_raw, dtt, past, h0t, conv_w, conv_b.reshape(1, -1),
      dt_bias.reshape(1, nh), dt_bias.reshape(nh, 1), a_log.reshape(1, nh), a_log.reshape(nh, 1),
      jnp.repeat(d_skip, HEAD_DIM_S).reshape(1, -1), norm_w.reshape(1, -1))
    return o, hout


def _outproj_kernel(x_ref, oa_ref, os_ref, gt_ref, sc_ref, sh_ref, g_ref, wa_ref, ws_ref, rw_ref, rb_ref,
                    x1_ref, h2_ref, lg_ref):
    nb, tl, d = x_ref.shape
    oa = oa_ref[...].reshape(nb * tl, -1)
    os_ = os_ref[...].reshape(nb * tl, -1)
    mix = jnp.dot(oa, wa_ref[...], preferred_element_type=F32) + jnp.dot(os_, ws_ref[...], preferred_element_type=F32)
    x1 = x_ref[...] + gt_ref[...] * mix.reshape(nb, tl, d)
    x1_ref[...] = x1
    ms = jnp.mean(x1 * x1, axis=-1, keepdims=True)
    h2 = x1 * lax.rsqrt(ms + EPS) * g_ref[...]
    h2 = h2 * (1.0 + sc_ref[...]) + sh_ref[...]
    h2 = h2.reshape(nb * tl, d)
    _store_row_tiles(h2_ref, h2)
    h_hi, h_lo = _split2(h2)
    w_hi, w_lo = rw_ref[0], rw_ref[1]
    lg = (jnp.dot(h_hi, w_hi, preferred_element_type=F32) + jnp.dot(h_lo, w_hi, preferred_element_type=F32)
          + jnp.dot(h_hi, w_lo, preferred_element_type=F32)) + rb_ref[...]
    lg_ref[...] = lg.reshape(nb, tl, -1)


def _outproj(x, oa, os_, gt, sc, sh, g, wa, ws, rw, rb):
    b, l, d = x.shape
    nb, tl = _row_blocking(b, l, ROW_TILE)
    row = lambda n: pl.BlockSpec((nb, tl, n), lambda i, j: (i, j, 0))
    mod = pl.BlockSpec((nb, 1, d), lambda i, j: (i, 0, 0))
    const = lambda a: pl.BlockSpec(a.shape, lambda i, j: tuple(0 for _ in a.shape))
    g3 = g.reshape(1, 1, d)
    return pl.pallas_call(
        _outproj_kernel,
        grid=(b // nb, l // tl),
        in_specs=[row(d), row(D_ATTN), row(D_INNER), mod, mod, mod, const(g3), const(wa), const(ws),
                  const(rw), const(rb)],
        out_specs=[row(d), pl.BlockSpec((nb * tl * ROW_TILE_SUBLANES, LANES), lambda i, j: (i * (l // tl) + j, 0)),
                   row(LANES)],
        out_shape=[jax.ShapeDtypeStruct((b, l, d), F32), jax.ShapeDtypeStruct((b * l * ROW_TILE_SUBLANES, LANES), F32),
                   jax.ShapeDtypeStruct((b, l, LANES), F32)],
        compiler_params=_params(("parallel", "parallel")),
    )(x, oa, os_, gt.reshape(b, 1, d), sc.reshape(b, 1, d), sh.reshape(b, 1, d), g3, wa, ws, rw, rb)


def _route_kernel(lg_ref, idx_ref, gate_ref, cnt_ref, carry_scr):
    @pl.when(pl.program_id(0) == 0)
    def _():
        carry_scr[...] = jnp.zeros(carry_scr.shape, F32)

    v = lg_ref[...]
    tr = v.shape[0]
    lane = lax.broadcasted_iota(jnp.int32, (tr, LANES), 1)
    lane_f = lane.astype(F32)
    tops, idxs = [], []
    onehot = jnp.zeros((tr, LANES), F32)
    for _ in range(TOP_K):
        m = jnp.max(v, axis=-1, keepdims=True)
        idx = jnp.min(jnp.where(v == m, lane_f, float(LANES)), axis=-1, keepdims=True)
        hit = lane_f == idx
        v = jnp.where(hit, -jnp.inf, v)
        onehot = onehot + hit.astype(F32)
        tops.append(m)
        idxs.append(idx)
    es = [jnp.exp(t - tops[0]) for t in tops]
    denom = es[0] + es[1] + es[2] + es[3]

    row = lax.broadcasted_iota(jnp.int32, (tr, tr), 0)
    col = lax.broadcasted_iota(jnp.int32, (tr, tr), 1)
    before = jnp.dot((col < row).astype(BF16), onehot.astype(BF16), preferred_element_type=F32) + carry_scr[...]

    idx_out = jnp.zeros((tr, LANES), F32)
    gate_out = jnp.zeros((tr, LANES), F32)
    for k in range(TOP_K):
        rank = jnp.sum(jnp.where(lane_f == idxs[k], before, 0.0), axis=-1, keepdims=True)
        idx_out = jnp.where(lane == k, idxs[k], idx_out)
        idx_out = jnp.where(lane == TOP_K + k, rank, idx_out)
        gate_out = jnp.where(lane == k, es[k] / denom, gate_out)
    idx_ref[...] = idx_out.astype(jnp.int32)
    gate_ref[...] = gate_out
    carry_scr[...] = carry_scr[...] + jnp.sum(onehot, axis=0, keepdims=True)
    cnt_ref[...] = carry_scr[...]


def _route(logits):
    t, n = logits.shape
    tr = min(ROUTE_TILE, t)
    assert t % tr == 0
    return pl.pallas_call(
        _route_kernel,
        grid=(t // tr,),
        in_specs=[pl.BlockSpec((tr, n), lambda i: (i, 0))],
        out_specs=[pl.BlockSpec((tr, n), lambda i: (i, 0)), pl.BlockSpec((tr, n), lambda i: (i, 0)),
                   pl.BlockSpec((1, n), lambda i: (0, 0))],
        out_shape=[jax.ShapeDtypeStruct((t, n), jnp.int32), jax.ShapeDtypeStruct((t, n), F32),
                   jax.ShapeDtypeStruct((1, n), F32)],
        scratch_shapes=[pltpu.VMEM((1, n), F32)],
        compiler_params=_params(("arbitrary",)),
    )(logits)


def _row_tile(ref, r):
    start = r * ROW_TILE_SUBLANES
    if not isinstance(r, int):
        start = pl.multiple_of(start, ROW_TILE_SUBLANES)
    return ref.at[pl.ds(start, ROW_TILE_SUBLANES), :]


def _store_row_tiles(ref, x, tile0=0):
    rows = x.shape[0]
    for s in range(x.shape[1] // LANES):
        ref[pl.ds(tile0 + s, rows, stride=ROW_TILE_SUBLANES), :] = x[:, s * LANES:(s + 1) * LANES]


def _load_row_tiles(ref, row0, rows):
    return jnp.concatenate(
        [ref[pl.ds(row0 * ROW_TILE_SUBLANES + s, rows, stride=ROW_TILE_SUBLANES), :] for s in range(ROW_TILE_SUBLANES)],
        axis=1)


def _gather_rows(idx_ref, src_hbm, dst, sem, n_rows):
    group = SUBLANES
    assert n_rows % group == 0

    def body(g, carry):
        for u in range(group):
            r = g * group + u
            pltpu.make_async_copy(_row_tile(src_hbm, idx_ref[0, 0, r]), _row_tile(dst, r), sem).start(priority=u % 2)
        return carry
    lax.fori_loop(0, n_rows // group, body, 0)


def _wait_rows(src_hbm, dst, sem, n_rows):
    pltpu.make_async_copy(src_hbm.at[pl.ds(0, n_rows * ROW_TILE_SUBLANES), :], dst, sem).wait()


EXPERT_COL_CHUNK = 256


GATHER_BUFS = 3

def _dispatch_kernel(ends_ref, pad_ref, dest_ref, h_hbm, xs_hbm, zbuf, hbuf, zsem, fsem, sem, *, n_slots):
    i = pl.program_id(0)
    n = pl.num_programs(0)
    m = zbuf.shape[0] // ROW_TILE_SUBLANES
    rows = dest_ref.shape[2]
    td = rows // TOP_K
    slot = lax.rem(i, GATHER_BUFS)

    def fetch(step, s):
        start = pl.multiple_of(step * (td * ROW_TILE_SUBLANES), td * ROW_TILE_SUBLANES)
        return pltpu.make_async_copy(h_hbm.at[pl.ds(start, td * ROW_TILE_SUBLANES), :], hbuf.at[s], fsem.at[s])

    def wait_rows_of(s):
        for _ in range(TOP_K):
            pltpu.make_async_copy(hbuf.at[s], xs_hbm.at[pl.ds(0, td * ROW_TILE_SUBLANES), :], sem.at[s]).wait()

    def zero_block(first_slot):
        start = pl.multiple_of(first_slot * ROW_TILE_SUBLANES, ROW_TILE_SUBLANES)
        return pltpu.make_async_copy(zbuf, xs_hbm.at[pl.ds(start, zbuf.shape[0]), :], zsem)

    @pl.when(i == 0)
    def _():
        zbuf[...] = jnp.zeros(zbuf.shape, zbuf.dtype)
        jobs = [(pad_ref[e] > 0, ends_ref[e] - m) for e in range(N_EXPERTS)]
        jobs += [(ends_ref[N_EXPERTS - 1] + j * m < n_slots, ends_ref[N_EXPERTS - 1] + j * m)
                 for j in range(N_EXPERTS + 1)]
        for cond, first in jobs:
            @pl.when(cond)
            def _():
                zero_block(first).start()
        for cond, first in jobs:
            @pl.when(cond)
            def _():
                zero_block(first).wait()
        fetch(0, 0).start()

    @pl.when(i >= 2)
    def _():
        wait_rows_of(lax.rem(i + 1, GATHER_BUFS))

    @pl.when(i + 1 < n)
    def _():
        fetch(i + 1, lax.rem(i + 1, GATHER_BUFS)).start()

    fetch(i, slot).wait()
    for s in range(GATHER_BUFS):
        @pl.when(slot == s)
        def _():
            for j in range(rows):
                pltpu.make_async_copy(_row_tile(hbuf.at[s], j // TOP_K), _row_tile(xs_hbm, dest_ref[0, 0, j]),
                                      sem.at[s]).start(priority=j % 2)

    @pl.when(i == n - 1)
    def _():
        @pl.when(i >= 1)
        def _():
            wait_rows_of(lax.rem(i + 2, GATHER_BUFS))
        wait_rows_of(slot)


def _dispatch(dest, ends_p, padded, h2, m, n_slots):
    t = dest.shape[0]
    td = min(DISPATCH_TILE, t)
    assert t % td == 0
    n_steps = t // td
    grid_spec = pltpu.PrefetchScalarGridSpec(
        num_scalar_prefetch=2,
        grid=(n_steps,),
        in_specs=[pl.BlockSpec((1, 1, TOP_K * td), lambda i, e, p: (i, 0, 0), memory_space=pltpu.SMEM),
                  pl.BlockSpec(memory_space=pl.ANY)],
        out_specs=pl.BlockSpec(memory_space=pl.ANY),
        scratch_shapes=[pltpu.VMEM((m * ROW_TILE_SUBLANES, LANES), F32),
                        pltpu.VMEM((GATHER_BUFS, td * ROW_TILE_SUBLANES, LANES), F32),
                        pltpu.SemaphoreType.DMA(()), pltpu.SemaphoreType.DMA((GATHER_BUFS,)),
                        pltpu.SemaphoreType.DMA((GATHER_BUFS,))],
    )
    return pl.pallas_call(
        functools.partial(_dispatch_kernel, n_slots=n_slots),
        grid_spec=grid_spec,
        out_shape=jax.ShapeDtypeStruct((n_slots * ROW_TILE_SUBLANES, LANES), F32),
        compiler_params=_params(("arbitrary",)),
    )(ends_p, padded, dest.reshape(n_steps, 1, TOP_K * td), h2)


def _experts_kernel(be_ref, used_ref, x_ref, wgu_ref, bgu_ref, wdn_ref, bdn_ref, o_ref):
    m = x_ref.shape[0] // ROW_TILE_SUBLANES
    cw = EXPERT_COL_CHUNK
    tiles_per_piece = cw // LANES
    n_ff, n_out = D_FF // cw, ROW_TILE_SUBLANES // tiles_per_piece
    in_use = pl.program_id(0) < used_ref[0]

    @pl.when(in_use)
    def _():
        x = _load_row_tiles(x_ref, 0, m).astype(BF16)
        acts = []
        for c in range(n_ff):
            g = (jnp.dot(x, wgu_ref[0, :, c * cw:(c + 1) * cw], preferred_element_type=F32)
                 + bgu_ref[0, :, c * cw:(c + 1) * cw])
            u = (jnp.dot(x, wgu_ref[0, :, D_FF + c * cw:D_FF + (c + 1) * cw], preferred_element_type=F32)
                 + bgu_ref[0, :, D_FF + c * cw:D_FF + (c + 1) * cw])
            gate = jnp.minimum(g, SWIGLU_LIMIT)
            up = jnp.clip(u, -SWIGLU_LIMIT, SWIGLU_LIMIT)
            acts.append(((up + 1.0) * gate * jax.nn.sigmoid(SWIGLU_ALPHA * gate)).astype(BF16))
        act = jnp.concatenate(acts, axis=1)
        for c in range(n_out):
            y = (jnp.dot(act, wdn_ref[0, :, c * cw:(c + 1) * cw], preferred_element_type=F32)
                 + bdn_ref[0, :, c * cw:(c + 1) * cw])
            _store_row_tiles(o_ref, y, tile0=c * tiles_per_piece)

    @pl.when(jnp.logical_not(in_use))
    def _():
        o_ref[...] = jnp.zeros(o_ref.shape, o_ref.dtype)


def _experts(block_e, n_used, xs, wgu, bgu, wdn, bdn, m):
    n_blocks = block_e.shape[0]
    d = D_MODEL
    mt = m * ROW_TILE_SUBLANES
    grid_spec = pltpu.PrefetchScalarGridSpec(
        num_scalar_prefetch=2,
        grid=(n_blocks,),
        in_specs=[pl.BlockSpec((mt, LANES), lambda i, be, nu: (jnp.minimum(i, nu[0] - 1), 0)),
                  pl.BlockSpec((1, d, 2 * D_FF), lambda i, be, nu: (be[i], 0, 0)),
                  pl.BlockSpec((1, 1, 2 * D_FF), lambda i, be, nu: (be[i], 0, 0)),
                  pl.BlockSpec((1, D_FF, d), lambda i, be, nu: (be[i], 0, 0)),
                  pl.BlockSpec((1, 1, d), lambda i, be, nu: (be[i], 0, 0))],
        out_specs=pl.BlockSpec((mt, LANES), lambda i, be, nu: (i, 0)),
    )
    return pl.pallas_call(
        _experts_kernel,
        grid_spec=grid_spec,
        out_shape=jax.ShapeDtypeStruct((n_blocks * mt, LANES), F32),
        compiler_params=_params(("arbitrary",)),
    )(block_e, n_used, xs, wgu, bgu.reshape(N_EXPERTS, 1, -1), wdn, bdn.reshape(N_EXPERTS, 1, -1))


def _combine_kernel(idx0_ref, idx1_ref, idxn_ref, y_hbm, x1_ref, gate_ref, gt_ref, g_ref, o_ref, ybuf, sem):
    i = pl.program_id(0)
    n = pl.num_programs(0)
    nb, tl, d = x1_ref.shape
    tc = nb * tl
    slot = lax.rem(i, GATHER_BUFS)
    nxt = lax.rem(i + 2, GATHER_BUFS)

    @pl.when(i == 0)
    def _():
        _gather_rows(idx0_ref, y_hbm, ybuf.at[0], sem.at[0], TOP_K * tc)
        _gather_rows(idx1_ref, y_hbm, ybuf.at[1], sem.at[1], TOP_K * tc)

    _wait_rows(y_hbm, ybuf.at[slot], sem.at[slot], TOP_K * tc)

    def prefetch(k):
        for r in range(k * tc, (k + 1) * tc):
            pltpu.make_async_copy(_row_tile(y_hbm, idxn_ref[0, 0, r]), _row_tile(ybuf.at[nxt], r),
                                  sem.at[nxt]).start(priority=r % 2)

    gates = gate_ref[...]
    y = gates[:, 0:1] * _load_row_tiles(ybuf.at[slot], 0, tc)
    prefetch(0)
    for k in range(1, TOP_K):
        y = y + gates[:, k:k + 1] * _load_row_tiles(ybuf.at[slot], k * tc, tc)
        prefetch(k)
    x2 = x1_ref[...] + gt_ref[...] * y.reshape(nb, tl, d)
    ms = jnp.mean(x2 * x2, axis=-1, keepdims=True)
    o_ref[...] = x2 * lax.rsqrt(ms + EPS) * g_ref[...]

    @pl.when(i == n - 1)
    def _():
        for ahead in (1, 2):
            s = lax.rem(i + ahead, GATHER_BUFS)
            _wait_rows(y_hbm, ybuf.at[s], sem.at[s], TOP_K * tc)


def _combine(dest, y_slots, x1, gates, gt, g_final):
    b, l, d = x1.shape
    nb, tl = _row_blocking(b, l, COMBINE_TILE)
    tc = nb * tl
    t = b * l
    n_steps = t // tc
    idx = jnp.swapaxes(dest.reshape(n_steps, tc, TOP_K), 1, 2).reshape(n_steps, 1, TOP_K * tc)
    steps_per_batch_row = l // tl
    tok = lambda i: (i // steps_per_batch_row, i % steps_per_batch_row, 0)
    smem = lambda f: pl.BlockSpec((1, 1, TOP_K * tc), f, memory_space=pltpu.SMEM)
    return pl.pallas_call(
        _combine_kernel,
        grid=(n_steps,),
        in_specs=[smem(lambda i: (0, 0, 0)),
                  smem(lambda i: (min(1, n_steps - 1), 0, 0)),
                  smem(lambda i: (jnp.minimum(i + 2, n_steps - 1), 0, 0)),
                  pl.BlockSpec(memory_space=pl.ANY),
                  pl.BlockSpec((nb, tl, d), tok),
                  pl.BlockSpec((tc, LANES), lambda i: (i, 0)),
                  pl.BlockSpec((nb, 1, d), lambda i: (i // steps_per_batch_row, 0, 0)),
                  pl.BlockSpec((1, 1, d), lambda i: (0, 0, 0))],
        out_specs=pl.BlockSpec((nb, tl, d), tok),
        out_shape=jax.ShapeDtypeStruct((b, l, d), F32),
        scratch_shapes=[pltpu.VMEM((GATHER_BUFS, TOP_K * tc * ROW_TILE_SUBLANES, LANES), F32),
                        pltpu.SemaphoreType.DMA((GATHER_BUFS,))],
        compiler_params=_params(("arbitrary",)),
    )(idx, idx, idx, y_slots, x1, gates, gt.reshape(b, 1, d), g_final.reshape(1, 1, d))


def _moe(h2, logits, x1, gt_m, g_final, wts):
    b, l, d = x1.shape
    t = b * l
    m = MOE_ROWS if t * TOP_K >= 4 * N_EXPERTS * MOE_ROWS else MOE_ROWS_SMALL
    idx, gates, counts = _route(logits.reshape(t, LANES))
    top_e = idx[:, :TOP_K]
    rank = idx[:, TOP_K:2 * TOP_K]
    counts = counts[0, :N_EXPERTS].astype(jnp.int32)
    padded = (counts + m - 1) // m * m
    ends_p = jnp.cumsum(padded)
    starts_p = ends_p - padded
    experts = jnp.arange(N_EXPERTS, dtype=jnp.int32)
    start_of = jnp.sum(jnp.where(top_e[..., None] == experts, starts_p, 0), axis=-1)
    dest = start_of + rank
    n_blocks = (t * TOP_K + N_EXPERTS * (m - 1) + m - 1) // m
    block_start = jnp.arange(n_blocks, dtype=jnp.int32)[:, None] * m
    block_e = jnp.minimum(jnp.sum((ends_p[None, :] <= block_start).astype(jnp.int32), axis=1), N_EXPERTS - 1)
    xs = _dispatch(dest, ends_p, padded, h2, m, n_blocks * m)
    n_used = (ends_p[N_EXPERTS - 1:] // m).astype(jnp.int32)
    y_slots = _experts(block_e, n_used, xs, *wts, m)
    return _combine(dest, y_slots, x1, gates, gt_m, g_final)


def _layer(x, mod, k_past, v_past, logf_past, conv_past, ssm_past, p, g_final):
    b, l, d = x.shape
    sh_a, sc_a, gt_a, sh_m, sc_m, gt_m = jnp.split(mod, 6, axis=-1)
    q, kf, vf, kb, vb, z, xbc, sm = _inproj(x, p['g_mix'], sc_a, sh_a, *p['w_in'])
    dt_raw = sm[:, :, N_HEADS_A:N_HEADS_A + N_HEADS_S]

    if k_past is None:
        logf, qx, kx = _cumsum_bias(sm, p['b_f'])
        o_a = _attention(q, qx, kb, kx, vb)
        conv_past = jnp.zeros((b, CONV_W - 1, CONV_DIM), F32)
        h0t = jnp.zeros((b, D_STATE, D_INNER), F32)
    else:
        lp = k_past.shape[1]
        past_t = jnp.swapaxes(logf_past, 1, 2).reshape(b * N_HEADS_A, lp)
        cumt_p = _cumsum_lanes(past_t).reshape(b, N_HEADS_A, lp)
        carry = jnp.pad(cumt_p[:, :, lp - 1], ((0, 0), (0, LANES - N_HEADS_A))).reshape(b, 1, LANES)
        lpad = -(-l // LANES) * LANES
        sm_pad = jnp.pad(sm, ((0, 0), (0, lpad - l), (0, 0)))
        logf, cum_n, cumt_n = _cumsum(sm_pad, p['b_f'], carry)
        logf = logf[:, :l]
        o_a = _attention_cached(q, kb, vb, k_past, v_past, cum_n, cumt_p, cumt_n)
        h0t = jnp.swapaxes(ssm_past.reshape(b, D_INNER, D_STATE), 1, 2)
    o_s, hout = _ssd(xbc, z, dt_raw, conv_past, h0t, p['conv_w'], p['conv_b'], p['dt_bias'], p['a_log'],
                     p['d_skip'], p['ssd_norm_w'])
    x1, h2, logits = _outproj(x, o_a, o_s, gt_a, sc_m, sh_m, p['g_ffn'], p['w_out_a'], p['w_out_s'],
                              p['router_w'], p['router_b'])
    y = _moe(h2, logits, x1, gt_m, g_final, p['experts'])

    assert l >= CONV_W - 1
    conv_new = xbc[:, l - (CONV_W - 1):]
    ssm_new = jnp.swapaxes(hout, 1, 2).reshape(b, N_HEADS_S, HEAD_DIM_S, D_STATE)
    return (y, kf.reshape(b, l, N_HEADS_A, HEAD_DIM_A), vf.reshape(b, l, N_HEADS_A, HEAD_DIM_A),
            logf[:, :, :N_HEADS_A], conv_new, ssm_new)


def kernel(x_prompt, x_sample, cache_k, cache_v, cache_logf, state_conv, state_ssm, c_prompt, c_sample, w_ada, b_ada, g_mix, w_in, b_f, conv_w, conv_b, dt_bias, a_log, d_skip, ssd_norm_w, w_out, g_ffn, router_w, router_b, w_gate_up, b_gate_up, w_down, b_down, g_final):
    assert w_ada.shape[0] == 1, "single-layer operation"
    bp = x_prompt.shape[0]
    w = w_in[0]
    q_end, k_end, v_end = D_ATTN, 2 * D_ATTN, 3 * D_ATTN
    f_end = v_end + N_HEADS_A
    z_end = f_end + D_INNER
    xbc_end = z_end + CONV_DIM
    w_small = jnp.concatenate(
        [w[:, v_end:f_end], w[:, xbc_end:], jnp.zeros((D_MODEL, LANES - N_HEADS_A - N_HEADS_S), F32)], axis=1)
    cast = lambda a: a.astype(BF16)
    p = {
        'g_mix': g_mix[0],
        'w_in': (cast(w[:, :q_end]), cast(w[:, q_end:k_end]), cast(w[:, k_end:v_end]), cast(w[:, f_end:z_end]),
                 cast(w[:, z_end:xbc_end]), cast(w_small)),
        'b_f': jnp.pad(b_f[0], (0, LANES - N_HEADS_A)).reshape(1, LANES),
        'conv_w': conv_w[0], 'conv_b': conv_b[0], 'dt_bias': dt_bias[0], 'a_log': a_log[0],
        'd_skip': d_skip[0], 'ssd_norm_w': ssd_norm_w[0],
        'w_out_a': cast(w_out[0, :D_ATTN]), 'w_out_s': cast(w_out[0, D_ATTN:]),
        'g_ffn': g_ffn[0],
        'router_w': jnp.stack(_split2(jnp.pad(router_w[0], ((0, 0), (0, LANES - N_EXPERTS))))),
        'router_b': jnp.pad(router_b[0], (0, LANES - N_EXPERTS), constant_values=-jnp.inf).reshape(1, LANES),
        'experts': (cast(w_gate_up[0]), b_gate_up[0], cast(w_down[0]), b_down[0]),
    }
    mod = _adaln(jnp.concatenate([c_prompt, c_sample], axis=0), w_ada[0], b_ada[0])
    outs_p = _layer(x_prompt, mod[:bp], None, None, None, None, None, p, g_final)
    outs_s = _layer(x_sample, mod[bp:], cache_k[0], cache_v[0], cache_logf[0], state_conv[0], state_ssm[0], p, g_final)
    stack = lambda a: a[None]
    return (outs_p[0], outs_s[0]) + tuple(stack(a) for a in outs_p[1:]) + tuple(stack(a) for a in outs_s[1:])
```

```python
import functools
import math

import jax
import jax.numpy as jnp
from jax import lax
from jax.experimental import pallas as pl
from jax.experimental.pallas import tpu as pltpu

F32 = jnp.float32
BF16 = jnp.bfloat16
HIGHEST = lax.Precision.HIGHEST

D_MODEL = 1024
N_HEADS_A = 8
HEAD_DIM_A = 128
D_ATTN = N_HEADS_A * HEAD_DIM_A
D_INNER = 1024
HEAD_DIM_S = 64
N_HEADS_S = D_INNER // HEAD_DIM_S
N_GROUPS_S = 2
D_STATE = 128
CONV_W = 4
CONV_DIM = D_INNER + 2 * N_GROUPS_S * D_STATE
N_EXPERTS = 32
TOP_K = 4
D_FF = 1024
SWIGLU_LIMIT = 7.0
SWIGLU_ALPHA = 1.702
EPS = 1e-5
LANES = 128
SUBLANES = 8
ROW_TILE_SUBLANES = D_MODEL // LANES
assert ROW_TILE_SUBLANES == SUBLANES
VMEM_LIMIT = 56 * 1024 * 1024
LOG2E = 1.4426950408889634
ATT_HEADS_PER_STEP = 4

ADALN_COL_TILE = 512
ROW_TILE = 512
ATT_TILE = 512
DEC_KV_TILE = 2048
CUM_TILE = 512
SSD_CHUNK = 128
SSD_STEP_CHUNKS = 2
ROUTE_TILE = 1024
MOE_ROWS = 512
MOE_ROWS_SMALL = 128
COMBINE_TILE = 256
DISPATCH_TILE = 256

def _params(sem):
    return pltpu.CompilerParams(dimension_semantics=sem, vmem_limit_bytes=VMEM_LIMIT)


def _row_blocking(b, l, tile):
    if l >= tile:
        assert l % tile == 0
        return 1, tile
    nb = min(b, max(1, tile // l))
    while b % nb:
        nb -= 1
    return nb, l


def _split3(x):
    hi = x.astype(BF16)
    r1 = x - hi.astype(F32)
    mid = r1.astype(BF16)
    lo = (r1 - mid.astype(F32)).astype(BF16)
    return hi, mid, lo


def _split2(x):
    hi = x.astype(BF16)
    return hi, (x - hi.astype(F32)).astype(BF16)


def _tri_dot(tri, x):
    hi, mid, lo = _split3(x)
    d = lambda p: jnp.dot(tri, p, preferred_element_type=F32)
    return d(hi) + d(mid) + d(lo)


def _dot_tri(x, tri):
    hi, mid, lo = _split3(x)
    d = lambda p: jnp.dot(p, tri, preferred_element_type=F32)
    return d(hi) + d(mid) + d(lo)


def _silu(x):
    return x * jax.nn.sigmoid(x)


def _softplus(x):
    return jnp.maximum(x, 0.0) + jnp.log1p(jnp.exp(-jnp.abs(x)))


def _log_sigmoid(x):
    return jnp.minimum(x, 0.0) - jnp.log1p(jnp.exp(-jnp.abs(x)))


def _adaln_kernel(c_ref, w_ref, b_ref, o_ref):
    s = _silu(c_ref[...])
    o_ref[...] = jnp.dot(s, w_ref[...], precision=HIGHEST, preferred_element_type=F32) + b_ref[...]


def _adaln(c, w, b):
    m, d = c.shape
    n = w.shape[1]
    tn = ADALN_COL_TILE
    assert n % tn == 0
    return pl.pallas_call(
        _adaln_kernel,
        grid=(n // tn,),
        in_specs=[pl.BlockSpec((m, d), lambda j: (0, 0)),
                  pl.BlockSpec((d, tn), lambda j: (0, j)),
                  pl.BlockSpec((1, tn), lambda j: (0, j))],
        out_specs=pl.BlockSpec((m, tn), lambda j: (0, j)),
        out_shape=jax.ShapeDtypeStruct((m, n), F32),
        compiler_params=_params(("parallel",)),
    )(c, w, b.reshape(1, n))


def _inproj_kernel(x_ref, g_ref, sc_ref, sh_ref, wq_ref, wk_ref, wv_ref, wz_ref, wx_ref, ws_ref,
                   q_ref, kf_ref, vf_ref, kb_ref, vb_ref, z_ref, xbc_ref, sm_ref):
    x = x_ref[...]
    nb, tl, d = x.shape
    ms = jnp.mean(x * x, axis=-1, keepdims=True)
    h = x * lax.rsqrt(ms + EPS) * g_ref[...]
    h = h * (1.0 + sc_ref[...]) + sh_ref[...]
    hb = h.reshape(nb * tl, d).astype(BF16)

    def mm(w_ref):
        return jnp.dot(hb, w_ref[...], preferred_element_type=F32)

    q = mm(wq_ref) * (LOG2E / math.sqrt(HEAD_DIM_A))
    q_ref[...] = q.astype(BF16).reshape(nb, tl, -1)
    k = mm(wk_ref)
    kf_ref[...] = k.reshape(nb, tl, -1)
    kb_ref[...] = k.astype(BF16).reshape(nb, tl, -1)
    v = mm(wv_ref)
    vf_ref[...] = v.reshape(nb, tl, -1)
    vb_ref[...] = v.astype(BF16).reshape(nb, tl, -1)
    z_ref[...] = mm(wz_ref).reshape(nb, tl, -1)
    xbc_ref[...] = mm(wx_ref).reshape(nb, tl, -1)
    sm_ref[...] = mm(ws_ref).reshape(nb, tl, -1)


def _inproj(x, g, sc, sh, wq, wk, wv, wz, wx, ws):
    b, l, d = x.shape
    nb, tl = _row_blocking(b, l, ROW_TILE)
    grid = (b // nb, l // tl)
    row = lambda n: pl.BlockSpec((nb, tl, n), lambda i, j: (i, j, 0))
    mod = pl.BlockSpec((nb, 1, d), lambda i, j: (i, 0, 0))
    wspec = lambda w: pl.BlockSpec(w.shape, lambda i, j: (0, 0), pipeline_mode=pl.Buffered(1))
    outs = [(D_ATTN, BF16), (D_ATTN, F32), (D_ATTN, F32), (D_ATTN, BF16), (D_ATTN, BF16),
            (D_INNER, F32), (CONV_DIM, F32), (LANES, F32)]
    return pl.pallas_call(
        _inproj_kernel,
        grid=grid,
        in_specs=[row(d), pl.BlockSpec((1, 1, d), lambda i, j: (0, 0, 0)), mod, mod,
                  wspec(wq), wspec(wk), wspec(wv), wspec(wz), wspec(wx), wspec(ws)],
        out_specs=[row(n) for n, _ in outs],
        out_shape=[jax.ShapeDtypeStruct((b, l, n), dt) for n, dt in outs],
        compiler_params=_params(("parallel", "parallel")),
    )(x, g.reshape(1, 1, d), sc.reshape(b, 1, d), sh.reshape(b, 1, d), wq, wk, wv, wz, wx, ws)


def _running_sum(v_ref, bias_ref, carry_ref, carry_scr):
    @pl.when(pl.program_id(1) == 0)
    def _():
        carry_scr[...] = carry_ref[0]

    v = v_ref[0]
    tl = v.shape[0]
    lf = _log_sigmoid(v + bias_ref[...])
    row = lax.broadcasted_iota(jnp.int32, (tl, tl), 0)
    col = lax.broadcasted_iota(jnp.int32, (tl, tl), 1)
    tri = (row >= col).astype(BF16)
    cs = _tri_dot(tri, lf) + carry_scr[...]
    carry_scr[...] = cs[tl - 1:tl, :]
    return lf, cs


def _cumsum_kernel(v_ref, bias_ref, carry_ref, logf_ref, cum_ref, cumt_ref, carry_scr):
    lf, cs = _running_sum(v_ref, bias_ref, carry_ref, carry_scr)
    logf_ref[0] = lf
    cum_ref[0] = cs
    cumt_ref[0] = cs.T[:N_HEADS_A, :]


def _cumsum(vals, bias, carry):
    b, l, n = vals.shape
    tl = min(CUM_TILE, l)
    assert l % tl == 0 and tl % LANES == 0
    return pl.pallas_call(
        _cumsum_kernel,
        grid=(b, l // tl),
        in_specs=[pl.BlockSpec((1, tl, n), lambda i, j: (i, j, 0)),
                  pl.BlockSpec((1, n), lambda i, j: (0, 0)),
                  pl.BlockSpec((1, 1, n), lambda i, j: (i, 0, 0))],
        out_specs=[pl.BlockSpec((1, tl, n), lambda i, j: (i, j, 0)),
                   pl.BlockSpec((1, tl, n), lambda i, j: (i, j, 0)),
                   pl.BlockSpec((1, N_HEADS_A, tl), lambda i, j: (i, 0, j))],
        out_shape=[jax.ShapeDtypeStruct((b, l, n), F32),
                   jax.ShapeDtypeStruct((b, l, n), F32),
                   jax.ShapeDtypeStruct((b, N_HEADS_A, l), F32)],
        scratch_shapes=[pltpu.VMEM((1, n), F32)],
        compiler_params=_params(("parallel", "arbitrary")),
    )(vals, bias, carry)


def _cumsum_lanes_kernel(x_ref, o_ref, carry_scr):
    @pl.when(pl.program_id(0) == 0)
    def _():
        carry_scr[...] = jnp.zeros(carry_scr.shape, F32)

    x = x_ref[...]
    tl = x.shape[1]
    row = lax.broadcasted_iota(jnp.int32, (tl, tl), 0)
    col = lax.broadcasted_iota(jnp.int32, (tl, tl), 1)
    cs = _dot_tri(x, (row <= col).astype(BF16)) + carry_scr[...]
    o_ref[...] = cs
    carry_scr[...] = cs[:, tl - 1:tl]


def _cumsum_lanes(x):
    rows, l = x.shape
    tl = min(CUM_TILE, l)
    assert l % tl == 0 and rows % SUBLANES == 0
    return pl.pallas_call(
        _cumsum_lanes_kernel,
        grid=(l // tl,),
        in_specs=[pl.BlockSpec((rows, tl), lambda j: (0, j))],
        out_specs=pl.BlockSpec((rows, tl), lambda j: (0, j)),
        out_shape=jax.ShapeDtypeStruct((rows, l), F32),
        scratch_shapes=[pltpu.VMEM((rows, 1), F32)],
        compiler_params=_params(("arbitrary",)),
    )(x)


N_BIAS_TERMS = 3


def _bias_placement():
    rows = jnp.arange(N_BIAS_TERMS * LANES)
    n, h = rows // LANES, rows % LANES
    col = jnp.arange(D_ATTN)[None, :]
    valid = (h < N_HEADS_A)[:, None]
    eq = jnp.where(valid & (col == (h * LANES + n)[:, None]), 1.0, 0.0)
    ek = jnp.where(valid & (col == (h * LANES + N_BIAS_TERMS + n)[:, None]), -1.0, 0.0)
    lane = jnp.arange(D_ATTN) % LANES
    ones_q = jnp.where((lane >= N_BIAS_TERMS) & (lane < 2 * N_BIAS_TERMS), 1.0, 0.0)
    ones_k = jnp.where(lane < N_BIAS_TERMS, 1.0, 0.0)
    return eq.astype(BF16), ek.astype(BF16), ones_q.reshape(1, -1).astype(F32), ones_k.reshape(1, -1).astype(F32)


def _cumsum_bias_kernel(v_ref, bias_ref, carry_ref, eq_ref, ek_ref, oq_ref, ok_ref, logf_ref, qx_ref, kx_ref, carry_scr):
    lf, cs = _running_sum(v_ref, bias_ref, carry_ref, carry_scr)
    logf_ref[0] = lf
    terms = jnp.concatenate(_split3(cs * LOG2E), axis=1)
    qx_ref[0] = (jnp.dot(terms, eq_ref[...], preferred_element_type=F32) + oq_ref[...]).astype(BF16)
    kx_ref[0] = (jnp.dot(terms, ek_ref[...], preferred_element_type=F32) + ok_ref[...]).astype(BF16)


def _cumsum_bias(vals, bias):
    b, l, n = vals.shape
    tl = min(CUM_TILE, l)
    assert l % tl == 0
    row = lambda w: pl.BlockSpec((1, tl, w), lambda i, j: (i, j, 0))
    const = lambda a: pl.BlockSpec(a.shape, lambda i, j: (0, 0))
    placement = _bias_placement()
    return pl.pallas_call(
        _cumsum_bias_kernel,
        grid=(b, l // tl),
        in_specs=[row(n), pl.BlockSpec((1, n), lambda i, j: (0, 0)), pl.BlockSpec((1, 1, n), lambda i, j: (i, 0, 0))]
        + [const(a) for a in placement],
        out_specs=[row(n), row(D_ATTN), row(D_ATTN)],
        out_shape=[jax.ShapeDtypeStruct((b, l, n), F32), jax.ShapeDtypeStruct((b, l, D_ATTN), BF16),
                   jax.ShapeDtypeStruct((b, l, D_ATTN), BF16)],
        scratch_shapes=[pltpu.VMEM((1, n), F32)],
        compiler_params=_params(("parallel", "arbitrary")),
    )(vals, bias, jnp.zeros((b, 1, n), F32), *placement)


_NT = (((1,), (1,)), ((), ()))


def _softmax_step(s, v, m_prev, l_prev, acc_prev):
    m_new = jnp.maximum(m_prev, jnp.max(s, axis=-1, keepdims=True))
    p = jnp.exp2(s - m_new)
    alpha = jnp.exp2(m_prev - m_new)
    l_new = alpha * l_prev + jnp.sum(p, axis=-1, keepdims=True)
    acc_new = alpha * acc_prev + jnp.dot(p.astype(BF16), v, preferred_element_type=F32)
    return m_new, l_new, acc_new


def _attn_kernel(q_ref, qx_ref, k_ref, kx_ref, v_ref, o_ref, m_scr, acc_scr, *, tile, heads):
    i = pl.program_id(2)
    dh = HEAD_DIM_A
    n_chunks = tile // LANES
    ones = jnp.ones((tile, dh), BF16)
    m_scr[...] = jnp.full(m_scr.shape, -jnp.inf, F32)
    acc_scr[...] = jnp.zeros(acc_scr.shape, F32)
    q2 = [jnp.concatenate([q_ref[0, :, hh * dh:(hh + 1) * dh], qx_ref[0, :, hh * dh:(hh + 1) * dh]], axis=1)
          for hh in range(heads)]

    def update(j, mask):
        start = pl.multiple_of(j * tile, tile)
        rows = pl.ds(start, tile)
        for hh in range(heads):
            sl = slice(hh * dh, (hh + 1) * dh)
            k2 = jnp.concatenate([k_ref[0, rows, sl], kx_ref[0, rows, sl]], axis=1)
            v2 = jnp.concatenate([v_ref[0, rows, sl], ones], axis=1)
            s = lax.dot_general(q2[hh], k2, _NT, preferred_element_type=F32)
            if mask is not None:
                s = jnp.where(mask, s, -jnp.inf)
            chunks = [s[:, c * LANES:(c + 1) * LANES] for c in range(n_chunks)]
            cmax = functools.reduce(jnp.maximum, chunks)
            m_prev = m_scr[hh]
            m_new = jnp.maximum(m_prev, jnp.max(cmax, axis=-1, keepdims=True))
            alpha = jnp.exp2(m_prev - m_new)
            p = jnp.concatenate([jnp.exp2(c - m_new) for c in chunks], axis=1).astype(BF16)
            pv = jnp.dot(p, v2, preferred_element_type=F32)
            m_scr[hh] = m_new
            acc_scr[hh, :, :dh] = alpha * acc_scr[hh, :, :dh] + pv[:, :dh]
            acc_scr[hh, :, dh:] = alpha * acc_scr[hh, :, dh:] + pv[:, dh:]

    def pair_body(jj, carry):
        update(2 * jj, None)
        update(2 * jj + 1, None)
        return carry

    lax.fori_loop(0, i // 2, pair_body, 0)
    row = lax.broadcasted_iota(jnp.int32, (tile, tile), 0)
    col = lax.broadcasted_iota(jnp.int32, (tile, tile), 1)
    odd = lax.rem(i, 2) == 1

    @pl.when(odd)
    def _():
        update(i - 1, None)
        update(i, col <= row)

    @pl.when(jnp.logical_not(odd))
    def _():
        update(i, col <= row)
    for hh in range(heads):
        o_ref[0, :, hh * dh:(hh + 1) * dh] = (acc_scr[hh, :, :dh] / acc_scr[hh, :, dh:]).astype(o_ref.dtype)


def _attention(q, qx, k, kx, v):
    b, l, _ = q.shape
    tile = ATT_TILE if l >= 2 * ATT_TILE else LANES
    assert l % tile == 0
    heads = ATT_HEADS_PER_STEP
    w = heads * HEAD_DIM_A
    qspec = pl.BlockSpec((1, tile, w), lambda bi, h, i: (bi, i, h))
    kspec = pl.BlockSpec((1, l, w), lambda bi, h, i: (bi, 0, h))
    return pl.pallas_call(
        functools.partial(_attn_kernel, tile=tile, heads=heads),
        grid=(b, N_HEADS_A // heads, l // tile),
        in_specs=[qspec, qspec, kspec, kspec, kspec],
        out_specs=qspec,
        out_shape=jax.ShapeDtypeStruct((b, l, D_ATTN), BF16),
        scratch_shapes=[pltpu.VMEM((heads, tile, LANES), F32), pltpu.VMEM((heads, tile, 2 * HEAD_DIM_A), F32)],
        compiler_params=_params(("parallel", "parallel", "arbitrary")),
    )(q, qx, k, kx, v)


def _attn_cached_kernel(q_ref, kn_ref, vn_ref, kc_ref, vc_ref, cumn_ref, cumtp_ref, cumtn_ref, o_ref,
                        m_scr, l_scr, acc_scr):
    j = pl.program_id(1)
    ld = q_ref.shape[1]
    dh = HEAD_DIM_A

    @pl.when(j == 0)
    def _():
        m_scr[...] = jnp.full(m_scr.shape, -jnp.inf, F32)
        l_scr[...] = jnp.zeros(l_scr.shape, F32)
        acc_scr[...] = jnp.zeros(acc_scr.shape, F32)

    def head_step(h, kh, vh, ck, mask):
        sl = slice(h * dh, (h + 1) * dh)
        s = lax.dot_general(q_ref[0, :, sl], kh, _NT, preferred_element_type=F32)
        s = s + (cumn_ref[0, :ld, h:h + 1] - ck) * LOG2E
        if mask is not None:
            s = jnp.where(mask, s, -jnp.inf)
        m, l, acc = _softmax_step(s, vh, m_scr[h], l_scr[h], acc_scr[:, sl])
        m_scr[h] = m
        l_scr[h] = l
        acc_scr[:, sl] = acc

    for h in range(N_HEADS_A):
        sl = slice(h * dh, (h + 1) * dh)
        rows = pl.ds(h, kc_ref.shape[1] // N_HEADS_A, stride=N_HEADS_A)
        head_step(h, kc_ref[0, rows, :].astype(BF16), vc_ref[0, rows, :].astype(BF16),
                  cumtp_ref[0, h:h + 1, :], None)

    @pl.when(j == pl.num_programs(1) - 1)
    def _():
        row = lax.broadcasted_iota(jnp.int32, (ld, ld), 0)
        col = lax.broadcasted_iota(jnp.int32, (ld, ld), 1)
        for h in range(N_HEADS_A):
            sl = slice(h * dh, (h + 1) * dh)
            head_step(h, kn_ref[0, :, sl], vn_ref[0, :, sl], cumtn_ref[0, h:h + 1, :ld], col <= row)
            o_ref[0, :, sl] = (acc_scr[:, sl] / l_scr[h]).astype(o_ref.dtype)


def _attention_cached(q, kn, vn, kc, vc, cum_new, cumt_past, cumt_new):
    b, ld, _ = q.shape
    lp = kc.shape[1]
    tk = min(DEC_KV_TILE, lp)
    assert lp % tk == 0
    lpad = cum_new.shape[1]
    new = pl.BlockSpec((1, ld, D_ATTN), lambda bi, j: (bi, 0, 0))
    past = pl.BlockSpec((1, tk * N_HEADS_A, HEAD_DIM_A), lambda bi, j: (bi, j, 0))
    kc = kc.reshape(b, lp * N_HEADS_A, HEAD_DIM_A)
    vc = vc.reshape(b, lp * N_HEADS_A, HEAD_DIM_A)
    return pl.pallas_call(
        _attn_cached_kernel,
        grid=(b, lp // tk),
        in_specs=[new, new, new, past, past,
                  pl.BlockSpec((1, lpad, LANES), lambda bi, j: (bi, 0, 0)),
                  pl.BlockSpec((1, N_HEADS_A, tk), lambda bi, j: (bi, 0, j)),
                  pl.BlockSpec((1, N_HEADS_A, lpad), lambda bi, j: (bi, 0, 0))],
        out_specs=new,
        out_shape=jax.ShapeDtypeStruct((b, ld, D_ATTN), BF16),
        scratch_shapes=[pltpu.VMEM((N_HEADS_A, ld, 1), F32), pltpu.VMEM((N_HEADS_A, ld, 1), F32),
                        pltpu.VMEM((ld, D_ATTN), F32)],
        compiler_params=_params(("parallel", "arbitrary")),
    )(q, kn, vn, kc, vc, cum_new, cumt_past, cumt_new)


def _expand_heads(a):
    r = a.shape[0]
    low = lax.broadcasted_iota(jnp.int32, (r, LANES), 1) < HEAD_DIM_S
    return jnp.concatenate(
        [jnp.where(low, a[:, 2 * j:2 * j + 1], a[:, 2 * j + 1:2 * j + 2]) for j in range(N_HEADS_S // 2)], axis=1)


CONV_HIST = SUBLANES


def _ssd_kernel(xbc_ref, z_ref, dt_ref, dtt_ref, past_ref, h0_ref, cw_ref, cb_ref, dtb_ref, dtbt_ref,
                alog_ref, alogt_ref, dsk_ref, nw_ref, o_ref, hout_ref, xbuf, ht_scr, *, lc):
    c = pl.program_id(1)
    lb = xbc_ref.shape[1]
    hist = CONV_HIST

    @pl.when(c == 0)
    def _():
        xbuf[0:hist, :] = past_ref[0]
        ht_scr[...] = h0_ref[0]

    xbuf[hist:hist + lb, :] = xbc_ref[0]
    u_all = cb_ref[...]
    for w in range(CONV_W):
        off = hist - (CONV_W - 1) + w
        u_all = u_all + xbuf[off:off + lb, :] * cw_ref[w:w + 1, :]
    xbuf[0:hist, :] = xbuf[lb:lb + hist, :]
    u_all = _silu(u_all)

    gn = N_GROUPS_S * D_STATE
    half = D_INNER // N_GROUPS_S
    heads_per_group = N_HEADS_S // N_GROUPS_S
    row = lax.broadcasted_iota(jnp.int32, (lc, lc), 0)
    col = lax.broadcasted_iota(jnp.int32, (lc, lc), 1)
    causal = col <= row
    tri_lower = causal.astype(BF16)
    tri_upper = (row <= col).astype(BF16)
    low = lax.broadcasted_iota(jnp.int32, (lc, LANES), 1) < HEAD_DIM_S
    a_head = -jnp.exp(alog_ref[...])
    a_head_t = -jnp.exp(alogt_ref[...])
    ht = ht_scr[...]

    for cc in range(lb // lc):
        rows = slice(cc * lc, (cc + 1) * lc)
        u = u_all[rows]
        xs = u[:, :D_INNER]
        bm = u[:, D_INNER:D_INNER + gn].astype(BF16)
        cm = u[:, D_INNER + gn:].astype(BF16)

        dt = _softplus(dt_ref[0, rows, :] + dtb_ref[...])
        dtt = _softplus(dtt_ref[0, :, rows] + dtbt_ref[...])
        a_cs = _tri_dot(tri_lower, dt * a_head)
        a_cst = _dot_tri(dtt * a_head_t, tri_upper)
        total = a_cs[lc - 1:lc, :]
        dt_e = _expand_heads(dt)
        w_e = _expand_heads(dt * jnp.exp(total - a_cs))
        ea_e = _expand_heads(jnp.exp(a_cs))
        cd_e = _expand_heads(jnp.exp(total))
        xdt = xs * dt_e
        xdw = (xs * w_e).astype(BF16)

        cbs = [lax.dot_general(cm[:, g * D_STATE:(g + 1) * D_STATE], bm[:, g * D_STATE:(g + 1) * D_STATE], _NT,
                               preferred_element_type=F32) for g in range(N_GROUPS_S)]
        yd = []
        for j in range(N_HEADS_S // 2):
            ms = []
            for hh in (2 * j, 2 * j + 1):
                seg = a_cs[:, hh:hh + 1] - a_cst[hh:hh + 1, :]
                dec = jnp.where(causal, jnp.exp(jnp.where(causal, seg, 0.0)), 0.0)
                ms.append((cbs[hh // heads_per_group] * dec).astype(BF16))
            xb = xdt[:, j * LANES:(j + 1) * LANES]
            rhs = jnp.concatenate([jnp.where(low, xb, 0.0), jnp.where(low, 0.0, xb)], axis=0).astype(BF16)
            yd.append(jnp.dot(jnp.concatenate(ms, axis=1), rhs, preferred_element_type=F32))
        y = jnp.concatenate(yd, axis=1)

        htb = ht.astype(BF16)
        y_off = jnp.concatenate(
            [jnp.dot(cm[:, g * D_STATE:(g + 1) * D_STATE], htb[:, g * half:(g + 1) * half],
                     preferred_element_type=F32) for g in range(N_GROUPS_S)], axis=1)
        st = jnp.concatenate(
            [lax.dot_general(bm[:, g * D_STATE:(g + 1) * D_STATE], xdw[:, g * half:(g + 1) * half],
                             (((0,), (0,)), ((), ())), preferred_element_type=F32) for g in range(N_GROUPS_S)], axis=1)
        ht = cd_e * ht + st

        y = y + y_off * ea_e + dsk_ref[...] * xs
        gz = y * _silu(z_ref[0, rows, :])
        outs = []
        for g in range(N_GROUPS_S):
            gg = gz[:, g * half:(g + 1) * half]
            outs.append(gg * lax.rsqrt(jnp.mean(gg * gg, axis=-1, keepdims=True) + EPS))
        o_ref[0, rows, :] = (jnp.concatenate(outs, axis=1) * nw_ref[...]).astype(o_ref.dtype)

    ht_scr[...] = ht
    hout_ref[0] = ht


def _ssd(xbc, z, dt_raw, conv_past, h0t, conv_w, conv_b, dt_bias, a_log, d_skip, norm_w):
    b, l, _ = xbc.shape
    lc = min(SSD_CHUNK, l)
    lb = lc * SSD_STEP_CHUNKS if l % (lc * SSD_STEP_CHUNKS) == 0 else lc
    assert l % lb == 0 and lc % CONV_HIST == 0
    nh = N_HEADS_S
    dtt = jnp.swapaxes(dt_raw, 1, 2)
    past = jnp.pad(conv_past, ((0, 0), (CONV_HIST - (CONV_W - 1), 0), (0, 0)))
    const = lambda shape: pl.BlockSpec(shape, lambda i, j: tuple(0 for _ in shape))
    o, hout = pl.pallas_call(
        functools.partial(_ssd_kernel, lc=lc),
        grid=(b, l // lb),
        in_specs=[pl.BlockSpec((1, lb, CONV_DIM), lambda i, j: (i, j, 0)),
                  pl.BlockSpec((1, lb, D_INNER), lambda i, j: (i, j, 0)),
                  pl.BlockSpec((1, lb, nh), lambda i, j: (i, j, 0)),
                  pl.BlockSpec((1, nh, lb), lambda i, j: (i, 0, j)),
                  pl.BlockSpec((1, CONV_HIST, CONV_DIM), lambda i, j: (i, 0, 0)),
                  pl.BlockSpec((1, D_STATE, D_INNER), lambda i, j: (i, 0, 0)),
                  const((CONV_W, CONV_DIM)), const((1, CONV_DIM)),
                  const((1, nh)), const((nh, 1)), const((1, nh)), const((nh, 1)),
                  const((1, D_INNER)), const((1, D_INNER))],
        out_specs=[pl.BlockSpec((1, lb, D_INNER), lambda i, j: (i, j, 0)),
                   pl.BlockSpec((1, D_STATE, D_INNER), lambda i, j: (i, 0, 0))],
        out_shape=[jax.ShapeDtypeStruct((b, l, D_INNER), BF16),
                   jax.ShapeDtypeStruct((b, D_STATE, D_INNER), F32)],
        scratch_shapes=[pltpu.VMEM((lb + CONV_HIST, CONV_DIM), F32), pltpu.VMEM((D_STATE, D_INNER), F32)],
        compiler_params=_params(("parallel", "arbitrary")),
    )(xbc, z, dt_raw, dtt, past, h0t, conv_w, conv_b.reshape(1, -1),
      dt_bias.reshape(1, nh), dt_bias.reshape(nh, 1), a_log.reshape(1, nh), a_log.reshape(nh, 1),
      jnp.repeat(d_skip, HEAD_DIM_S).reshape(1, -1), norm_w.reshape(1, -1))
    return o, hout


def _outproj_kernel(x_ref, oa_ref, os_ref, gt_ref, sc_ref, sh_ref, g_ref, wa_ref, ws_ref, rw_ref, rb_ref,
                    x1_ref, h2_ref, lg_ref):
    nb, tl, d = x_ref.shape
    oa = oa_ref[...].reshape(nb * tl, -1)
    os_ = os_ref[...].reshape(nb * tl, -1)
    mix = jnp.dot(oa, wa_ref[...], preferred_element_type=F32) + jnp.dot(os_, ws_ref[...], preferred_element_type=F32)
    x1 = x_ref[...] + gt_ref[...] * mix.reshape(nb, tl, d)
    x1_ref[...] = x1
    ms = jnp.mean(x1 * x1, axis=-1, keepdims=True)
    h2 = x1 * lax.rsqrt(ms + EPS) * g_ref[...]
    h2 = h2 * (1.0 + sc_ref[...]) + sh_ref[...]
    h2 = h2.reshape(nb * tl, d)
    _store_row_tiles(h2_ref, h2)
    h_hi, h_lo = _split2(h2)
    w_hi, w_lo = rw_ref[0], rw_ref[1]
    lg = (jnp.dot(h_hi, w_hi, preferred_element_type=F32) + jnp.dot(h_lo, w_hi, preferred_element_type=F32)
          + jnp.dot(h_hi, w_lo, preferred_element_type=F32)) + rb_ref[...]
    lg_ref[...] = lg.reshape(nb, tl, -1)


def _outproj(x, oa, os_, gt, sc, sh, g, wa, ws, rw, rb):
    b, l, d = x.shape
    nb, tl = _row_blocking(b, l, ROW_TILE)
    row = lambda n: pl.BlockSpec((nb, tl, n), lambda i, j: (i, j, 0))
    mod = pl.BlockSpec((nb, 1, d), lambda i, j: (i, 0, 0))
    const = lambda a: pl.BlockSpec(a.shape, lambda i, j: tuple(0 for _ in a.shape))
    g3 = g.reshape(1, 1, d)
    return pl.pallas_call(
        _outproj_kernel,
        grid=(b // nb, l // tl),
        in_specs=[row(d), row(D_ATTN), row(D_INNER), mod, mod, mod, const(g3), const(wa), const(ws),
                  const(rw), const(rb)],
        out_specs=[row(d), pl.BlockSpec((nb * tl * ROW_TILE_SUBLANES, LANES), lambda i, j: (i * (l // tl) + j, 0)),
                   row(LANES)],
        out_shape=[jax.ShapeDtypeStruct((b, l, d), F32), jax.ShapeDtypeStruct((b * l * ROW_TILE_SUBLANES, LANES), F32),
                   jax.ShapeDtypeStruct((b, l, LANES), F32)],
        compiler_params=_params(("parallel", "parallel")),
    )(x, oa, os_, gt.reshape(b, 1, d), sc.reshape(b, 1, d), sh.reshape(b, 1, d), g3, wa, ws, rw, rb)


def _route_kernel(lg_ref, idx_ref, gate_ref, cnt_ref, carry_scr):
    @pl.when(pl.program_id(0) == 0)
    def _():
        carry_scr[...] = jnp.zeros(carry_scr.shape, F32)

    v = lg_ref[...]
    tr = v.shape[0]
    lane = lax.broadcasted_iota(jnp.int32, (tr, LANES), 1)
    lane_f = lane.astype(F32)
    tops, idxs = [], []
    onehot = jnp.zeros((tr, LANES), F32)
    for _ in range(TOP_K):
        m = jnp.max(v, axis=-1, keepdims=True)
        idx = jnp.min(jnp.where(v == m, lane_f, float(LANES)), axis=-1, keepdims=True)
        hit = lane_f == idx
        v = jnp.where(hit, -jnp.inf, v)
        onehot = onehot + hit.astype(F32)
        tops.append(m)
        idxs.append(idx)
    es = [jnp.exp(t - tops[0]) for t in tops]
    denom = es[0] + es[1] + es[2] + es[3]

    row = lax.broadcasted_iota(jnp.int32, (tr, tr), 0)
    col = lax.broadcasted_iota(jnp.int32, (tr, tr), 1)
    before = jnp.dot((col < row).astype(BF16), onehot.astype(BF16), preferred_element_type=F32) + carry_scr[...]

    idx_out = jnp.zeros((tr, LANES), F32)
    gate_out = jnp.zeros((tr, LANES), F32)
    for k in range(TOP_K):
        rank = jnp.sum(jnp.where(lane_f == idxs[k], before, 0.0), axis=-1, keepdims=True)
        idx_out = jnp.where(lane == k, idxs[k], idx_out)
        idx_out = jnp.where(lane == TOP_K + k, rank, idx_out)
        gate_out = jnp.where(lane == k, es[k] / denom, gate_out)
    idx_ref[...] = idx_out.astype(jnp.int32)
    gate_ref[...] = gate_out
    carry_scr[...] = carry_scr[...] + jnp.sum(onehot, axis=0, keepdims=True)
    cnt_ref[...] = carry_scr[...]


def _route(logits):
    t, n = logits.shape
    tr = min(ROUTE_TILE, t)
    assert t % tr == 0
    return pl.pallas_call(
        _route_kernel,
        grid=(t // tr,),
        in_specs=[pl.BlockSpec((tr, n), lambda i: (i, 0))],
        out_specs=[pl.BlockSpec((tr, n), lambda i: (i, 0)), pl.BlockSpec((tr, n), lambda i: (i, 0)),
                   pl.BlockSpec((1, n), lambda i: (0, 0))],
        out_shape=[jax.ShapeDtypeStruct((t, n), jnp.int32), jax.ShapeDtypeStruct((t, n), F32),
                   jax.ShapeDtypeStruct((1, n), F32)],
        scratch_shapes=[pltpu.VMEM((1, n), F32)],
        compiler_params=_params(("arbitrary",)),
    )(logits)


def _row_tile(ref, r):
    start = r * ROW_TILE_SUBLANES
    if not isinstance(r, int):
        start = pl.multiple_of(start, ROW_TILE_SUBLANES)
    return ref.at[pl.ds(start, ROW_TILE_SUBLANES), :]


def _store_row_tiles(ref, x, tile0=0, row0=0):
    rows = x.shape[0]
    for s in range(x.shape[1] // LANES):
        ref[pl.ds(row0 * ROW_TILE_SUBLANES + tile0 + s, rows, stride=ROW_TILE_SUBLANES), :] = x[:, s * LANES:(s + 1) * LANES]


def _load_row_tiles(ref, row0, rows):
    return jnp.concatenate(
        [ref[pl.ds(row0 * ROW_TILE_SUBLANES + s, rows, stride=ROW_TILE_SUBLANES), :] for s in range(ROW_TILE_SUBLANES)],
        axis=1)


def _gather_rows(idx_ref, src_hbm, dst, sem, n_rows):
    group = SUBLANES
    assert n_rows % group == 0

    def body(g, carry):
        for u in range(group):
            r = g * group + u
            pltpu.make_async_copy(_row_tile(src_hbm, idx_ref[0, 0, r]), _row_tile(dst, r), sem).start(priority=u % 2)
        return carry
    lax.fori_loop(0, n_rows // group, body, 0)


def _wait_rows(src_hbm, dst, sem, n_rows):
    pltpu.make_async_copy(src_hbm.at[pl.ds(0, n_rows * ROW_TILE_SUBLANES), :], dst, sem).wait()


EXPERT_COL_CHUNK = 256


GATHER_BUFS = 3

def _dispatch_kernel(ends_ref, pad_ref, dest_ref, h_hbm, xs_hbm, zbuf, hbuf, zsem, fsem, sem, *, n_slots):
    i = pl.program_id(0)
    n = pl.num_programs(0)
    m = zbuf.shape[0] // ROW_TILE_SUBLANES
    rows = dest_ref.shape[2]
    td = rows // TOP_K
    slot = lax.rem(i, GATHER_BUFS)

    def fetch(step, s):
        start = pl.multiple_of(step * (td * ROW_TILE_SUBLANES), td * ROW_TILE_SUBLANES)
        return pltpu.make_async_copy(h_hbm.at[pl.ds(start, td * ROW_TILE_SUBLANES), :], hbuf.at[s], fsem.at[s])

    def wait_rows_of(s):
        for _ in range(TOP_K):
            pltpu.make_async_copy(hbuf.at[s], xs_hbm.at[pl.ds(0, td * ROW_TILE_SUBLANES), :], sem.at[s]).wait()

    def zero_block(first_slot):
        start = pl.multiple_of(first_slot * ROW_TILE_SUBLANES, ROW_TILE_SUBLANES)
        return pltpu.make_async_copy(zbuf, xs_hbm.at[pl.ds(start, zbuf.shape[0]), :], zsem)

    @pl.when(i == 0)
    def _():
        zbuf[...] = jnp.zeros(zbuf.shape, zbuf.dtype)
        jobs = [(pad_ref[e] > 0, ends_ref[e] - m) for e in range(N_EXPERTS)]
        jobs += [(ends_ref[N_EXPERTS - 1] + j * m < n_slots, ends_ref[N_EXPERTS - 1] + j * m)
                 for j in range(N_EXPERTS + 1)]
        for cond, first in jobs:
            @pl.when(cond)
            def _():
                zero_block(first).start()
        for cond, first in jobs:
            @pl.when(cond)
            def _():
                zero_block(first).wait()
        fetch(0, 0).start()

    @pl.when(i >= 2)
    def _():
        wait_rows_of(lax.rem(i + 1, GATHER_BUFS))

    @pl.when(i + 1 < n)
    def _():
        fetch(i + 1, lax.rem(i + 1, GATHER_BUFS)).start()

    fetch(i, slot).wait()
    for s in range(GATHER_BUFS):
        @pl.when(slot == s)
        def _():
            for j in range(rows):
                pltpu.make_async_copy(_row_tile(hbuf.at[s], j // TOP_K), _row_tile(xs_hbm, dest_ref[0, 0, j]),
                                      sem.at[s]).start(priority=j % 2)

    @pl.when(i == n - 1)
    def _():
        @pl.when(i >= 1)
        def _():
            wait_rows_of(lax.rem(i + 2, GATHER_BUFS))
        wait_rows_of(slot)


def _dispatch(dest, ends_p, padded, h2, m, n_slots):
    t = dest.shape[0]
    td = min(DISPATCH_TILE, t)
    assert t % td == 0
    n_steps = t // td
    grid_spec = pltpu.PrefetchScalarGridSpec(
        num_scalar_prefetch=2,
        grid=(n_steps,),
        in_specs=[pl.BlockSpec((1, 1, TOP_K * td), lambda i, e, p: (i, 0, 0), memory_space=pltpu.SMEM),
                  pl.BlockSpec(memory_space=pl.ANY)],
        out_specs=pl.BlockSpec(memory_space=pl.ANY),
        scratch_shapes=[pltpu.VMEM((m * ROW_TILE_SUBLANES, LANES), F32),
                        pltpu.VMEM((GATHER_BUFS, td * ROW_TILE_SUBLANES, LANES), F32),
                        pltpu.SemaphoreType.DMA(()), pltpu.SemaphoreType.DMA((GATHER_BUFS,)),
                        pltpu.SemaphoreType.DMA((GATHER_BUFS,))],
    )
    return pl.pallas_call(
        functools.partial(_dispatch_kernel, n_slots=n_slots),
        grid_spec=grid_spec,
        out_shape=jax.ShapeDtypeStruct((n_slots * ROW_TILE_SUBLANES, LANES), F32),
        compiler_params=_params(("arbitrary",)),
    )(ends_p, padded, dest.reshape(n_steps, 1, TOP_K * td), h2)


EXPERT_BLOCKS_PER_STEP = 2


def _experts_kernel(be_ref, used_ref, x_ref, *refs):
    o_ref = refs[-1]
    weights = [refs[4 * j:4 * j + 4] for j in range(EXPERT_BLOCKS_PER_STEP)]
    m = x_ref.shape[0] // (ROW_TILE_SUBLANES * EXPERT_BLOCKS_PER_STEP)
    cw = EXPERT_COL_CHUNK
    tiles_per_piece = cw // LANES
    n_ff, n_out = D_FF // cw, ROW_TILE_SUBLANES // tiles_per_piece
    in_use = pl.program_id(0) * EXPERT_BLOCKS_PER_STEP < used_ref[0]

    @pl.when(in_use)
    def _():
        for j, (wgu_ref, bgu_ref, wdn_ref, bdn_ref) in enumerate(weights):
            x = _load_row_tiles(x_ref, j * m, m).astype(BF16)
            acts = []
            for c in range(n_ff):
                g = (jnp.dot(x, wgu_ref[0, :, c * cw:(c + 1) * cw], preferred_element_type=F32)
                     + bgu_ref[0, :, c * cw:(c + 1) * cw])
                u = (jnp.dot(x, wgu_ref[0, :, D_FF + c * cw:D_FF + (c + 1) * cw], preferred_element_type=F32)
                     + bgu_ref[0, :, D_FF + c * cw:D_FF + (c + 1) * cw])
                gate = jnp.minimum(g, SWIGLU_LIMIT)
                up = jnp.clip(u, -SWIGLU_LIMIT, SWIGLU_LIMIT)
                acts.append(((up + 1.0) * gate * jax.nn.sigmoid(SWIGLU_ALPHA * gate)).astype(BF16))
            act = jnp.concatenate(acts, axis=1)
            for c in range(n_out):
                y = (jnp.dot(act, wdn_ref[0, :, c * cw:(c + 1) * cw], preferred_element_type=F32)
                     + bdn_ref[0, :, c * cw:(c + 1) * cw])
                _store_row_tiles(o_ref, y, tile0=c * tiles_per_piece, row0=j * m)

    @pl.when(jnp.logical_not(in_use))
    def _():
        o_ref[...] = jnp.zeros(o_ref.shape, o_ref.dtype)


def _experts(block_e, n_used, xs, wgu, bgu, wdn, bdn, m):
    n_blocks = block_e.shape[0]
    per = EXPERT_BLOCKS_PER_STEP
    assert n_blocks % per == 0
    d = D_MODEL
    mt = m * ROW_TILE_SUBLANES * per
    weight_specs = []
    for j in range(per):
        expert = lambda i, be, nu, j=j: (be[i * per + j], 0, 0)
        weight_specs += [pl.BlockSpec((1, d, 2 * D_FF), expert), pl.BlockSpec((1, 1, 2 * D_FF), expert),
                         pl.BlockSpec((1, D_FF, d), expert), pl.BlockSpec((1, 1, d), expert)]
    grid_spec = pltpu.PrefetchScalarGridSpec(
        num_scalar_prefetch=2,
        grid=(n_blocks // per,),
        in_specs=[pl.BlockSpec((mt, LANES), lambda i, be, nu: (i, 0))] + weight_specs,
        out_specs=pl.BlockSpec((mt, LANES), lambda i, be, nu: (i, 0)),
    )
    weights = (wgu, bgu.reshape(N_EXPERTS, 1, -1), wdn, bdn.reshape(N_EXPERTS, 1, -1))
    return pl.pallas_call(
        _experts_kernel,
        grid_spec=grid_spec,
        out_shape=jax.ShapeDtypeStruct((n_blocks * m * ROW_TILE_SUBLANES, LANES), F32),
        compiler_params=_params(("arbitrary",)),
    )(block_e, n_used, xs, *(weights * per))


def _combine_kernel(idx0_ref, idx1_ref, idxn_ref, y_hbm, x1_ref, gate_ref, gt_ref, g_ref, o_ref, ybuf, sem):
    i = pl.program_id(0)
    n = pl.num_programs(0)
    nb, tl, d = x1_ref.shape
    tc = nb * tl
    slot = lax.rem(i, GATHER_BUFS)
    nxt = lax.rem(i + 2, GATHER_BUFS)

    @pl.when(i == 0)
    def _():
        _gather_rows(idx0_ref, y_hbm, ybuf.at[0], sem.at[0], TOP_K * tc)
        _gather_rows(idx1_ref, y_hbm, ybuf.at[1], sem.at[1], TOP_K * tc)

    _wait_rows(y_hbm, ybuf.at[slot], sem.at[slot], TOP_K * tc)

    def prefetch(k):
        for r in range(k * tc, (k + 1) * tc):
            pltpu.make_async_copy(_row_tile(y_hbm, idxn_ref[0, 0, r]), _row_tile(ybuf.at[nxt], r),
                                  sem.at[nxt]).start(priority=r % 2)

    gates = gate_ref[...]
    y = gates[:, 0:1] * _load_row_tiles(ybuf.at[slot], 0, tc)
    prefetch(0)
    for k in range(1, TOP_K):
        y = y + gates[:, k:k + 1] * _load_row_tiles(ybuf.at[slot], k * tc, tc)
        prefetch(k)
    x2 = x1_ref[...] + gt_ref[...] * y.reshape(nb, tl, d)
    ms = jnp.mean(x2 * x2, axis=-1, keepdims=True)
    o_ref[...] = x2 * lax.rsqrt(ms + EPS) * g_ref[...]

    @pl.when(i == n - 1)
    def _():
        for ahead in (1, 2):
            s = lax.rem(i + ahead, GATHER_BUFS)
            _wait_rows(y_hbm, ybuf.at[s], sem.at[s], TOP_K * tc)


def _combine(dest, y_slots, x1, gates, gt, g_final):
    b, l, d = x1.shape
    nb, tl = _row_blocking(b, l, COMBINE_TILE)
    tc = nb * tl
    t = b * l
    n_steps = t // tc
    idx = jnp.swapaxes(dest.reshape(n_steps, tc, TOP_K), 1, 2).reshape(n_steps, 1, TOP_K * tc)
    steps_per_batch_row = l // tl
    tok = lambda i: (i // steps_per_batch_row, i % steps_per_batch_row, 0)
    smem = lambda f: pl.BlockSpec((1, 1, TOP_K * tc), f, memory_space=pltpu.SMEM)
    return pl.pallas_call(
        _combine_kernel,
        grid=(n_steps,),
        in_specs=[smem(lambda i: (0, 0, 0)),
                  smem(lambda i: (min(1, n_steps - 1), 0, 0)),
                  smem(lambda i: (jnp.minimum(i + 2, n_steps - 1), 0, 0)),
                  pl.BlockSpec(memory_space=pl.ANY),
                  pl.BlockSpec((nb, tl, d), tok),
                  pl.BlockSpec((tc, LANES), lambda i: (i, 0)),
                  pl.BlockSpec((nb, 1, d), lambda i: (i // steps_per_batch_row, 0, 0)),
                  pl.BlockSpec((1, 1, d), lambda i: (0, 0, 0))],
        out_specs=pl.BlockSpec((nb, tl, d), tok),
        out_shape=jax.ShapeDtypeStruct((b, l, d), F32),
        scratch_shapes=[pltpu.VMEM((GATHER_BUFS, TOP_K * tc * ROW_TILE_SUBLANES, LANES), F32),
                        pltpu.SemaphoreType.DMA((GATHER_BUFS,))],
        compiler_params=_params(("arbitrary",)),
    )(idx, idx, idx, y_slots, x1, gates, gt.reshape(b, 1, d), g_final.reshape(1, 1, d))


def _moe(h2, logits, x1, gt_m, g_final, wts):
    b, l, d = x1.shape
    t = b * l
    m = MOE_ROWS if t * TOP_K >= 4 * N_EXPERTS * MOE_ROWS else MOE_ROWS_SMALL
    idx, gates, counts = _route(logits.reshape(t, LANES))
    top_e = idx[:, :TOP_K]
    rank = idx[:, TOP_K:2 * TOP_K]
    counts = counts[0, :N_EXPERTS].astype(jnp.int32)
    padded = (counts + m - 1) // m * m
    ends_p = jnp.cumsum(padded)
    starts_p = ends_p - padded
    experts = jnp.arange(N_EXPERTS, dtype=jnp.int32)
    start_of = jnp.sum(jnp.where(top_e[..., None] == experts, starts_p, 0), axis=-1)
    dest = start_of + rank
    n_blocks = (t * TOP_K + N_EXPERTS * (m - 1) + m - 1) // m
    n_blocks += -n_blocks % EXPERT_BLOCKS_PER_STEP
    block_start = jnp.arange(n_blocks, dtype=jnp.int32)[:, None] * m
    block_e = jnp.minimum(jnp.sum((ends_p[None, :] <= block_start).astype(jnp.int32), axis=1), N_EXPERTS - 1)
    xs = _dispatch(dest, ends_p, padded, h2, m, n_blocks * m)
    n_used = (ends_p[N_EXPERTS - 1:] // m).astype(jnp.int32)
    y_slots = _experts(block_e, n_used, xs, *wts, m)
    return _combine(dest, y_slots, x1, gates, gt_m, g_final)


def _layer(x, mod, k_past, v_past, logf_past, conv_past, ssm_past, p, g_final):
    b, l, d = x.shape
    sh_a, sc_a, gt_a, sh_m, sc_m, gt_m = jnp.split(mod, 6, axis=-1)
    q, kf, vf, kb, vb, z, xbc, sm = _inproj(x, p['g_mix'], sc_a, sh_a, *p['w_in'])
    dt_raw = sm[:, :, N_HEADS_A:N_HEADS_A + N_HEADS_S]

    if k_past is None:
        logf, qx, kx = _cumsum_bias(sm, p['b_f'])
        o_a = _attention(q, qx, kb, kx, vb)
        conv_past = jnp.zeros((b, CONV_W - 1, CONV_DIM), F32)
        h0t = jnp.zeros((b, D_STATE, D_INNER), F32)
    else:
        lp = k_past.shape[1]
        past_t = jnp.swapaxes(logf_past, 1, 2).reshape(b * N_HEADS_A, lp)
        cumt_p = _cumsum_lanes(past_t).reshape(b, N_HEADS_A, lp)
        carry = jnp.pad(cumt_p[:, :, lp - 1], ((0, 0), (0, LANES - N_HEADS_A))).reshape(b, 1, LANES)
        lpad = -(-l // LANES) * LANES
        sm_pad = jnp.pad(sm, ((0, 0), (0, lpad - l), (0, 0)))
        logf, cum_n, cumt_n = _cumsum(sm_pad, p['b_f'], carry)
        logf = logf[:, :l]
        o_a = _attention_cached(q, kb, vb, k_past, v_past, cum_n, cumt_p, cumt_n)
        h0t = jnp.swapaxes(ssm_past.reshape(b, D_INNER, D_STATE), 1, 2)
    o_s, hout = _ssd(xbc, z, dt_raw, conv_past, h0t, p['conv_w'], p['conv_b'], p['dt_bias'], p['a_log'],
                     p['d_skip'], p['ssd_norm_w'])
    x1, h2, logits = _outproj(x, o_a, o_s, gt_a, sc_m, sh_m, p['g_ffn'], p['w_out_a'], p['w_out_s'],
                              p['router_w'], p['router_b'])
    y = _moe(h2, logits, x1, gt_m, g_final, p['experts'])

    assert l >= CONV_W - 1
    conv_new = xbc[:, l - (CONV_W - 1):]
    ssm_new = jnp.swapaxes(hout, 1, 2).reshape(b, N_HEADS_S, HEAD_DIM_S, D_STATE)
    return (y, kf.reshape(b, l, N_HEADS_A, HEAD_DIM_A), vf.reshape(b, l, N_HEADS_A, HEAD_DIM_A),
            logf[:, :, :N_HEADS_A], conv_new, ssm_new)


def kernel(x_prompt, x_sample, cache_k, cache_v, cache_logf, state_conv, state_ssm, c_prompt, c_sample, w_ada, b_ada, g_mix, w_in, b_f, conv_w, conv_b, dt_bias, a_log, d_skip, ssd_norm_w, w_out, g_ffn, router_w, router_b, w_gate_up, b_gate_up, w_down, b_down, g_final):
    assert w_ada.shape[0] == 1, "single-layer operation"
    bp = x_prompt.shape[0]
    w = w_in[0]
    q_end, k_end, v_end = D_ATTN, 2 * D_ATTN, 3 * D_ATTN
    f_end = v_end + N_HEADS_A
    z_end = f_end + D_INNER
    xbc_end = z_end + CONV_DIM
    w_small = jnp.concatenate(
        [w[:, v_end:f_end], w[:, xbc_end:], jnp.zeros((D_MODEL, LANES - N_HEADS_A - N_HEADS_S), F32)], axis=1)
    cast = lambda a: a.astype(BF16)
    p = {
        'g_mix': g_mix[0],
        'w_in': (cast(w[:, :q_end]), cast(w[:, q_end:k_end]), cast(w[:, k_end:v_end]), cast(w[:, f_end:z_end]),
                 cast(w[:, z_end:xbc_end]), cast(w_small)),
        'b_f': jnp.pad(b_f[0], (0, LANES - N_HEADS_A)).reshape(1, LANES),
        'conv_w': conv_w[0], 'conv_b': conv_b[0], 'dt_bias': dt_bias[0], 'a_log': a_log[0],
        'd_skip': d_skip[0], 'ssd_norm_w': ssd_norm_w[0],
        'w_out_a': cast(w_out[0, :D_ATTN]), 'w_out_s': cast(w_out[0, D_ATTN:]),
        'g_ffn': g_ffn[0],
        'router_w': jnp.stack(_split2(jnp.pad(router_w[0], ((0, 0), (0, LANES - N_EXPERTS))))),
        'router_b': jnp.pad(router_b[0], (0, LANES - N_EXPERTS), constant_values=-jnp.inf).reshape(1, LANES),
        'experts': (cast(w_gate_up[0]), b_gate_up[0], cast(w_down[0]), b_down[0]),
    }
    mod = _adaln(jnp.concatenate([c_prompt, c_sample], axis=0), w_ada[0], b_ada[0])
    outs_p = _layer(x_prompt, mod[:bp], None, None, None, None, None, p, g_final)
    outs_s = _layer(x_sample, mod[bp:], cache_k[0], cache_v[0], cache_logf[0], state_conv[0], state_ssm[0], p, g_final)
    stack = lambda a: a[None]
    return (outs_p[0], outs_s[0]) + tuple(stack(a) for a in outs_p[1:]) + tuple(stack(a) for a in outs_s[1:])
```
